```python
import jax, jax.numpy as jnp
from jax import lax
import numpy as np

D_MODEL = 1024
BATCH = 16
SEQ = 256
DEPTH = 2
DEC_BATCH = 8
DEC_SEQ = 2048
PAST_LEN = 256

GRID_W = 64
EPS = 1e-6
NEG_INF = -1e30
N_BRANCH = 3
MIX_W = D_MODEL // 2
M_HEADS = 4
M_DH = MIX_W // M_HEADS
M_CHUNK = 128
C_GROUPS = 4
C_DG = MIX_W // C_GROUPS
C_CHUNK = 128
A_HEADS = 8
A_KV = 2
A_GROUP = A_HEADS // A_KV
A_DH = MIX_W // A_HEADS
A_KVW = A_KV * A_DH
WINDOW = 128
A_BLOCK = 128
ROPE_BASE = 10000.0
ROPE_AXIS = A_DH // 2
ROPE_FREQS = ROPE_AXIS // 2
N_KEYS = 128
N_EXPERTS = N_KEYS * N_KEYS
PEER_HEADS = 8
PEER_DK = 256
PEER_TOPK = 16
PEER_BLOCK = 128
IN_W = 7 * MIX_W + 4 * M_HEADS + 2 * A_KVW

kernel_name = 'hybrid_mlstm_gmlp_swa_peer_diffusion_step'


def rms_norm(x, g=None):
    xf = x.astype(jnp.float32)
    y = xf * lax.rsqrt(jnp.mean(xf * xf, axis=-1, keepdims=True) + EPS)
    if g is not None:
        y = y * g.astype(jnp.float32)
    return y.astype(x.dtype)


def axial_rope(x):
    n = x.shape[1]
    rows = n // GRID_W
    t_row = jnp.repeat(jnp.arange(rows), GRID_W)
    t_col = jnp.tile(jnp.arange(GRID_W), rows)
    freqs = ROPE_BASE ** (-jnp.arange(ROPE_FREQS, dtype=jnp.float32) / ROPE_FREQS)

    def rot(xh, pos):
        ang = pos.astype(jnp.float32)[:, None] * freqs[None, :]
        cos = jnp.cos(ang)[None, :, None, :]
        sin = jnp.sin(ang)[None, :, None, :]
        x1, x2 = xh[..., :ROPE_FREQS], xh[..., ROPE_FREQS:]
        return jnp.concatenate([x1 * cos - x2 * sin, x2 * cos + x1 * sin], axis=-1)

    xf = x.astype(jnp.float32)
    out = jnp.concatenate([rot(xf[..., :ROPE_AXIS], t_row), rot(xf[..., ROPE_AXIS:], t_col)], axis=-1)
    return out.astype(x.dtype)


def _attend(qb, k_ctx, v_ctx, sink, k_loc=None, v_loc=None, mask=None):
    f32 = jnp.float32
    scale = A_DH ** -0.5
    B, Q = qb.shape[0], qb.shape[1]
    lc = k_ctx.shape[1]
    sink_col = jnp.broadcast_to(sink.astype(f32).reshape(1, A_KV, A_GROUP, 1, 1), (B, A_KV, A_GROUP, Q, 1))
    s_ctx = jnp.einsum('bqkgd,bjkd->bkgqj', qb, k_ctx, preferred_element_type=f32) * scale
    parts = [sink_col, s_ctx]
    if k_loc is not None:
        s_loc = jnp.einsum('bqkgd,bjkd->bkgqj', qb, k_loc, preferred_element_type=f32) * scale
        parts.append(jnp.where(mask[None, None, None], s_loc, NEG_INF))
    p = jax.nn.softmax(jnp.concatenate(parts, axis=-1), axis=-1)
    out = jnp.einsum('bkgqj,bjkd->bqkgd', p[..., 1:1 + lc], v_ctx.astype(f32))
    if k_loc is not None:
        out = out + jnp.einsum('bkgqj,bjkd->bqkgd', p[..., 1 + lc:], v_loc.astype(f32))
    return out.astype(qb.dtype)


def context_attention(q, k, v, sink):
    B, L = q.shape[:2]
    nb = L // A_BLOCK
    qb = q.reshape(B, nb, A_BLOCK, A_KV, A_GROUP, A_DH).swapaxes(0, 1)
    out = lax.map(lambda qi: _attend(qi, k, v, sink), qb)
    return out.swapaxes(0, 1).reshape(B, L, MIX_W)


def latent_attention(q, k, v, k_ctx, v_ctx, sink):
    B, S = q.shape[:2]
    nb = S // A_BLOCK
    qb = q.reshape(B, nb, A_BLOCK, A_KV, A_GROUP, A_DH).swapaxes(0, 1)
    pad = ((0, 0), (A_BLOCK, A_BLOCK), (0, 0), (0, 0))
    kp = jnp.pad(k, pad)
    vp = jnp.pad(v, pad)

    def blk(args):
        i, qi = args
        start = i * A_BLOCK
        kw = lax.dynamic_slice_in_dim(kp, start, 3 * A_BLOCK, axis=1)
        vw = lax.dynamic_slice_in_dim(vp, start, 3 * A_BLOCK, axis=1)
        kpos = start - A_BLOCK + jnp.arange(3 * A_BLOCK)
        qpos = start + jnp.arange(A_BLOCK)
        mask = (jnp.abs(qpos[:, None] - kpos[None, :]) <= WINDOW) & (kpos >= 0)[None, :] & (kpos < S)[None, :]
        return _attend(qi, k_ctx, v_ctx, sink, kw, vw, mask)

    out = lax.map(blk, (jnp.arange(nb), qb))
    return out.swapaxes(0, 1).reshape(B, S, MIX_W)


def mlstm_chunked(q, k, v, li, lf, C0, n0, m0):
    B, S = q.shape[:2]
    nc = S // M_CHUNK

    def seq_chunks(a):
        return a.reshape(B, nc, M_CHUNK, M_HEADS, M_DH).transpose(1, 0, 3, 2, 4)

    def gate_chunks(a):
        return a.reshape(B, nc, M_CHUNK, M_HEADS).transpose(1, 0, 3, 2)

    lower = jnp.tril(jnp.ones((M_CHUNK, M_CHUNK), dtype=bool))

    def step(carry, inp):
        C, n, m = carry
        qc, kc, vc, lic, lfc = inp
        b = jnp.cumsum(lfc, axis=-1)
        a = b + m[..., None]
        dmat = jnp.where(lower, b[..., :, None] - b[..., None, :] + lic[..., None, :], -jnp.inf)
        m_t = jnp.maximum(a, jnp.max(dmat, axis=-1))
        w = jnp.exp(dmat - m_t[..., None])
        w0 = jnp.exp(a - m_t)
        s = w * jnp.einsum('bhtd,bhsd->bhts', qc, kc)
        num = w0[..., None] * jnp.einsum('bhde,bhte->bhtd', C, qc) + jnp.einsum('bhts,bhsd->bhtd', s, vc)
        den = w0 * jnp.einsum('bhe,bhte->bht', n, qc) + jnp.sum(s, axis=-1)
        h = num / jnp.maximum(jnp.abs(den), jnp.exp(-m_t))[..., None]
        m_end = m_t[..., -1]
        w_end = w[..., -1, :]
        w0_end = w0[..., -1]
        C_new = w0_end[..., None, None] * C + jnp.einsum('bhs,bhsd,bhse->bhde', w_end, vc, kc)
        n_new = w0_end[..., None] * n + jnp.einsum('bhs,bhse->bhe', w_end, kc)
        return (C_new, n_new, m_end), h

    (C, n, m), h = lax.scan(step, (C0, n0, m0),
                            (seq_chunks(q), seq_chunks(k), seq_chunks(v), gate_chunks(li), gate_chunks(lf)))
    h = h.transpose(1, 0, 3, 2, 4).reshape(B, S, M_HEADS, M_DH)
    return h, C, n, m


def mlstm_bidir(q, k, v, g, C0, n0, m0):
    li_f, lf_f = g[:, :, 0], jax.nn.log_sigmoid(g[:, :, 1])
    li_b, lf_b = g[:, :, 2], jax.nn.log_sigmoid(g[:, :, 3])
    h_f, C_f, n_f, m_f = mlstm_chunked(q, k, v, li_f, lf_f, C0[:, 0], n0[:, 0], m0[:, 0])
    fl = lambda a: jnp.flip(a, axis=1)
    h_b, C_b, n_b, m_b = mlstm_chunked(fl(q), fl(k), fl(v), fl(li_b), fl(lf_b), C0[:, 1], n0[:, 1], m0[:, 1])
    return (h_f + fl(h_b), jnp.stack([C_f, C_b], axis=1), jnp.stack([n_f, n_b], axis=1),
            jnp.stack([m_f, m_b], axis=1))


def chunk_mlp(u, v, w_s, b_s):
    B, S = u.shape[:2]
    nc = S // C_CHUNK
    vr = rms_norm(v).reshape(B, nc, C_CHUNK, C_GROUPS, C_DG)
    z = jnp.einsum('gpq,bnqgc->bnpgc', w_s, vr) + b_s.T[None, None, :, :, None]
    return u * z.reshape(B, S, MIX_W)


def peer(x, w_q, keys, u_tab, v_tab):
    B, S, D = x.shape
    T = B * S
    xt = x.reshape(T, D)
    q = (xt @ w_q).reshape(T, PEER_HEADS, 2, PEER_DK // 2)
    s = jnp.einsum('thpd,hpkd->thpk', q, keys, preferred_element_type=jnp.float32)
    v1, i1 = lax.top_k(s[:, :, 0], PEER_TOPK)
    v2, i2 = lax.top_k(s[:, :, 1], PEER_TOPK)
    cand = (v1[..., :, None] + v2[..., None, :]).reshape(T, PEER_HEADS, PEER_TOPK * PEER_TOPK)
    vs, ic = lax.top_k(cand, PEER_TOPK)
    e1 = jnp.take_along_axis(i1, ic // PEER_TOPK, axis=-1)
    e2 = jnp.take_along_axis(i2, ic % PEER_TOPK, axis=-1)
    nb = T // PEER_BLOCK
    ids = (e1 * N_KEYS + e2).reshape(nb, PEER_BLOCK, PEER_HEADS * PEER_TOPK)
    gw = jax.nn.softmax(vs, axis=-1).reshape(nb, PEER_BLOCK, PEER_HEADS * PEER_TOPK)
    xb = xt.reshape(nb, PEER_BLOCK, D)

    def blk(args):
        xi, ii, gi = args
        act = jax.nn.gelu(jnp.einsum('tkd,td->tk', u_tab[ii], xi, preferred_element_type=jnp.float32))
        return jnp.einsum('tk,tkd->td', (gi * act).astype(x.dtype), v_tab[ii])

    return lax.map(blk, (xb, ids, gw)).reshape(B, S, D)


def token_mixer(h, P, ctx):
    B, S, _ = h.shape
    f32 = jnp.float32
    z = h @ P['w_in']
    sizes = [MIX_W] * 4 + [4 * M_HEADS] + [MIX_W] * 3 + [A_KVW] * 2
    qm, km, vm, om, gm, uc, vc, qa, ka, va = jnp.split(z, np.cumsum(sizes)[:-1].tolist(), axis=-1)
    hd = (B, S, M_HEADS, M_DH)
    q_m = qm.reshape(hd).astype(f32)
    k_m = km.reshape(hd).astype(f32) * (M_DH ** -0.5)
    v_m = vm.reshape(hd).astype(f32)
    g = gm.reshape(B, S, 4, M_HEADS).astype(f32) + P['b_gates_m'].astype(f32)
    if ctx is None:
        C0 = jnp.zeros((B, 2, M_HEADS, M_DH, M_DH), f32)
        n0 = jnp.zeros((B, 2, M_HEADS, M_DH), f32)
        m0 = jnp.zeros((B, 2, M_HEADS), f32)
    else:
        C0, n0, m0 = ctx[2].astype(f32), ctx[3].astype(f32), ctx[4].astype(f32)
    h_m, C, n, m = mlstm_bidir(q_m, k_m, v_m, g, C0, n0, m0)
    y_m = rms_norm(h_m, P['g_mlstm']).reshape(B, S, MIX_W).astype(h.dtype) * jax.nn.sigmoid(om)
    y_c = chunk_mlp(uc, vc, P['w_spatial'], P['b_spatial'])
    q_a = qa.reshape(B, S, A_HEADS, A_DH)
    k_a = ka.reshape(B, S, A_KV, A_DH)
    v_a = va.reshape(B, S, A_KV, A_DH)
    if ctx is None:
        y_a = context_attention(q_a, k_a, v_a, P['sink'])
        new = (k_a, v_a, C, n, m)
    else:
        y_a = latent_attention(axial_rope(q_a), axial_rope(k_a), v_a, ctx[0], ctx[1], P['sink'])
        new = None
    gate = jax.nn.sigmoid(h @ P['w_merge'] + P['b_merge']).reshape(B, S, N_BRANCH, D_MODEL)
    br = jnp.stack([y_m, y_c.astype(h.dtype), y_a.astype(h.dtype)], axis=2)
    proj = jnp.einsum('bsnm,nmd->bsnd', br, P['w_branch'])
    out = jnp.sum(gate * proj, axis=2) @ P['w_out']
    return out, new


def layer(x, cond, P, ctx):
    mod = jax.nn.silu(cond) @ P['w_ada'] + P['b_ada']
    sh1, sc1, g1, sh2, sc2, g2 = jnp.split(mod, 6, axis=-1)
    h = rms_norm(x, P['g_norm1']) * (1 + sc1) + sh1
    y, new = token_mixer(h, P, ctx)
    x = x + g1 * y
    h2 = rms_norm(x, P['g_norm2']) * (1 + sc2) + sh2
    x = x + g2 * peer(h2, P['w_peer_q'], P['peer_keys'], P['peer_u'], P['peer_v'])
    return x, new


def setup_inputs(seed: int = 0) -> dict:
    key = jax.random.key(seed)
    ks = jax.random.split(key, 32)
    f32 = jnp.float32
    D = D_MODEL

    def nrm(k, shape, scale):
        return jax.random.normal(k, shape, f32) * scale

    f_bias = jnp.linspace(3.0, 6.0, M_HEADS, dtype=f32)
    gate_offsets = jnp.stack([jnp.zeros_like(f_bias), f_bias, jnp.zeros_like(f_bias), f_bias], axis=0)
    return {
        'x_prompt': nrm(ks[0], (BATCH, SEQ, D), 1.0),
        'x_sample': nrm(ks[1], (DEC_BATCH, DEC_SEQ, D), 1.0),
        'cache_k': nrm(ks[2], (DEC_BATCH, DEPTH, PAST_LEN, A_KV, A_DH), 1.0),
        'cache_v': nrm(ks[3], (DEC_BATCH, DEPTH, PAST_LEN, A_KV, A_DH), 1.0),
        'state_C': nrm(ks[4], (DEC_BATCH, DEPTH, 2, M_HEADS, M_DH, M_DH), 0.05),
        'state_n': nrm(ks[5], (DEC_BATCH, DEPTH, 2, M_HEADS, M_DH), 0.5),
        'state_m': nrm(ks[6], (DEC_BATCH, DEPTH, 2, M_HEADS), 1.0),
        'c': nrm(ks[7], (DEC_BATCH, D), 1.0),
        'c_ctx': nrm(ks[8], (D,), 1.0),
        'w_ada': nrm(ks[9], (DEPTH, D, 6 * D), D ** -0.5),
        'b_ada': nrm(ks[10], (DEPTH, 6 * D), 0.02),
        'g_norm1': 1.0 + nrm(ks[11], (DEPTH, D), 0.05),
        'g_norm2': 1.0 + nrm(ks[12], (DEPTH, D), 0.05),
        'w_in': nrm(ks[13], (DEPTH, D, IN_W), D ** -0.5),
        'b_gates_m': gate_offsets[None] + nrm(ks[14], (DEPTH, 4, M_HEADS), 0.1),
        'g_mlstm': 1.0 + nrm(ks[15], (DEPTH, M_HEADS, M_DH), 0.05),
        'w_spatial': nrm(ks[16], (DEPTH, C_GROUPS, C_CHUNK, C_CHUNK), C_CHUNK ** -0.5),
        'b_spatial': 1.0 + nrm(ks[17], (DEPTH, C_GROUPS, C_CHUNK), 0.05),
        'sink': nrm(ks[18], (DEPTH, A_HEADS), 0.5),
        'w_branch': nrm(ks[19], (DEPTH, N_BRANCH, MIX_W, D), MIX_W ** -0.5),
        'w_merge': nrm(ks[20], (DEPTH, D, N_BRANCH * D), D ** -0.5),
        'b_merge': nrm(ks[21], (DEPTH, N_BRANCH * D), 0.02),
        'w_out': nrm(ks[22], (DEPTH, D, D), D ** -0.5),
        'w_peer_q': nrm(ks[23], (DEPTH, D, PEER_HEADS * PEER_DK), D ** -0.5),
        'peer_keys': nrm(ks[24], (DEPTH, PEER_HEADS, 2, N_KEYS, PEER_DK // 2), (PEER_DK // 2) ** -0.5),
        'peer_u': nrm(ks[25], (DEPTH, N_EXPERTS, D), D ** -0.5),
        'peer_v': nrm(ks[26], (DEPTH, N_EXPERTS, D), 0.5),
        'g_final': 1.0 + nrm(ks[27], (D,), 0.05),
    }


def reference(x_prompt, x_sample, cache_k, cache_v, state_C, state_n, state_m, c, c_ctx,
              w_ada, b_ada, g_norm1, g_norm2, w_in, b_gates_m, g_mlstm, w_spatial, b_spatial, sink,
              w_branch, w_merge, b_merge, w_out, w_peer_q, peer_keys, peer_u, peer_v, g_final):
    cond_ctx = c_ctx[None, None, :]
    cond_lat = c[:, None, :]
    xp = x_prompt
    xs = x_sample
    ks_, vs_, Cs_, ns_, ms_ = [], [], [], [], []
    for l in range(DEPTH):
        P = {'w_ada': w_ada[l], 'b_ada': b_ada[l], 'g_norm1': g_norm1[l], 'g_norm2': g_norm2[l],
             'w_in': w_in[l], 'b_gates_m': b_gates_m[l], 'g_mlstm': g_mlstm[l],
             'w_spatial': w_spatial[l], 'b_spatial': b_spatial[l], 'sink': sink[l],
             'w_branch': w_branch[l], 'w_merge': w_merge[l], 'b_merge': b_merge[l], 'w_out': w_out[l],
             'w_peer_q': w_peer_q[l], 'peer_keys': peer_keys[l], 'peer_u': peer_u[l], 'peer_v': peer_v[l]}
        xp, (k_l, v_l, C_l, n_l, m_l) = layer(xp, cond_ctx, P, None)
        ks_.append(k_l)
        vs_.append(v_l)
        Cs_.append(C_l)
        ns_.append(n_l)
        ms_.append(m_l)
        ctx = (cache_k[:, l], cache_v[:, l], state_C[:, l], state_n[:, l], state_m[:, l])
        xs, _ = layer(xs, cond_lat, P, ctx)
    y_prompt = rms_norm(xp, g_final)
    y_sample = rms_norm(xs, g_final)
    new_cache_k = jnp.stack(ks_, axis=1)
    new_cache_v = jnp.stack(vs_, axis=1)
    new_state_C = jnp.stack(Cs_, axis=1)
    new_state_n = jnp.stack(ns_, axis=1)
    new_state_m = jnp.stack(ms_, axis=1)
    return (y_prompt, y_sample, new_cache_k, new_cache_v, new_state_C, new_state_n, new_state_m)
```

```python
import functools

import numpy as np
import jax
import jax.numpy as jnp
from jax import lax
from jax.experimental import pallas as pl
from jax.experimental.pallas import tpu as pltpu

D_MODEL = 1024
DEPTH = 2
GRID_W = 64
EPS = 1e-6
NEG_INF = -1e30
MIX_W = D_MODEL // 2
M_HEADS = 4
M_DH = MIX_W // M_HEADS
CHUNK = 128
C_GROUPS = 4
A_HEADS = 8
A_KV = 2
A_GROUP = A_HEADS // A_KV
A_DH = MIX_W // A_HEADS
A_KVW = A_KV * A_DH
ROPE_BASE = 10000.0
ROPE_FREQS = A_DH // 4
N_KEYS = 128
N_EXPERTS = N_KEYS * N_KEYS
PEER_HEADS = 8
PEER_TOPK = 16
PEER_QW = 2 * PEER_HEADS * N_KEYS

LANES = 128
MXU_DTYPE = jnp.bfloat16
VMEM_LIMIT = 56 * 1024 * 1024

Z_QM, Z_KM, Z_VM, Z_OM, Z_UC, Z_VC, Z_QA = (i * MIX_W for i in range(7))
Z_KA = 7 * MIX_W
Z_VA = Z_KA + A_KVW
Z_GM = Z_VA + A_KVW
Z_W = Z_GM + LANES

f32 = jnp.float32


def _params(sem):
    return pltpu.CompilerParams(dimension_semantics=sem, vmem_limit_bytes=VMEM_LIMIT)


def _mm(a, b):
    return jnp.dot(a.astype(MXU_DTYPE), b.astype(MXU_DTYPE), preferred_element_type=f32)


def _mm_nt(a, b):
    return lax.dot_general(a.astype(MXU_DTYPE), b.astype(MXU_DTYPE), (((1,), (1,)), ((), ())),
                           preferred_element_type=f32)


def _split3(x):
    hi = x.astype(jnp.bfloat16)
    r1 = x - hi.astype(f32)
    mid = r1.astype(jnp.bfloat16)
    lo = (r1 - mid.astype(f32)).astype(jnp.bfloat16)
    return hi, mid, lo


def _mm_exact_lhs(a01, x):
    a = a01.astype(jnp.bfloat16)
    hi, mid, lo = _split3(x)
    return (jnp.dot(a, hi, preferred_element_type=f32) + jnp.dot(a, mid, preferred_element_type=f32)
            + jnp.dot(a, lo, preferred_element_type=f32))


def _mm3(a, b):
    ah, am, al = _split3(a)
    bh, bm, bl = _split3(b)
    d = functools.partial(jnp.dot, preferred_element_type=f32)
    return (d(ah, bh) + (d(ah, bm) + d(am, bh)) + (d(ah, bl) + d(al, bh) + d(am, bm)))


def _rms(x):
    return x * lax.rsqrt(jnp.mean(x * x, axis=-1, keepdims=True) + EPS)


def _mod_norm(x, g, scale, shift):
    return _rms(x) * g * (1.0 + scale) + shift


def _ada_kernel(c_ref, w_ref, b_ref, o_ref):
    c = c_ref[...]
    o_ref[0] = _mm3(c * jax.nn.sigmoid(c), w_ref[0]) + b_ref[0]


def _ada(cond, w_ada, b_ada):
    rows = cond.shape[0]
    tn = 1536
    return pl.pallas_call(
        _ada_kernel,
        grid=(DEPTH, 6 * D_MODEL // tn),
        in_specs=[pl.BlockSpec((rows, D_MODEL), lambda l, j: (0, 0)),
                  pl.BlockSpec((1, D_MODEL, tn), lambda l, j: (l, 0, j)),
                  pl.BlockSpec((1, 1, tn), lambda l, j: (l, 0, j))],
        out_specs=pl.BlockSpec((1, rows, tn), lambda l, j: (l, 0, j)),
        out_shape=jax.ShapeDtypeStruct((DEPTH, rows, 6 * D_MODEL), f32),
        compiler_params=_params(("parallel", "parallel")),
        name="ada",
    )(cond, w_ada, b_ada.reshape(DEPTH, 1, 6 * D_MODEL))


class _Layout:
    def __init__(self, nb_c, s_c, nb_l, s_l):
        self.nb_c, self.s_c, self.nb_l, self.s_l = nb_c, s_c, nb_l, s_l
        self.t_c = nb_c * s_c
        self.t_l = nb_l * s_l
        self.t = self.t_c + self.t_l

    def mod_row(self, tm):
        assert self.t_c % tm == 0 and self.s_l % tm == 0
        nc_tiles = self.t_c // tm
        per = self.s_l // tm
        return lambda i: jnp.where(i < nc_tiles, 0, 1 + (i - nc_tiles) // per)


def _inproj_kernel(x_ref, mod_ref, g1_ref, w_ref, z_ref):
    h = _mod_norm(x_ref[...], g1_ref[...], mod_ref[0, 1:2, :], mod_ref[0, 0:1, :])
    z_ref[...] = _mm(h, w_ref[...])


def _inproj(lay, x, mod_l, g1, w):
    tm = 256
    row = lay.mod_row(tm)
    return pl.pallas_call(
        _inproj_kernel,
        grid=(lay.t // tm,),
        in_specs=[pl.BlockSpec((tm, D_MODEL), lambda i: (i, 0)),
                  pl.BlockSpec((1, 6, D_MODEL), lambda i: (row(i), 0, 0)),
                  pl.BlockSpec((1, D_MODEL), lambda i: (0, 0)),
                  pl.BlockSpec((D_MODEL, Z_W), lambda i: (0, 0), pipeline_mode=pl.Buffered(1))],
        out_specs=pl.BlockSpec((tm, Z_W), lambda i: (i, 0)),
        out_shape=jax.ShapeDtypeStruct((lay.t, Z_W), f32),
        compiler_params=_params(("parallel",)),
        name="inproj",
    )(x, mod_l, g1, w)


def _log_sigmoid(x):
    return jnp.minimum(x, 0.0) - jnp.log1p(jnp.exp(-jnp.abs(x)))


def _col(a, j):
    lane = lax.broadcasted_iota(jnp.int32, a.shape, 1)
    return jnp.sum(jnp.where(lane == j, a, 0.0), axis=1, keepdims=True)


def _mlstm_kernel(qf_ref, kf_ref, vf_ref, gf_ref, qb_ref, kb_ref, vb_ref, gb_ref, bias_ref,
                  c0_ref, n0_ref, m0_ref, hf_ref, hb_ref, c_out, n_out, m_out, ct_s, n_s, m_s):
    c = pl.program_id(1)

    @pl.when(c == 0)
    def _():
        for idx in range(2 * M_HEADS):
            ct_s[idx] = c0_ref[0, idx].T
        n_s[...] = n0_ref[0]
        m_s[...] = m0_ref[0]

    ri = lax.broadcasted_iota(jnp.int32, (CHUNK, CHUNK), 0)
    ci = lax.broadcasted_iota(jnp.int32, (CHUNK, CHUNK), 1)
    lane = lax.broadcasted_iota(jnp.int32, (CHUNK, LANES), 1)
    is_forget = ((lane // M_HEADS) % 2) == 1

    for d, (q_ref, k_ref, v_ref, g_ref, h_ref) in enumerate(
            ((qf_ref, kf_ref, vf_ref, gf_ref, hf_ref), (qb_ref, kb_ref, vb_ref, gb_ref, hb_ref))):
        rev = d == 1
        keep = (ri <= ci) if rev else (ri >= ci)
        g = g_ref[...] + bias_ref[...]
        lg = jnp.where(is_forget, _log_sigmoid(g), g)
        bc = _mm_exact_lhs(keep, lg)
        bct = bc.T
        lgt = lg.T
        te = 0 if rev else CHUNK - 1
        for hh in range(M_HEADS):
            idx = d * M_HEADS + hh
            j_li = 2 * d * M_HEADS + hh
            j_lf = j_li + M_HEADS
            sl = slice(hh * M_DH, (hh + 1) * M_DH)
            q = q_ref[:, sl]
            k = k_ref[:, sl] * (M_DH ** -0.5)
            v = v_ref[:, sl]
            b_col = _col(bc, j_lf)
            li_col = _col(lg, j_li)
            b_row = bct[j_lf:j_lf + 1, :]
            li_row = lgt[j_li:j_li + 1, :]
            mp = m_s[idx:idx + 1, 0:1]
            a_col = b_col + mp
            dm = jnp.where(keep, b_col - b_row + li_row, -jnp.inf)
            m_t = jnp.maximum(a_col, jnp.max(dm, axis=1, keepdims=True))
            w = jnp.exp(dm - m_t)
            w0 = jnp.exp(a_col - m_t)
            kt = k.T
            s = w * _mm(q, kt)
            ct = ct_s[idx]
            n_row = n_s[idx:idx + 1, :]
            num = w0 * _mm(q, ct) + _mm(s, v)
            den = w0 * jnp.sum(q * n_row, axis=1, keepdims=True) + jnp.sum(s, axis=1, keepdims=True)
            h_ref[:, sl] = num / jnp.maximum(jnp.abs(den), jnp.exp(-m_t))
            m_end = m_t[te:te + 1, :]
            w0_end = w0[te:te + 1, :]
            w_end = jnp.exp(b_col[te:te + 1, :] - b_col + li_col - m_end)
            ct_s[idx] = w0_end * ct + _mm(kt, v * w_end)
            n_s[idx:idx + 1, :] = w0_end * n_row + jnp.sum(k * w_end, axis=0, keepdims=True)
            m_s[idx:idx + 1, :] = jnp.broadcast_to(m_end, (1, LANES))

    @pl.when(c == pl.num_programs(1) - 1)
    def _():
        for idx in range(2 * M_HEADS):
            c_out[0, idx] = ct_s[idx].T
        n_out[0] = n_s[...]
        m_out[0] = m_s[...]


def _mlstm(z, bias_row, c0, n0, m0, nb, s, row0):
    nc = s // CHUNK
    r0 = row0 // CHUNK
    nh = 2 * M_HEADS

    def fwd(col):
        return lambda b, c: (r0 + b * nc + c, col)

    def bwd(col):
        return lambda b, c: (r0 + b * nc + (nc - 1 - c), col)

    wide = lambda im: pl.BlockSpec((CHUNK, MIX_W), im)
    gate = lambda im: pl.BlockSpec((CHUNK, LANES), im)
    state = lambda shape: pl.BlockSpec((1,) + shape, lambda b, c: (b,) + (0,) * len(shape))
    gcol = Z_GM // LANES
    return pl.pallas_call(
        _mlstm_kernel,
        grid=(nb, nc),
        in_specs=[wide(fwd(0)), wide(fwd(1)), wide(fwd(2)), gate(fwd(gcol)),
                  wide(bwd(0)), wide(bwd(1)), wide(bwd(2)), gate(bwd(gcol)),
                  pl.BlockSpec((1, LANES), lambda b, c: (0, 0)),
                  state((nh, M_DH, M_DH)), state((nh, M_DH)), state((nh, LANES))],
        out_specs=[pl.BlockSpec((CHUNK, MIX_W), lambda b, c: (b * nc + c, 0)),
                   pl.BlockSpec((CHUNK, MIX_W), lambda b, c: (b * nc + (nc - 1 - c), 0)),
                   state((nh, M_DH, M_DH)), state((nh, M_DH)), state((nh, LANES))],
        out_shape=[jax.ShapeDtypeStruct((nb * s, MIX_W), f32), jax.ShapeDtypeStruct((nb * s, MIX_W), f32),
                   jax.ShapeDtypeStruct((nb, nh, M_DH, M_DH), f32), jax.ShapeDtypeStruct((nb, nh, M_DH), f32),
                   jax.ShapeDtypeStruct((nb, nh, LANES), f32)],
        scratch_shapes=[pltpu.VMEM((nh, M_DH, M_DH), f32), pltpu.VMEM((nh, M_DH), f32),
                        pltpu.VMEM((nh, LANES), f32)],
        compiler_params=_params(("parallel", "arbitrary")),
        name="mlstm",
    )(z, z, z, z, z, z, z, z, bias_row, c0, n0, m0)


def _half_placements(a):
    lane = lax.broadcasted_iota(jnp.int32, a.shape, 1)
    g0 = jnp.where(lane < A_DH, a, 0.0)
    g1 = jnp.where(lane >= A_DH, a, 0.0)
    return ((g0, pltpu.roll(g0, A_DH, 1)), (pltpu.roll(g1, A_DH, 1), g1))


def _group_attend(q, k_all, v_all, sink_ref, l, keep):
    kz = _half_placements(k_all)
    vz = _half_placements(v_all)
    slabs = []
    for slab in range(A_HEADS // 2):
        acc = None
        for pos in range(2):
            head = 2 * slab + pos
            g = head // A_GROUP
            s = _mm_nt(q[:, slab * LANES:(slab + 1) * LANES], kz[g][pos]) * (A_DH ** -0.5)
            if keep is not None:
                s = jnp.where(keep, s, NEG_INF)
            sk = sink_ref[l, head]
            m = jnp.maximum(jnp.max(s, axis=1, keepdims=True), sk)
            p = jnp.exp(s - m)
            den = jnp.sum(p, axis=1, keepdims=True) + jnp.exp(sk - m)
            o = _mm(p, vz[g][pos]) / den
            acc = o if acc is None else acc + o
        slabs.append(acc)
    return jnp.concatenate(slabs, axis=1)


def _ctx_attn_kernel(l, sink_ref, q_ref, k_ref, v_ref, o_ref):
    o_ref[...] = _group_attend(q_ref[...], k_ref[...], v_ref[...], sink_ref, l, None)


def _ctx_attn(lay, z, sink, l):
    s = lay.s_c
    return pl.pallas_call(
        functools.partial(_ctx_attn_kernel, l),
        grid=(lay.nb_c,),
        in_specs=[pl.BlockSpec(memory_space=pltpu.SMEM),
                  pl.BlockSpec((s, MIX_W), lambda b: (b, Z_QA // MIX_W)),
                  pl.BlockSpec((s, A_KVW), lambda b: (b, Z_KA // A_KVW)),
                  pl.BlockSpec((s, A_KVW), lambda b: (b, Z_VA // A_KVW))],
        out_specs=pl.BlockSpec((s, MIX_W), lambda b: (b, 0)),
        out_shape=jax.ShapeDtypeStruct((lay.t_c, MIX_W), f32),
        compiler_params=_params(("parallel",)),
        name="ctx_attn",
    )(sink, z, z, z)


def _rope(x, cos, sin):
    lane = lax.broadcasted_iota(jnp.int32, cos.shape, 1)
    first = (lane % (2 * ROPE_FREQS)) < ROPE_FREQS
    out = []
    for j in range(x.shape[1] // LANES):
        xs = x[:, j * LANES:(j + 1) * LANES]
        partner = jnp.where(first, pltpu.roll(xs, LANES - ROPE_FREQS, 1), pltpu.roll(xs, ROPE_FREQS, 1))
        out.append(xs * cos + partner * sin)
    return out[0] if len(out) == 1 else jnp.concatenate(out, axis=1)


def _lat_attn_kernel(l, nblk, sink_ref, q_ref, kp_ref, kc_ref, kn_ref, vp_ref, vc_ref, vn_ref,
                     cq_ref, sq_ref, cp_ref, sp_ref, cn_ref, sn_ref, ck_ref, cv_ref, o_ref):
    i = pl.program_id(1)
    q = _rope(q_ref[...], cq_ref[...], sq_ref[...])
    k_all = jnp.concatenate([ck_ref[0, 0],
                             _rope(kp_ref[...], cp_ref[...], sp_ref[...]),
                             _rope(kc_ref[...], cq_ref[...], sq_ref[...]),
                             _rope(kn_ref[...], cn_ref[...], sn_ref[...])], axis=0)
    v_all = jnp.concatenate([cv_ref[0, 0], vp_ref[...], vc_ref[...], vn_ref[...]], axis=0)
    lc = ck_ref.shape[2]
    nk = lc + 3 * CHUNK
    r = lax.broadcasted_iota(jnp.int32, (CHUNK, nk), 0)
    cc = lax.broadcasted_iota(jnp.int32, (CHUNK, nk), 1) - lc
    lo = jnp.maximum(r, jnp.where(i == 0, CHUNK, 0))
    hi = jnp.minimum(r + 2 * CHUNK, jnp.where(i == nblk - 1, 2 * CHUNK - 1, 3 * CHUNK))
    keep = (cc < 0) | ((cc >= lo) & (cc <= hi))
    o_ref[...] = _group_attend(q, k_all, v_all, sink_ref, l, keep)


def _lat_attn(lay, z, sink, l, cache_k, cache_v, cos, sin):
    s, nb = lay.s_l, lay.nb_l
    nblk = s // CHUNK
    r0 = lay.t_c // CHUNK
    lc = cache_k.shape[2]

    def rows(off):
        return lambda b, i: r0 + b * nblk + jnp.clip(i + off, 0, nblk - 1)

    def zspec(width, col, off):
        rf = rows(off)
        return pl.BlockSpec((CHUNK, width), lambda b, i: (rf(b, i), col))

    def tab(off):
        return pl.BlockSpec((CHUNK, LANES), lambda b, i: (jnp.clip(i + off, 0, nblk - 1), 0))

    cache = pl.BlockSpec((1, 1, lc, A_KVW), lambda b, i: (b, l, 0, 0))
    kcol, vcol = Z_KA // A_KVW, Z_VA // A_KVW
    return pl.pallas_call(
        functools.partial(_lat_attn_kernel, l, nblk),
        grid=(nb, nblk),
        in_specs=[pl.BlockSpec(memory_space=pltpu.SMEM),
                  zspec(MIX_W, Z_QA // MIX_W, 0),
                  zspec(A_KVW, kcol, -1), zspec(A_KVW, kcol, 0), zspec(A_KVW, kcol, 1),
                  zspec(A_KVW, vcol, -1), zspec(A_KVW, vcol, 0), zspec(A_KVW, vcol, 1),
                  tab(0), tab(0), tab(-1), tab(-1), tab(1), tab(1), cache, cache],
        out_specs=pl.BlockSpec((CHUNK, MIX_W), lambda b, i: (b * nblk + i, 0)),
        out_shape=jax.ShapeDtypeStruct((lay.t_l, MIX_W), f32),
        compiler_params=_params(("parallel", "parallel")),
        name="lat_attn",
    )(sink, z, z, z, z, z, z, z, cos, sin, cos, sin, cos, sin, cache_k, cache_v)


def _rope_tables(s):
    t = jnp.arange(s)
    freqs = ROPE_BASE ** (-jnp.arange(ROPE_FREQS, dtype=f32) / ROPE_FREQS)
    a_row = (t // GRID_W).astype(f32)[:, None] * freqs[None, :]
    a_col = (t % GRID_W).astype(f32)[:, None] * freqs[None, :]
    cos = jnp.concatenate([jnp.cos(a_row)] * 2 + [jnp.cos(a_col)] * 2, axis=1)
    sin = jnp.concatenate([-jnp.sin(a_row), jnp.sin(a_row), -jnp.sin(a_col), jnp.sin(a_col)], axis=1)
    return jnp.concatenate([cos] * A_KV, axis=1), jnp.concatenate([sin] * A_KV, axis=1)


def _merge_kernel(x_ref, mod_ref, g1_ref, g2_ref, hf_ref, hb_ref, om_ref, uc_ref, vc_ref, ya_ref,
                  gm_ref, ws_ref, bs_ref, wmerge_ref, bmerge_ref, wbr_ref, wout_ref, wq_ref, keys_ref,
                  x1_ref, h2t_ref, st_ref):
    x = x_ref[...]
    tm = x.shape[0]
    mod = mod_ref[0]
    h = _mod_norm(x, g1_ref[...], mod[1:2], mod[0:1])
    hm = hf_ref[...] + hb_ref[...]
    ym = jnp.concatenate(
        [_rms(hm[:, hh * M_DH:(hh + 1) * M_DH]) * gm_ref[:, hh * M_DH:(hh + 1) * M_DH] for hh in range(M_HEADS)],
        axis=1) * jax.nn.sigmoid(om_ref[...])
    vr = _rms(vc_ref[...])
    zc = []
    for n in range(tm // CHUNK):
        rs = slice(n * CHUNK, (n + 1) * CHUNK)
        zc.append(jnp.concatenate(
            [_mm(ws_ref[g], vr[rs, g * LANES:(g + 1) * LANES]) + _col(bs_ref[...], g) for g in range(C_GROUPS)],
            axis=1))
    yc = uc_ref[...] * jnp.concatenate(zc, axis=0)
    mixed = jnp.zeros((tm, D_MODEL), f32)
    for n, y in enumerate((ym, yc, ya_ref[...])):
        gate = jax.nn.sigmoid(_mm(h, wmerge_ref[:, n * D_MODEL:(n + 1) * D_MODEL])
                              + bmerge_ref[:, n * D_MODEL:(n + 1) * D_MODEL])
        mixed = mixed + gate * _mm(y, wbr_ref[n])
    x1 = x + mod[2:3] * _mm(mixed, wout_ref[...])
    x1_ref[...] = x1
    h2 = _mod_norm(x1, g2_ref[...], mod[4:5], mod[3:4])
    h2t_ref[...] = h2.T.astype(h2t_ref.dtype)
    qp = _mm(h2, wq_ref[...])
    for hp in range(2 * PEER_HEADS):
        st_ref[hp] = _mm_nt(keys_ref[hp], qp[:, hp * N_KEYS:(hp + 1) * N_KEYS])


def _merge(lay, x, mod_l, g1, g2, hf, hb, z, ya, gm, ws, bs_t, wmerge, bmerge, wbr, wout, wq, keys):
    tm = 256
    row = lay.mod_row(tm)
    tok = lambda w, col=0: pl.BlockSpec((tm, w), lambda i: (i, col))
    full = lambda a: pl.BlockSpec(a.shape, lambda i: (0,) * a.ndim, pipeline_mode=pl.Buffered(1))
    return pl.pallas_call(
        _merge_kernel,
        grid=(lay.t // tm,),
        in_specs=[tok(D_MODEL), pl.BlockSpec((1, 6, D_MODEL), lambda i: (row(i), 0, 0)), full(g1), full(g2),
                  tok(MIX_W), tok(MIX_W), tok(MIX_W, Z_OM // MIX_W), tok(MIX_W, Z_UC // MIX_W),
                  tok(MIX_W, Z_VC // MIX_W), tok(MIX_W),
                  full(gm), full(ws), full(bs_t), full(wmerge), full(bmerge), full(wbr), full(wout), full(wq),
                  full(keys)],
        out_specs=[tok(D_MODEL), pl.BlockSpec((D_MODEL, tm), lambda i: (0, i)),
                   pl.BlockSpec((2 * PEER_HEADS, N_KEYS, tm), lambda i: (0, 0, i))],
        out_shape=[jax.ShapeDtypeStruct((lay.t, D_MODEL), f32), jax.ShapeDtypeStruct((D_MODEL, lay.t), MXU_DTYPE),
                   jax.ShapeDtypeStruct((2 * PEER_HEADS, N_KEYS, lay.t), f32)],
        compiler_params=_params(("parallel",)),
        name="merge",
    )(x, mod_l, g1, g2, hf, hb, z, z, z, ya, gm, ws, bs_t, wmerge, bmerge, wbr, wout, wq, keys)


_CELLS = [(i, j) for i in range(PEER_TOPK) for j in range(PEER_TOPK) if (i + 1) * (j + 1) <= PEER_TOPK]


def _route_kernel(st_ref, r2_ref, b_ref, jd_ref, a_ref, val_s, rank_s):
    tb = st_ref.shape[2]
    key = lax.broadcasted_iota(jnp.int32, (N_KEYS, tb), 0).astype(f32)

    def top16(hp, _):
        h = hp // 2
        p = hp % 2

        def rnd(r, carry):
            s, rank = carry
            m = jnp.max(s, axis=0, keepdims=True)
            first = jnp.min(jnp.where(s == m, key, float(N_KEYS)), axis=0, keepdims=True)
            hit = key == first
            val_s[p, r, pl.ds(h, 1), :] = m
            return jnp.where(hit, -jnp.inf, s), jnp.where(hit, r.astype(f32), rank)

        _, rank = lax.fori_loop(0, PEER_TOPK, rnd, (st_ref[hp], jnp.full((N_KEYS, tb), float(PEER_TOPK), f32)))
        rank_s[hp] = rank
        return 0

    lax.fori_loop(0, 2 * PEER_HEADS, top16, 0)

    v1 = [val_s[0, i] for i in range(PEER_TOPK)]
    v2 = [val_s[1, i] for i in range(PEER_TOPK)]
    cand = [v1[i] + v2[j] for (i, j) in _CELLS]
    ncell = len(_CELLS)
    before = [jnp.zeros((PEER_HEADS, tb), f32) for _ in range(ncell)]
    for x in range(ncell):
        for y in range(x + 1, ncell):
            x_first = (cand[x] >= cand[y]).astype(f32)
            before[y] = before[y] + x_first
            before[x] = before[x] + (1.0 - x_first)
    ea = [jnp.exp(v1[i] - v1[0]) for i in range(PEER_TOPK)]
    eb = [jnp.exp(v2[j] - v2[0]) for j in range(PEER_TOPK)]
    jcount = [jnp.zeros((PEER_HEADS, tb), f32) for _ in range(PEER_TOPK)]
    zsum = jnp.zeros((PEER_HEADS, tb), f32)
    for x, (i, j) in enumerate(_CELLS):
        sel = (before[x] < PEER_TOPK).astype(f32)
        jcount[i] = jcount[i] + sel
        zsum = zsum + sel * (ea[i] * eb[j])
    inv_z = 1.0 / zsum

    for h in range(PEER_HEADS):
        s1 = st_ref[2 * h]
        s2 = st_ref[2 * h + 1]
        rank1 = rank_s[2 * h]
        jd = jnp.zeros((N_KEYS, tb), f32)
        for i in range(PEER_TOPK):
            jd = jnp.where(rank1 == float(i), jcount[i][h:h + 1, :], jd)
        jd_ref[h] = jd
        a_ref[h] = jnp.exp(s1 - v1[0][h:h + 1, :]) * inv_z[h:h + 1, :]
        b_ref[h] = jnp.exp(s2 - v2[0][h:h + 1, :])
        r2_ref[h] = rank_s[2 * h + 1]


def _route(st):
    t = st.shape[2]
    tb = LANES
    out = jax.ShapeDtypeStruct((PEER_HEADS, N_KEYS, t), f32)
    spec = pl.BlockSpec((PEER_HEADS, N_KEYS, tb), lambda i: (0, 0, i))
    return pl.pallas_call(
        _route_kernel,
        grid=(t // tb,),
        in_specs=[pl.BlockSpec((2 * PEER_HEADS, N_KEYS, tb), lambda i: (0, 0, i))],
        out_specs=[spec] * 4,
        out_shape=[out] * 4,
        scratch_shapes=[pltpu.VMEM((2, PEER_TOPK, PEER_HEADS, tb), f32),
                        pltpu.VMEM((2 * PEER_HEADS, N_KEYS, tb), f32)],
        compiler_params=_params(("parallel",)),
        name="peer_route",
    )(st)


PEER_TT = 512
PEER_E1 = 4
PEER_ROWS = 16


def _peer_kernel(final, h2t_ref, r2_ref, b_ref, jd_ref, a_ref, u_ref, vt_ref, x1_ref, mod_ref, gf_ref,
                 o_ref, acc_s, h_s, a_s):
    j = pl.program_id(1)

    @pl.when(j == 0)
    def _():
        acc_s[...] = jnp.zeros_like(acc_s)

    h_s[...] = jnp.dot(u_ref[...], h2t_ref[...], preferred_element_type=f32)
    for e in range(PEER_E1):
        e1 = j * PEER_E1 + e
        jd = [jd_ref[hh, pl.ds(e1, 1), :] for hh in range(PEER_HEADS)]
        aa = [a_ref[hh, pl.ds(e1, 1), :] for hh in range(PEER_HEADS)]

        def chunk(r, _):
            rows = pl.ds(pl.multiple_of(r * PEER_ROWS, PEER_ROWS), PEER_ROWS)
            gate = jnp.zeros((PEER_ROWS, PEER_TT), f32)
            for hh in range(PEER_HEADS):
                gate = gate + jnp.where(r2_ref[hh, rows, :] < jd[hh], b_ref[hh, rows, :], 0.0) * aa[hh]
            hrows = pl.ds(pl.multiple_of(e * N_KEYS + r * PEER_ROWS, PEER_ROWS), PEER_ROWS)
            a_s[hrows, :] = (gate * jax.nn.gelu(h_s[hrows, :])).astype(a_s.dtype)
            return 0

        lax.fori_loop(0, N_KEYS // PEER_ROWS, chunk, 0)
    acc_s[...] += jnp.dot(vt_ref[...], a_s[...], preferred_element_type=f32)

    @pl.when(j == pl.num_programs(1) - 1)
    def _():
        x2 = x1_ref[...] + mod_ref[0, 5:6, :] * acc_s[...].T
        o_ref[...] = _rms(x2) * gf_ref[...] if final else x2


def _peer(lay, h2t, r2, b, jd, a, u, vt, x1, mod_l, g_final, final):
    tt = PEER_TT
    eb = PEER_E1 * N_KEYS
    row = lay.mod_row(tt)
    gate = pl.BlockSpec((PEER_HEADS, N_KEYS, tt), lambda i, j: (0, 0, i))
    return pl.pallas_call(
        functools.partial(_peer_kernel, final),
        grid=(lay.t // tt, N_EXPERTS // eb),
        in_specs=[pl.BlockSpec((D_MODEL, tt), lambda i, j: (0, i)), gate, gate, gate, gate,
                  pl.BlockSpec((eb, D_MODEL), lambda i, j: (j, 0)),
                  pl.BlockSpec((D_MODEL, eb), lambda i, j: (0, j)),
                  pl.BlockSpec((tt, D_MODEL), lambda i, j: (i, 0)),
                  pl.BlockSpec((1, 6, D_MODEL), lambda i, j: (row(i), 0, 0)),
                  pl.BlockSpec((1, D_MODEL), lambda i, j: (0, 0))],
        out_specs=pl.BlockSpec((tt, D_MODEL), lambda i, j: (i, 0)),
        out_shape=jax.ShapeDtypeStruct((lay.t, D_MODEL), f32),
        scratch_shapes=[pltpu.VMEM((D_MODEL, tt), f32), pltpu.VMEM((eb, tt), f32), pltpu.VMEM((eb, tt), MXU_DTYPE)],
        compiler_params=_params(("parallel", "arbitrary")),
        name="peer",
    )(h2t, r2, b, jd, a, u, vt, x1, mod_l, g_final)


def _reorder_w_in(w_in_l):
    ng = 4 * M_HEADS
    a = w_in_l[:, :4 * MIX_W]
    g = w_in_l[:, 4 * MIX_W:4 * MIX_W + ng]
    rest = w_in_l[:, 4 * MIX_W + ng:]
    pad = jnp.zeros((D_MODEL, LANES - ng), w_in_l.dtype)
    return jnp.concatenate([a, rest, g, pad], axis=1).astype(MXU_DTYPE)


def _forward(x_prompt, x_sample, cache_k, cache_v, state_C, state_n, state_m, c, c_ctx,
             w_ada, b_ada, g_norm1, g_norm2, w_in, b_gates_m, g_mlstm, w_spatial, b_spatial, sink,
             w_branch, w_merge, b_merge, w_out, w_peer_q, peer_keys, peer_u, peer_v, g_final):
    nb_c, s_c, _ = x_prompt.shape
    nb_l, s_l, _ = x_sample.shape
    lay = _Layout(nb_c, s_c, nb_l, s_l)
    lc = cache_k.shape[2]
    nh = 2 * M_HEADS

    cond = jnp.concatenate([c_ctx[None, :], c], axis=0)
    cond = jnp.pad(cond, ((0, (-cond.shape[0]) % 8), (0, 0)))
    mod = _ada(cond, w_ada, b_ada).reshape(DEPTH, cond.shape[0], 6, D_MODEL)
    x = jnp.concatenate([x_prompt.reshape(lay.t_c, D_MODEL), x_sample.reshape(lay.t_l, D_MODEL)], axis=0)
    cos, sin = _rope_tables(s_l)
    ck = cache_k.reshape(nb_l, DEPTH, lc, A_KVW)
    cv = cache_v.reshape(nb_l, DEPTH, lc, A_KVW)
    zero_c = jnp.zeros((nb_c, nh, M_DH, M_DH), f32)
    zero_n = jnp.zeros((nb_c, nh, M_DH), f32)
    zero_m = jnp.zeros((nb_c, nh, LANES), f32)
    gfin = g_final.reshape(1, D_MODEL)

    ks, vs, cs, ns, ms = [], [], [], [], []
    for l in range(DEPTH):
        g1 = g_norm1[l].reshape(1, D_MODEL)
        g2 = g_norm2[l].reshape(1, D_MODEL)
        z = _inproj(lay, x, mod[l], g1, _reorder_w_in(w_in[l]))
        bias_row = jnp.pad(b_gates_m[l].reshape(1, 4 * M_HEADS), ((0, 0), (0, LANES - 4 * M_HEADS)))
        hf_c, hb_c, c_new, n_new, m_new = _mlstm(z, bias_row, zero_c, zero_n, zero_m, nb_c, s_c, 0)
        m0 = jnp.broadcast_to(state_m[:, l].reshape(nb_l, nh, 1), (nb_l, nh, LANES))
        hf_l, hb_l, _, _, _ = _mlstm(z, bias_row, state_C[:, l].reshape(nb_l, nh, M_DH, M_DH),
                                     state_n[:, l].reshape(nb_l, nh, M_DH), m0, nb_l, s_l, lay.t_c)
        ya_c = _ctx_attn(lay, z, sink, l)
        ya_l = _lat_attn(lay, z, sink, l, ck, cv, cos, sin)
        hf = jnp.concatenate([hf_c, hf_l], axis=0)
        hb = jnp.concatenate([hb_c, hb_l], axis=0)
        ya = jnp.concatenate([ya_c, ya_l], axis=0)
        x1, h2t, st = _merge(
            lay, x, mod[l], g1, g2, hf, hb, z, ya, g_mlstm[l].reshape(1, MIX_W),
            w_spatial[l].astype(MXU_DTYPE), jnp.pad(b_spatial[l].T, ((0, 0), (0, LANES - C_GROUPS))), w_merge[l].astype(MXU_DTYPE),
            b_merge[l].reshape(1, 3 * D_MODEL), w_branch[l].astype(MXU_DTYPE), w_out[l].astype(MXU_DTYPE),
            w_peer_q[l].astype(MXU_DTYPE), peer_keys[l].reshape(2 * PEER_HEADS, N_KEYS, N_KEYS).astype(MXU_DTYPE))
        r2, b, jd, a = _route(st)
        x = _peer(lay, h2t, r2, b, jd, a, peer_u[l].astype(MXU_DTYPE), peer_v[l].T.astype(MXU_DTYPE), x1, mod[l],
                  gfin, l == DEPTH - 1)
        ks.append(z[:lay.t_c, Z_KA:Z_KA + A_KVW].reshape(nb_c, s_c, A_KV, A_DH))
        vs.append(z[:lay.t_c, Z_VA:Z_VA + A_KVW].reshape(nb_c, s_c, A_KV, A_DH))
        cs.append(c_new.reshape(nb_c, 2, M_HEADS, M_DH, M_DH))
        ns.append(n_new.reshape(nb_c, 2, M_HEADS, M_DH))
        ms.append(m_new[:, :, 0].reshape(nb_c, 2, M_HEADS))
    return (x[:lay.t_c].reshape(nb_c, s_c, D_MODEL), x[lay.t_c:].reshape(nb_l, s_l, D_MODEL),
            jnp.stack(ks, axis=1), jnp.stack(vs, axis=1), jnp.stack(cs, axis=1), jnp.stack(ns, axis=1),
            jnp.stack(ms, axis=1))


def kernel(x_prompt, x_sample, cache_k, cache_v, state_C, state_n, state_m, c, c_ctx, w_ada, b_ada, g_norm1, g_norm2, w_in, b_gates_m, g_mlstm, w_spatial, b_spatial, sink, w_branch, w_merge, b_merge, w_out, w_peer_q, peer_keys, peer_u, peer_v, g_final):
    return _forward(x_prompt, x_sample, cache_k, cache_v, state_C, state_n, state_m, c, c_ctx, w_ada, b_ada,
                    g_norm1, g_norm2, w_in, b_gates_m, g_mlstm, w_spatial, b_spatial, sink, w_branch, w_merge,
                    b_merge, w_out, w_peer_q, peer_keys, peer_u, peer_v, g_final)
```

```python
import functools

import numpy as np
import jax
import jax.numpy as jnp
from jax import lax
from jax.experimental import pallas as pl
from jax.experimental.pallas import tpu as pltpu

D_MODEL = 1024
DEPTH = 2
GRID_W = 64
EPS = 1e-6
NEG_INF = -1e30
MIX_W = D_MODEL // 2
M_HEADS = 4
M_DH = MIX_W // M_HEADS
CHUNK = 128
C_GROUPS = 4
A_HEADS = 8
A_KV = 2
A_GROUP = A_HEADS // A_KV
A_DH = MIX_W // A_HEADS
A_KVW = A_KV * A_DH
ROPE_BASE = 10000.0
ROPE_FREQS = A_DH // 4
N_KEYS = 128
N_EXPERTS = N_KEYS * N_KEYS
PEER_HEADS = 8
PEER_TOPK = 16
PEER_QW = 2 * PEER_HEADS * N_KEYS

LANES = 128
MXU_DTYPE = jnp.bfloat16
GATE_DTYPE = jnp.bfloat16
VMEM_LIMIT = 56 * 1024 * 1024

Z_QM, Z_KM, Z_VM, Z_OM, Z_UC, Z_VC, Z_QA = (i * MIX_W for i in range(7))
Z_KA = 7 * MIX_W
Z_VA = Z_KA + A_KVW
Z_GM = Z_VA + A_KVW
Z_W = Z_GM + LANES

f32 = jnp.float32


def _params(sem):
    return pltpu.CompilerParams(dimension_semantics=sem, vmem_limit_bytes=VMEM_LIMIT)


def _mm(a, b):
    return jnp.dot(a.astype(MXU_DTYPE), b.astype(MXU_DTYPE), preferred_element_type=f32)


def _mm_nt(a, b):
    return lax.dot_general(a.astype(MXU_DTYPE), b.astype(MXU_DTYPE), (((1,), (1,)), ((), ())),
                           preferred_element_type=f32)


def _split3(x):
    hi = x.astype(jnp.bfloat16)
    r1 = x - hi.astype(f32)
    mid = r1.astype(jnp.bfloat16)
    lo = (r1 - mid.astype(f32)).astype(jnp.bfloat16)
    return hi, mid, lo


def _mm_exact_lhs(a01, x):
    a = a01.astype(jnp.bfloat16)
    hi, mid, lo = _split3(x)
    return (jnp.dot(a, hi, preferred_element_type=f32) + jnp.dot(a, mid, preferred_element_type=f32)
            + jnp.dot(a, lo, preferred_element_type=f32))


def _mm3(a, b):
    ah, am, al = _split3(a)
    bh, bm, bl = _split3(b)
    d = functools.partial(jnp.dot, preferred_element_type=f32)
    return (d(ah, bh) + (d(ah, bm) + d(am, bh)) + (d(ah, bl) + d(al, bh) + d(am, bm)))


def _rms(x):
    return x * lax.rsqrt(jnp.mean(x * x, axis=-1, keepdims=True) + EPS)


def _mod_norm(x, g, scale, shift):
    return _rms(x) * g * (1.0 + scale) + shift


def _ada_kernel(c_ref, w_ref, b_ref, o_ref):
    c = c_ref[...]
    o_ref[0] = _mm3(c * jax.nn.sigmoid(c), w_ref[0]) + b_ref[0]


def _ada(cond, w_ada, b_ada):
    rows = cond.shape[0]
    tn = 1536
    return pl.pallas_call(
        _ada_kernel,
        grid=(DEPTH, 6 * D_MODEL // tn),
        in_specs=[pl.BlockSpec((rows, D_MODEL), lambda l, j: (0, 0)),
                  pl.BlockSpec((1, D_MODEL, tn), lambda l, j: (l, 0, j)),
                  pl.BlockSpec((1, 1, tn), lambda l, j: (l, 0, j))],
        out_specs=pl.BlockSpec((1, rows, tn), lambda l, j: (l, 0, j)),
        out_shape=jax.ShapeDtypeStruct((DEPTH, rows, 6 * D_MODEL), f32),
        compiler_params=_params(("parallel", "parallel")),
        name="ada",
    )(cond, w_ada, b_ada.reshape(DEPTH, 1, 6 * D_MODEL))


class _Layout:
    def __init__(self, nb_c, s_c, nb_l, s_l):
        self.nb_c, self.s_c, self.nb_l, self.s_l = nb_c, s_c, nb_l, s_l
        self.t_c = nb_c * s_c
        self.t_l = nb_l * s_l
        self.t = self.t_c + self.t_l

    def mod_row(self, tm):
        assert self.t_c % tm == 0 and self.s_l % tm == 0
        nc_tiles = self.t_c // tm
        per = self.s_l // tm
        return lambda i: jnp.where(i < nc_tiles, 0, 1 + (i - nc_tiles) // per)


def _inproj_kernel(x_ref, mod_ref, g1_ref, w_ref, z_ref):
    h = _mod_norm(x_ref[...], g1_ref[...], mod_ref[0, 1:2, :], mod_ref[0, 0:1, :])
    z_ref[...] = _mm(h, w_ref[...])


def _inproj(lay, x, mod_l, g1, w):
    tm = 256
    row = lay.mod_row(tm)
    return pl.pallas_call(
        _inproj_kernel,
        grid=(lay.t // tm,),
        in_specs=[pl.BlockSpec((tm, D_MODEL), lambda i: (i, 0)),
                  pl.BlockSpec((1, 6, D_MODEL), lambda i: (row(i), 0, 0)),
                  pl.BlockSpec((1, D_MODEL), lambda i: (0, 0)),
                  pl.BlockSpec((D_MODEL, Z_W), lambda i: (0, 0), pipeline_mode=pl.Buffered(1))],
        out_specs=pl.BlockSpec((tm, Z_W), lambda i: (i, 0)),
        out_shape=jax.ShapeDtypeStruct((lay.t, Z_W), f32),
        compiler_params=_params(("parallel",)),
        name="inproj",
    )(x, mod_l, g1, w)


def _log_sigmoid(x):
    return jnp.minimum(x, 0.0) - jnp.log1p(jnp.exp(-jnp.abs(x)))


def _col(a, j):
    lane = lax.broadcasted_iota(jnp.int32, a.shape, 1)
    return jnp.sum(jnp.where(lane == j, a, 0.0), axis=1, keepdims=True)


def _mlstm_kernel(qf_ref, kf_ref, vf_ref, gf_ref, qb_ref, kb_ref, vb_ref, gb_ref, bias_ref,
                  c0_ref, n0_ref, m0_ref, hf_ref, hb_ref, c_out, n_out, m_out, ct_s, n_s, m_s):
    c = pl.program_id(1)

    @pl.when(c == 0)
    def _():
        for idx in range(2 * M_HEADS):
            ct_s[idx] = c0_ref[0, idx].T
        n_s[...] = n0_ref[0]
        m_s[...] = m0_ref[0]

    ri = lax.broadcasted_iota(jnp.int32, (CHUNK, CHUNK), 0)
    ci = lax.broadcasted_iota(jnp.int32, (CHUNK, CHUNK), 1)
    lane = lax.broadcasted_iota(jnp.int32, (CHUNK, LANES), 1)
    is_forget = ((lane // M_HEADS) % 2) == 1

    for d, (q_ref, k_ref, v_ref, g_ref, h_ref) in enumerate(
            ((qf_ref, kf_ref, vf_ref, gf_ref, hf_ref), (qb_ref, kb_ref, vb_ref, gb_ref, hb_ref))):
        rev = d == 1
        keep = (ri <= ci) if rev else (ri >= ci)
        g = g_ref[...] + bias_ref[...]
        lg = jnp.where(is_forget, _log_sigmoid(g), g)
        bc = _mm_exact_lhs(keep, lg)
        bct = bc.T
        lgt = lg.T
        te = 0 if rev else CHUNK - 1
        for hh in range(M_HEADS):
            idx = d * M_HEADS + hh
            j_li = 2 * d * M_HEADS + hh
            j_lf = j_li + M_HEADS
            sl = slice(hh * M_DH, (hh + 1) * M_DH)
            q = q_ref[:, sl]
            k = k_ref[:, sl] * (M_DH ** -0.5)
            v = v_ref[:, sl]
            b_col = _col(bc, j_lf)
            li_col = _col(lg, j_li)
            b_row = bct[j_lf:j_lf + 1, :]
            li_row = lgt[j_li:j_li + 1, :]
            mp = m_s[idx:idx + 1, 0:1]
            a_col = b_col + mp
            dm = jnp.where(keep, b_col - b_row + li_row, -jnp.inf)
            m_t = jnp.maximum(a_col, jnp.max(dm, axis=1, keepdims=True))
            w = jnp.exp(dm - m_t)
            w0 = jnp.exp(a_col - m_t)
            kt = k.T
            s = w * _mm(q, kt)
            ct = ct_s[idx]
            n_row = n_s[idx:idx + 1, :]
            num = w0 * _mm(q, ct) + _mm(s, v)
            den = w0 * jnp.sum(q * n_row, axis=1, keepdims=True) + jnp.sum(s, axis=1, keepdims=True)
            h_ref[:, sl] = num / jnp.maximum(jnp.abs(den), jnp.exp(-m_t))
            m_end = m_t[te:te + 1, :]
            w0_end = w0[te:te + 1, :]
            w_end = jnp.exp(b_col[te:te + 1, :] - b_col + li_col - m_end)
            ct_s[idx] = w0_end * ct + _mm(kt, v * w_end)
            n_s[idx:idx + 1, :] = w0_end * n_row + jnp.sum(k * w_end, axis=0, keepdims=True)
            m_s[idx:idx + 1, :] = jnp.broadcast_to(m_end, (1, LANES))

    @pl.when(c == pl.num_programs(1) - 1)
    def _():
        for idx in range(2 * M_HEADS):
            c_out[0, idx] = ct_s[idx].T
        n_out[0] = n_s[...]
        m_out[0] = m_s[...]


def _mlstm(z, bias_row, c0, n0, m0, nb, s, row0):
    nc = s // CHUNK
    r0 = row0 // CHUNK
    nh = 2 * M_HEADS

    def fwd(col):
        return lambda b, c: (r0 + b * nc + c, col)

    def bwd(col):
        return lambda b, c: (r0 + b * nc + (nc - 1 - c), col)

    wide = lambda im: pl.BlockSpec((CHUNK, MIX_W), im)
    gate = lambda im: pl.BlockSpec((CHUNK, LANES), im)
    state = lambda shape: pl.BlockSpec((1,) + shape, lambda b, c: (b,) + (0,) * len(shape))
    gcol = Z_GM // LANES
    return pl.pallas_call(
        _mlstm_kernel,
        grid=(nb, nc),
        in_specs=[wide(fwd(0)), wide(fwd(1)), wide(fwd(2)), gate(fwd(gcol)),
                  wide(bwd(0)), wide(bwd(1)), wide(bwd(2)), gate(bwd(gcol)),
                  pl.BlockSpec((1, LANES), lambda b, c: (0, 0)),
                  state((nh, M_DH, M_DH)), state((nh, M_DH)), state((nh, LANES))],
        out_specs=[pl.BlockSpec((CHUNK, MIX_W), lambda b, c: (b * nc + c, 0)),
                   pl.BlockSpec((CHUNK, MIX_W), lambda b, c: (b * nc + (nc - 1 - c), 0)),
                   state((nh, M_DH, M_DH)), state((nh, M_DH)), state((nh, LANES))],
        out_shape=[jax.ShapeDtypeStruct((nb * s, MIX_W), f32), jax.ShapeDtypeStruct((nb * s, MIX_W), f32),
                   jax.ShapeDtypeStruct((nb, nh, M_DH, M_DH), f32), jax.ShapeDtypeStruct((nb, nh, M_DH), f32),
                   jax.ShapeDtypeStruct((nb, nh, LANES), f32)],
        scratch_shapes=[pltpu.VMEM((nh, M_DH, M_DH), f32), pltpu.VMEM((nh, M_DH), f32),
                        pltpu.VMEM((nh, LANES), f32)],
        compiler_params=_params(("parallel", "arbitrary")),
        name="mlstm",
    )(z, z, z, z, z, z, z, z, bias_row, c0, n0, m0)


def _half_placements(a):
    lane = lax.broadcasted_iota(jnp.int32, a.shape, 1)
    g0 = jnp.where(lane < A_DH, a, 0.0)
    g1 = jnp.where(lane >= A_DH, a, 0.0)
    return ((g0, pltpu.roll(g0, A_DH, 1)), (pltpu.roll(g1, A_DH, 1), g1))


def _group_attend(q, k_all, v_all, sink_ref, l, keep):
    kz = _half_placements(k_all)
    vz = _half_placements(v_all)
    slabs = []
    for slab in range(A_HEADS // 2):
        acc = None
        for pos in range(2):
            head = 2 * slab + pos
            g = head // A_GROUP
            s = _mm_nt(q[:, slab * LANES:(slab + 1) * LANES], kz[g][pos]) * (A_DH ** -0.5)
            if keep is not None:
                s = jnp.where(keep, s, NEG_INF)
            sk = sink_ref[l, head]
            m = jnp.maximum(jnp.max(s, axis=1, keepdims=True), sk)
            p = jnp.exp(s - m)
            den = jnp.sum(p, axis=1, keepdims=True) + jnp.exp(sk - m)
            o = _mm(p, vz[g][pos]) / den
            acc = o if acc is None else acc + o
        slabs.append(acc)
    return jnp.concatenate(slabs, axis=1)


def _ctx_attn_kernel(l, sink_ref, q_ref, k_ref, v_ref, o_ref):
    o_ref[...] = _group_attend(q_ref[...], k_ref[...], v_ref[...], sink_ref, l, None)


def _ctx_attn(lay, z, sink, l):
    s = lay.s_c
    return pl.pallas_call(
        functools.partial(_ctx_attn_kernel, l),
        grid=(lay.nb_c,),
        in_specs=[pl.BlockSpec(memory_space=pltpu.SMEM),
                  pl.BlockSpec((s, MIX_W), lambda b: (b, Z_QA // MIX_W)),
                  pl.BlockSpec((s, A_KVW), lambda b: (b, Z_KA // A_KVW)),
                  pl.BlockSpec((s, A_KVW), lambda b: (b, Z_VA // A_KVW))],
        out_specs=pl.BlockSpec((s, MIX_W), lambda b: (b, 0)),
        out_shape=jax.ShapeDtypeStruct((lay.t_c, MIX_W), f32),
        compiler_params=_params(("parallel",)),
        name="ctx_attn",
    )(sink, z, z, z)


def _rope(x, cos, sin):
    lane = lax.broadcasted_iota(jnp.int32, cos.shape, 1)
    first = (lane % (2 * ROPE_FREQS)) < ROPE_FREQS
    out = []
    for j in range(x.shape[1] // LANES):
        xs = x[:, j * LANES:(j + 1) * LANES]
        partner = jnp.where(first, pltpu.roll(xs, LANES - ROPE_FREQS, 1), pltpu.roll(xs, ROPE_FREQS, 1))
        out.append(xs * cos + partner * sin)
    return out[0] if len(out) == 1 else jnp.concatenate(out, axis=1)


def _lat_attn_kernel(l, nblk, sink_ref, q_ref, kp_ref, kc_ref, kn_ref, vp_ref, vc_ref, vn_ref,
                     cq_ref, sq_ref, cp_ref, sp_ref, cn_ref, sn_ref, ck_ref, cv_ref, o_ref):
    i = pl.program_id(1)
    q = _rope(q_ref[...], cq_ref[...], sq_ref[...])
    k_all = jnp.concatenate([ck_ref[0, 0],
                             _rope(kp_ref[...], cp_ref[...], sp_ref[...]),
                             _rope(kc_ref[...], cq_ref[...], sq_ref[...]),
                             _rope(kn_ref[...], cn_ref[...], sn_ref[...])], axis=0)
    v_all = jnp.concatenate([cv_ref[0, 0], vp_ref[...], vc_ref[...], vn_ref[...]], axis=0)
    lc = ck_ref.shape[2]
    nk = lc + 3 * CHUNK
    r = lax.broadcasted_iota(jnp.int32, (CHUNK, nk), 0)
    cc = lax.broadcasted_iota(jnp.int32, (CHUNK, nk), 1) - lc
    lo = jnp.maximum(r, jnp.where(i == 0, CHUNK, 0))
    hi = jnp.minimum(r + 2 * CHUNK, jnp.where(i == nblk - 1, 2 * CHUNK - 1, 3 * CHUNK))
    keep = (cc < 0) | ((cc >= lo) & (cc <= hi))
    o_ref[...] = _group_attend(q, k_all, v_all, sink_ref, l, keep)


def _lat_attn(lay, z, sink, l, cache_k, cache_v, cos, sin):
    s, nb = lay.s_l, lay.nb_l
    nblk = s // CHUNK
    r0 = lay.t_c // CHUNK
    lc = cache_k.shape[2]

    def rows(off):
        return lambda b, i: r0 + b * nblk + jnp.clip(i + off, 0, nblk - 1)

    def zspec(width, col, off):
        rf = rows(off)
        return pl.BlockSpec((CHUNK, width), lambda b, i: (rf(b, i), col))

    def tab(off):
        return pl.BlockSpec((CHUNK, LANES), lambda b, i: (jnp.clip(i + off, 0, nblk - 1), 0))

    cache = pl.BlockSpec((1, 1, lc, A_KVW), lambda b, i: (b, l, 0, 0))
    kcol, vcol = Z_KA // A_KVW, Z_VA // A_KVW
    return pl.pallas_call(
        functools.partial(_lat_attn_kernel, l, nblk),
        grid=(nb, nblk),
        in_specs=[pl.BlockSpec(memory_space=pltpu.SMEM),
                  zspec(MIX_W, Z_QA // MIX_W, 0),
                  zspec(A_KVW, kcol, -1), zspec(A_KVW, kcol, 0), zspec(A_KVW, kcol, 1),
                  zspec(A_KVW, vcol, -1), zspec(A_KVW, vcol, 0), zspec(A_KVW, vcol, 1),
                  tab(0), tab(0), tab(-1), tab(-1), tab(1), tab(1), cache, cache],
        out_specs=pl.BlockSpec((CHUNK, MIX_W), lambda b, i: (b * nblk + i, 0)),
        out_shape=jax.ShapeDtypeStruct((lay.t_l, MIX_W), f32),
        compiler_params=_params(("parallel", "parallel")),
        name="lat_attn",
    )(sink, z, z, z, z, z, z, z, cos, sin, cos, sin, cos, sin, cache_k, cache_v)


def _rope_tables(s):
    t = jnp.arange(s)
    freqs = ROPE_BASE ** (-jnp.arange(ROPE_FREQS, dtype=f32) / ROPE_FREQS)
    a_row = (t // GRID_W).astype(f32)[:, None] * freqs[None, :]
    a_col = (t % GRID_W).astype(f32)[:, None] * freqs[None, :]
    cos = jnp.concatenate([jnp.cos(a_row)] * 2 + [jnp.cos(a_col)] * 2, axis=1)
    sin = jnp.concatenate([-jnp.sin(a_row), jnp.sin(a_row), -jnp.sin(a_col), jnp.sin(a_col)], axis=1)
    return jnp.concatenate([cos] * A_KV, axis=1), jnp.concatenate([sin] * A_KV, axis=1)


def _merge_kernel(x_ref, mod_ref, g1_ref, g2_ref, hf_ref, hb_ref, om_ref, uc_ref, vc_ref, ya_ref,
                  gm_ref, ws_ref, bs_ref, wmerge_ref, bmerge_ref, wbr_ref, wout_ref, wq_ref, keys_ref,
                  x1_ref, h2t_ref, st_ref):
    x = x_ref[...]
    tm = x.shape[0]
    mod = mod_ref[0]
    h = _mod_norm(x, g1_ref[...], mod[1:2], mod[0:1])
    hm = hf_ref[...] + hb_ref[...]
    ym = jnp.concatenate(
        [_rms(hm[:, hh * M_DH:(hh + 1) * M_DH]) * gm_ref[:, hh * M_DH:(hh + 1) * M_DH] for hh in range(M_HEADS)],
        axis=1) * jax.nn.sigmoid(om_ref[...])
    vr = _rms(vc_ref[...])
    zc = []
    for n in range(tm // CHUNK):
        rs = slice(n * CHUNK, (n + 1) * CHUNK)
        zc.append(jnp.concatenate(
            [_mm(ws_ref[g], vr[rs, g * LANES:(g + 1) * LANES]) + _col(bs_ref[...], g) for g in range(C_GROUPS)],
            axis=1))
    yc = uc_ref[...] * jnp.concatenate(zc, axis=0)
    mixed = jnp.zeros((tm, D_MODEL), f32)
    for n, y in enumerate((ym, yc, ya_ref[...])):
        gate = jax.nn.sigmoid(_mm(h, wmerge_ref[:, n * D_MODEL:(n + 1) * D_MODEL])
                              + bmerge_ref[:, n * D_MODEL:(n + 1) * D_MODEL])
        mixed = mixed + gate * _mm(y, wbr_ref[n])
    x1 = x + mod[2:3] * _mm(mixed, wout_ref[...])
    x1_ref[...] = x1
    h2 = _mod_norm(x1, g2_ref[...], mod[4:5], mod[3:4])
    h2t_ref[...] = h2.T.astype(h2t_ref.dtype)
    qp = _mm(h2, wq_ref[...])
    for hp in range(2 * PEER_HEADS):
        st_ref[hp] = _mm_nt(keys_ref[hp], qp[:, hp * N_KEYS:(hp + 1) * N_KEYS])


def _merge(lay, x, mod_l, g1, g2, hf, hb, z, ya, gm, ws, bs_t, wmerge, bmerge, wbr, wout, wq, keys):
    tm = 256
    row = lay.mod_row(tm)
    tok = lambda w, col=0: pl.BlockSpec((tm, w), lambda i: (i, col))
    full = lambda a: pl.BlockSpec(a.shape, lambda i: (0,) * a.ndim, pipeline_mode=pl.Buffered(1))
    return pl.pallas_call(
        _merge_kernel,
        grid=(lay.t // tm,),
        in_specs=[tok(D_MODEL), pl.BlockSpec((1, 6, D_MODEL), lambda i: (row(i), 0, 0)), full(g1), full(g2),
                  tok(MIX_W), tok(MIX_W), tok(MIX_W, Z_OM // MIX_W), tok(MIX_W, Z_UC // MIX_W),
                  tok(MIX_W, Z_VC // MIX_W), tok(MIX_W),
                  full(gm), full(ws), full(bs_t), full(wmerge), full(bmerge), full(wbr), full(wout), full(wq),
                  full(keys)],
        out_specs=[tok(D_MODEL), pl.BlockSpec((D_MODEL, tm), lambda i: (0, i)),
                   pl.BlockSpec((2 * PEER_HEADS, N_KEYS, tm), lambda i: (0, 0, i))],
        out_shape=[jax.ShapeDtypeStruct((lay.t, D_MODEL), f32), jax.ShapeDtypeStruct((D_MODEL, lay.t), MXU_DTYPE),
                   jax.ShapeDtypeStruct((2 * PEER_HEADS, N_KEYS, lay.t), f32)],
        compiler_params=_params(("parallel",)),
        name="merge",
    )(x, mod_l, g1, g2, hf, hb, z, z, z, ya, gm, ws, bs_t, wmerge, bmerge, wbr, wout, wq, keys)


_CELLS = [(i, j) for i in range(PEER_TOPK) for j in range(PEER_TOPK) if (i + 1) * (j + 1) <= PEER_TOPK]


def _route_kernel(st_ref, r2_ref, b_ref, jd_ref, a_ref, val_s, rank_s):
    tb = st_ref.shape[2]
    key = lax.broadcasted_iota(jnp.int32, (N_KEYS, tb), 0).astype(f32)

    def top16(hp, _):
        h = hp // 2
        p = hp % 2

        def rnd(r, carry):
            s, rank = carry
            m = jnp.max(s, axis=0, keepdims=True)
            first = jnp.min(jnp.where(s == m, key, float(N_KEYS)), axis=0, keepdims=True)
            hit = key == first
            val_s[p, r, pl.ds(h, 1), :] = m
            return jnp.where(hit, -jnp.inf, s), jnp.where(hit, r.astype(f32), rank)

        _, rank = lax.fori_loop(0, PEER_TOPK, rnd, (st_ref[hp], jnp.full((N_KEYS, tb), float(PEER_TOPK), f32)))
        rank_s[hp] = rank
        return 0

    lax.fori_loop(0, 2 * PEER_HEADS, top16, 0)

    v1 = [val_s[0, i] for i in range(PEER_TOPK)]
    v2 = [val_s[1, i] for i in range(PEER_TOPK)]
    cand = [v1[i] + v2[j] for (i, j) in _CELLS]
    ncell = len(_CELLS)
    before = [jnp.zeros((PEER_HEADS, tb), f32) for _ in range(ncell)]
    for x in range(ncell):
        for y in range(x + 1, ncell):
            x_first = (cand[x] >= cand[y]).astype(f32)
            before[y] = before[y] + x_first
            before[x] = before[x] + (1.0 - x_first)
    ea = [jnp.exp(v1[i] - v1[0]) for i in range(PEER_TOPK)]
    eb = [jnp.exp(v2[j] - v2[0]) for j in range(PEER_TOPK)]
    jcount = [jnp.zeros((PEER_HEADS, tb), f32) for _ in range(PEER_TOPK)]
    zsum = jnp.zeros((PEER_HEADS, tb), f32)
    for x, (i, j) in enumerate(_CELLS):
        sel = (before[x] < PEER_TOPK).astype(f32)
        jcount[i] = jcount[i] + sel
        zsum = zsum + sel * (ea[i] * eb[j])
    inv_z = 1.0 / zsum

    for h in range(PEER_HEADS):
        s1 = st_ref[2 * h]
        s2 = st_ref[2 * h + 1]
        rank1 = rank_s[2 * h]
        jd = jnp.zeros((N_KEYS, tb), f32)
        for i in range(PEER_TOPK):
            jd = jnp.where(rank1 == float(i), jcount[i][h:h + 1, :], jd)
        jd_ref[h] = jd.astype(jd_ref.dtype)
        a_ref[h] = (jnp.exp(s1 - v1[0][h:h + 1, :]) * inv_z[h:h + 1, :]).astype(a_ref.dtype)
        b_ref[h] = jnp.exp(s2 - v2[0][h:h + 1, :]).astype(b_ref.dtype)
        r2_ref[h] = rank_s[2 * h + 1].astype(r2_ref.dtype)


def _route(st):
    t = st.shape[2]
    tb = LANES
    out = [jax.ShapeDtypeStruct((PEER_HEADS, N_KEYS, t), dt) for dt in (GATE_DTYPE, GATE_DTYPE, f32, f32)]
    spec = pl.BlockSpec((PEER_HEADS, N_KEYS, tb), lambda i: (0, 0, i))
    return pl.pallas_call(
        _route_kernel,
        grid=(t // tb,),
        in_specs=[pl.BlockSpec((2 * PEER_HEADS, N_KEYS, tb), lambda i: (0, 0, i))],
        out_specs=[spec] * 4,
        out_shape=out,
        scratch_shapes=[pltpu.VMEM((2, PEER_TOPK, PEER_HEADS, tb), f32),
                        pltpu.VMEM((2 * PEER_HEADS, N_KEYS, tb), f32)],
        compiler_params=_params(("parallel",)),
        name="peer_route",
    )(st)


PEER_TT = 512
PEER_E1 = 8
PEER_SUB = 2
PEER_ROWS = 16


def _peer_kernel(final, h2t_ref, r2_ref, b_ref, jd_ref, a_ref, u_ref, vt_ref, x1_ref, mod_ref, gf_ref,
                 o_ref, acc_s, a_s):
    j = pl.program_id(1)

    @pl.when(j == 0)
    def _():
        acc_s[...] = jnp.zeros_like(acc_s)

    sub_rows = PEER_SUB * N_KEYS
    n_sub = PEER_E1 // PEER_SUB
    srow = lambda sb: slice(sb * sub_rows, (sb + 1) * sub_rows)
    hdot = lambda sb: jnp.dot(u_ref[srow(sb), :], h2t_ref[...], preferred_element_type=f32)

    def accumulate(sb):
        acc_s[...] += jnp.dot(vt_ref[:, srow(sb)], a_s[srow(sb), :], preferred_element_type=f32)

    h_next = hdot(0)
    for sb in range(n_sub):
        h = h_next
        if sb + 1 < n_sub:
            h_next = hdot(sb + 1)
        if sb >= 1:
            accumulate(sb - 1)
        for e in range(PEER_SUB):
            e1 = j * PEER_E1 + sb * PEER_SUB + e
            jd = [jd_ref[hh, pl.ds(e1, 1), :].astype(GATE_DTYPE) for hh in range(PEER_HEADS)]
            aa = [a_ref[hh, pl.ds(e1, 1), :].astype(GATE_DTYPE) for hh in range(PEER_HEADS)]
            for r in range(N_KEYS // PEER_ROWS):
                rows = slice(r * PEER_ROWS, (r + 1) * PEER_ROWS)
                gate = jnp.zeros((PEER_ROWS, PEER_TT), GATE_DTYPE)
                for hh in range(PEER_HEADS):
                    hit = r2_ref[hh, rows, :] < jd[hh]
                    gate = gate + jnp.where(hit, b_ref[hh, rows, :], jnp.zeros((), GATE_DTYPE)) * aa[hh]
                lo = e * N_KEYS + r * PEER_ROWS
                act = gate.astype(f32) * jax.nn.gelu(h[lo:lo + PEER_ROWS, :])
                a_s[sb * sub_rows + lo:sb * sub_rows + lo + PEER_ROWS, :] = act.astype(a_s.dtype)
    accumulate(n_sub - 1)

    @pl.when(j == pl.num_programs(1) - 1)
    def _():
        x2 = x1_ref[...] + mod_ref[0, 5:6, :] * acc_s[...].T
        o_ref[...] = _rms(x2) * gf_ref[...] if final else x2


def _peer(lay, h2t, r2, b, jd, a, u, vt, x1, mod_l, g_final, final):
    tt = PEER_TT
    eb = PEER_E1 * N_KEYS
    row = lay.mod_row(tt)
    gate = pl.BlockSpec((PEER_HEADS, N_KEYS, tt), lambda i, j: (0, 0, i))
    return pl.pallas_call(
        functools.partial(_peer_kernel, final),
        grid=(lay.t // tt, N_EXPERTS // eb),
        in_specs=[pl.BlockSpec((D_MODEL, tt), lambda i, j: (0, i)), gate, gate, gate, gate,
                  pl.BlockSpec((eb, D_MODEL), lambda i, j: (j, 0)),
                  pl.BlockSpec((D_MODEL, eb), lambda i, j: (0, j)),
                  pl.BlockSpec((tt, D_MODEL), lambda i, j: (i, 0)),
                  pl.BlockSpec((1, 6, D_MODEL), lambda i, j: (row(i), 0, 0)),
                  pl.BlockSpec((1, D_MODEL), lambda i, j: (0, 0))],
        out_specs=pl.BlockSpec((tt, D_MODEL), lambda i, j: (i, 0)),
        out_shape=jax.ShapeDtypeStruct((lay.t, D_MODEL), f32),
        scratch_shapes=[pltpu.VMEM((D_MODEL, tt), f32), pltpu.VMEM((eb, tt), MXU_DTYPE)],
        compiler_params=_params(("parallel", "arbitrary")),
        name="peer",
    )(h2t, r2, b, jd, a, u, vt, x1, mod_l, g_final)


def _reorder_w_in(w_in_l):
    ng = 4 * M_HEADS
    a = w_in_l[:, :4 * MIX_W]
    g = w_in_l[:, 4 * MIX_W:4 * MIX_W + ng]
    rest = w_in_l[:, 4 * MIX_W + ng:]
    pad = jnp.zeros((D_MODEL, LANES - ng), w_in_l.dtype)
    return jnp.concatenate([a, rest, g, pad], axis=1).astype(MXU_DTYPE)


def _forward(x_prompt, x_sample, cache_k, cache_v, state_C, state_n, state_m, c, c_ctx,
             w_ada, b_ada, g_norm1, g_norm2, w_in, b_gates_m, g_mlstm, w_spatial, b_spatial, sink,
             w_branch, w_merge, b_merge, w_out, w_peer_q, peer_keys, peer_u, peer_v, g_final):
    nb_c, s_c, _ = x_prompt.shape
    nb_l, s_l, _ = x_sample.shape
    lay = _Layout(nb_c, s_c, nb_l, s_l)
    lc = cache_k.shape[2]
    nh = 2 * M_HEADS

    cond = jnp.concatenate([c_ctx[None, :], c], axis=0)
    cond = jnp.pad(cond, ((0, (-cond.shape[0]) % 8), (0, 0)))
    mod = _ada(cond, w_ada, b_ada).reshape(DEPTH, cond.shape[0], 6, D_MODEL)
    x = jnp.concatenate([x_prompt.reshape(lay.t_c, D_MODEL), x_sample.reshape(lay.t_l, D_MODEL)], axis=0)
    cos, sin = _rope_tables(s_l)
    ck = cache_k.reshape(nb_l, DEPTH, lc, A_KVW)
    cv = cache_v.reshape(nb_l, DEPTH, lc, A_KVW)
    zero_c = jnp.zeros((nb_c, nh, M_DH, M_DH), f32)
    zero_n = jnp.zeros((nb_c, nh, M_DH), f32)
    zero_m = jnp.zeros((nb_c, nh, LANES), f32)
    gfin = g_final.reshape(1, D_MODEL)

    ks, vs, cs, ns, ms = [], [], [], [], []
    for l in range(DEPTH):
        g1 = g_norm1[l].reshape(1, D_MODEL)
        g2 = g_norm2[l].reshape(1, D_MODEL)
        z = _inproj(lay, x, mod[l], g1, _reorder_w_in(w_in[l]))
        bias_row = jnp.pad(b_gates_m[l].reshape(1, 4 * M_HEADS), ((0, 0), (0, LANES - 4 * M_HEADS)))
        hf_c, hb_c, c_new, n_new, m_new = _mlstm(z, bias_row, zero_c, zero_n, zero_m, nb_c, s_c, 0)
        m0 = jnp.broadcast_to(state_m[:, l].reshape(nb_l, nh, 1), (nb_l, nh, LANES))
        hf_l, hb_l, _, _, _ = _mlstm(z, bias_row, state_C[:, l].reshape(nb_l, nh, M_DH, M_DH),
                                     state_n[:, l].reshape(nb_l, nh, M_DH), m0, nb_l, s_l, lay.t_c)
        ya_c = _ctx_attn(lay, z, sink, l)
        ya_l = _lat_attn(lay, z, sink, l, ck, cv, cos, sin)
        hf = jnp.concatenate([hf_c, hf_l], axis=0)
        hb = jnp.concatenate([hb_c, hb_l], axis=0)
        ya = jnp.concatenate([ya_c, ya_l], axis=0)
        x1, h2t, st = _merge(
            lay, x, mod[l], g1, g2, hf, hb, z, ya, g_mlstm[l].reshape(1, MIX_W),
            w_spatial[l].astype(MXU_DTYPE), jnp.pad(b_spatial[l].T, ((0, 0), (0, LANES - C_GROUPS))), w_merge[l].astype(MXU_DTYPE),
            b_merge[l].reshape(1, 3 * D_MODEL), w_branch[l].astype(MXU_DTYPE), w_out[l].astype(MXU_DTYPE),
            w_peer_q[l].astype(MXU_DTYPE), peer_keys[l].reshape(2 * PEER_HEADS, N_KEYS, N_KEYS).astype(MXU_DTYPE))
        r2, b, jd, a = _route(st)
        x = _peer(lay, h2t, r2, b, jd, a, peer_u[l].astype(MXU_DTYPE), peer_v[l].T.astype(MXU_DTYPE), x1, mod[l],
                  gfin, l == DEPTH - 1)
        ks.append(z[:lay.t_c, Z_KA:Z_KA + A_KVW].reshape(nb_c, s_c, A_KV, A_DH))
        vs.append(z[:lay.t_c, Z_VA:Z_VA + A_KVW].reshape(nb_c, s_c, A_KV, A_DH))
        cs.append(c_new.reshape(nb_c, 2, M_HEADS, M_DH, M_DH))
        ns.append(n_new.reshape(nb_c, 2, M_HEADS, M_DH))
        ms.append(m_new[:, :, 0].reshape(nb_c, 2, M_HEADS))
    return (x[:lay.t_c].reshape(nb_c, s_c, D_MODEL), x[lay.t_c:].reshape(nb_l, s_l, D_MODEL),
            jnp.stack(ks, axis=1), jnp.stack(vs, axis=1), jnp.stack(cs, axis=1), jnp.stack(ns, axis=1),
            jnp.stack(ms, axis=1))


def kernel(x_prompt, x_sample, cache_k, cache_v, state_C, state_n, state_m, c, c_ctx, w_ada, b_ada, g_norm1, g_norm2, w_in, b_gates_m, g_mlstm, w_spatial, b_spatial, sink, w_branch, w_merge, b_merge, w_out, w_peer_q, peer_keys, peer_u, peer_v, g_final):
    return _forward(x_prompt, x_sample, cache_k, cache_v, state_C, state_n, state_m, c, c_ctx, w_ada, b_ada,
                    g_norm1, g_norm2, w_in, b_gates_m, g_mlstm, w_spatial, b_spatial, sink, w_branch, w_merge,
                    b_merge, w_out, w_peer_q, peer_keys, peer_u, peer_v, g_final)
```

```python
import functools

import numpy as np
import jax
import jax.numpy as jnp
from jax import lax
from jax.experimental import pallas as pl
from jax.experimental.pallas import tpu as pltpu

D_MODEL = 1024
DEPTH = 2
GRID_W = 64
EPS = 1e-6
NEG_INF = -1e30
MIX_W = D_MODEL // 2
M_HEADS = 4
M_DH = MIX_W // M_HEADS
CHUNK = 128
C_GROUPS = 4
A_HEADS = 8
A_KV = 2
A_GROUP = A_HEADS // A_KV
A_DH = MIX_W // A_HEADS
A_KVW = A_KV * A_DH
ROPE_BASE = 10000.0
ROPE_FREQS = A_DH // 4
N_KEYS = 128
N_EXPERTS = N_KEYS * N_KEYS
PEER_HEADS = 8
PEER_TOPK = 16
PEER_QW = 2 * PEER_HEADS * N_KEYS

LANES = 128
MXU_DTYPE = jnp.bfloat16
GATE_DTYPE = jnp.bfloat16
VMEM_LIMIT = 56 * 1024 * 1024

Z_QM, Z_KM, Z_VM, Z_OM, Z_UC, Z_VC, Z_QA = (i * MIX_W for i in range(7))
Z_KA = 7 * MIX_W
Z_VA = Z_KA + A_KVW
Z_GM = Z_VA + A_KVW
Z_W = Z_GM + LANES

f32 = jnp.float32


def _params(sem):
    return pltpu.CompilerParams(dimension_semantics=sem, vmem_limit_bytes=VMEM_LIMIT)


def _mm(a, b):
    return jnp.dot(a.astype(MXU_DTYPE), b.astype(MXU_DTYPE), preferred_element_type=f32)


def _mm_nt(a, b):
    return lax.dot_general(a.astype(MXU_DTYPE), b.astype(MXU_DTYPE), (((1,), (1,)), ((), ())),
                           preferred_element_type=f32)


def _split3(x):
    hi = x.astype(jnp.bfloat16)
    r1 = x - hi.astype(f32)
    mid = r1.astype(jnp.bfloat16)
    lo = (r1 - mid.astype(f32)).astype(jnp.bfloat16)
    return hi, mid, lo


def _mm_exact_lhs(a01, x):
    a = a01.astype(jnp.bfloat16)
    hi, mid, lo = _split3(x)
    return (jnp.dot(a, hi, preferred_element_type=f32) + jnp.dot(a, mid, preferred_element_type=f32)
            + jnp.dot(a, lo, preferred_element_type=f32))


def _mm3(a, b):
    ah, am, al = _split3(a)
    bh, bm, bl = _split3(b)
    d = functools.partial(jnp.dot, preferred_element_type=f32)
    return (d(ah, bh) + (d(ah, bm) + d(am, bh)) + (d(ah, bl) + d(al, bh) + d(am, bm)))


def _rms(x):
    return x * lax.rsqrt(jnp.mean(x * x, axis=-1, keepdims=True) + EPS)


def _mod_norm(x, g, scale, shift):
    return _rms(x) * g * (1.0 + scale) + shift


def _ada_kernel(c_ref, w_ref, b_ref, o_ref):
    c = c_ref[...]
    o_ref[0] = _mm3(c * jax.nn.sigmoid(c), w_ref[0]) + b_ref[0]


def _ada(cond, w_ada, b_ada):
    rows = cond.shape[0]
    tn = 1536
    return pl.pallas_call(
        _ada_kernel,
        grid=(DEPTH, 6 * D_MODEL // tn),
        in_specs=[pl.BlockSpec((rows, D_MODEL), lambda l, j: (0, 0)),
                  pl.BlockSpec((1, D_MODEL, tn), lambda l, j: (l, 0, j)),
                  pl.BlockSpec((1, 1, tn), lambda l, j: (l, 0, j))],
        out_specs=pl.BlockSpec((1, rows, tn), lambda l, j: (l, 0, j)),
        out_shape=jax.ShapeDtypeStruct((DEPTH, rows, 6 * D_MODEL), f32),
        compiler_params=_params(("parallel", "parallel")),
        name="ada",
    )(cond, w_ada, b_ada.reshape(DEPTH, 1, 6 * D_MODEL))


class _Layout:
    def __init__(self, nb_c, s_c, nb_l, s_l):
        self.nb_c, self.s_c, self.nb_l, self.s_l = nb_c, s_c, nb_l, s_l
        self.t_c = nb_c * s_c
        self.t_l = nb_l * s_l
        self.t = self.t_c + self.t_l

    def mod_row(self, tm):
        assert self.t_c % tm == 0 and self.s_l % tm == 0
        nc_tiles = self.t_c // tm
        per = self.s_l // tm
        return lambda i: jnp.where(i < nc_tiles, 0, 1 + (i - nc_tiles) // per)


def _inproj_kernel(x_ref, mod_ref, g1_ref, w_ref, z_ref):
    h = _mod_norm(x_ref[...], g1_ref[...], mod_ref[0, 1:2, :], mod_ref[0, 0:1, :])
    z_ref[...] = _mm(h, w_ref[...])


def _inproj(lay, x, mod_l, g1, w):
    tm = 256
    row = lay.mod_row(tm)
    return pl.pallas_call(
        _inproj_kernel,
        grid=(lay.t // tm,),
        in_specs=[pl.BlockSpec((tm, D_MODEL), lambda i: (i, 0)),
                  pl.BlockSpec((1, 6, D_MODEL), lambda i: (row(i), 0, 0)),
                  pl.BlockSpec((1, D_MODEL), lambda i: (0, 0)),
                  pl.BlockSpec((D_MODEL, Z_W), lambda i: (0, 0), pipeline_mode=pl.Buffered(1))],
        out_specs=pl.BlockSpec((tm, Z_W), lambda i: (i, 0)),
        out_shape=jax.ShapeDtypeStruct((lay.t, Z_W), f32),
        compiler_params=_params(("parallel",)),
        name="inproj",
    )(x, mod_l, g1, w)


def _log_sigmoid(x):
    return jnp.minimum(x, 0.0) - jnp.log1p(jnp.exp(-jnp.abs(x)))


def _col(a, j):
    lane = lax.broadcasted_iota(jnp.int32, a.shape, 1)
    return jnp.sum(jnp.where(lane == j, a, 0.0), axis=1, keepdims=True)


def _mlstm_kernel(qf_ref, kf_ref, vf_ref, gf_ref, qb_ref, kb_ref, vb_ref, gb_ref, bias_ref,
                  c0_ref, n0_ref, m0_ref, hf_ref, hb_ref, c_out, n_out, m_out, ct_s, n_s, m_s):
    c = pl.program_id(1)

    @pl.when(c == 0)
    def _():
        for idx in range(2 * M_HEADS):
            ct_s[idx] = c0_ref[0, idx].T
        n_s[...] = n0_ref[0]
        m_s[...] = m0_ref[0]

    ri = lax.broadcasted_iota(jnp.int32, (CHUNK, CHUNK), 0)
    ci = lax.broadcasted_iota(jnp.int32, (CHUNK, CHUNK), 1)
    lane = lax.broadcasted_iota(jnp.int32, (CHUNK, LANES), 1)
    is_forget = ((lane // M_HEADS) % 2) == 1

    for d, (q_ref, k_ref, v_ref, g_ref, h_ref) in enumerate(
            ((qf_ref, kf_ref, vf_ref, gf_ref, hf_ref), (qb_ref, kb_ref, vb_ref, gb_ref, hb_ref))):
        rev = d == 1
        keep = (ri <= ci) if rev else (ri >= ci)
        g = g_ref[...] + bias_ref[...]
        lg = jnp.where(is_forget, _log_sigmoid(g), g)
        bc = _mm_exact_lhs(keep, lg)
        bct = bc.T
        lgt = lg.T
        te = 0 if rev else CHUNK - 1
        for hh in range(M_HEADS):
            idx = d * M_HEADS + hh
            j_li = 2 * d * M_HEADS + hh
            j_lf = j_li + M_HEADS
            sl = slice(hh * M_DH, (hh + 1) * M_DH)
            q = q_ref[:, sl]
            k = k_ref[:, sl] * (M_DH ** -0.5)
            v = v_ref[:, sl]
            b_col = _col(bc, j_lf)
            li_col = _col(lg, j_li)
            b_row = bct[j_lf:j_lf + 1, :]
            li_row = lgt[j_li:j_li + 1, :]
            mp = m_s[idx:idx + 1, 0:1]
            a_col = b_col + mp
            dm = jnp.where(keep, b_col - b_row + li_row, -jnp.inf)
            m_t = jnp.maximum(a_col, jnp.max(dm, axis=1, keepdims=True))
            w = jnp.exp(dm - m_t)
            w0 = jnp.exp(a_col - m_t)
            kt = k.T
            s = w * _mm(q, kt)
            ct = ct_s[idx]
            n_row = n_s[idx:idx + 1, :]
            num = w0 * _mm(q, ct) + _mm(s, v)
            den = w0 * jnp.sum(q * n_row, axis=1, keepdims=True) + jnp.sum(s, axis=1, keepdims=True)
            h_ref[:, sl] = num / jnp.maximum(jnp.abs(den), jnp.exp(-m_t))
            m_end = m_t[te:te + 1, :]
            w0_end = w0[te:te + 1, :]
            w_end = jnp.exp(b_col[te:te + 1, :] - b_col + li_col - m_end)
            ct_s[idx] = w0_end * ct + _mm(kt, v * w_end)
            n_s[idx:idx + 1, :] = w0_end * n_row + jnp.sum(k * w_end, axis=0, keepdims=True)
            m_s[idx:idx + 1, :] = jnp.broadcast_to(m_end, (1, LANES))

    @pl.when(c == pl.num_programs(1) - 1)
    def _():
        for idx in range(2 * M_HEADS):
            c_out[0, idx] = ct_s[idx].T
        n_out[0] = n_s[...]
        m_out[0] = m_s[...]


def _mlstm(z, bias_row, c0, n0, m0, nb, s, row0):
    nc = s // CHUNK
    r0 = row0 // CHUNK
    nh = 2 * M_HEADS

    def fwd(col):
        return lambda b, c: (r0 + b * nc + c, col)

    def bwd(col):
        return lambda b, c: (r0 + b * nc + (nc - 1 - c), col)

    wide = lambda im: pl.BlockSpec((CHUNK, MIX_W), im)
    gate = lambda im: pl.BlockSpec((CHUNK, LANES), im)
    state = lambda shape: pl.BlockSpec((1,) + shape, lambda b, c: (b,) + (0,) * len(shape))
    gcol = Z_GM // LANES
    return pl.pallas_call(
        _mlstm_kernel,
        grid=(nb, nc),
        in_specs=[wide(fwd(0)), wide(fwd(1)), wide(fwd(2)), gate(fwd(gcol)),
                  wide(bwd(0)), wide(bwd(1)), wide(bwd(2)), gate(bwd(gcol)),
                  pl.BlockSpec((1, LANES), lambda b, c: (0, 0)),
                  state((nh, M_DH, M_DH)), state((nh, M_DH)), state((nh, LANES))],
        out_specs=[pl.BlockSpec((CHUNK, MIX_W), lambda b, c: (b * nc + c, 0)),
                   pl.BlockSpec((CHUNK, MIX_W), lambda b, c: (b * nc + (nc - 1 - c), 0)),
                   state((nh, M_DH, M_DH)), state((nh, M_DH)), state((nh, LANES))],
        out_shape=[jax.ShapeDtypeStruct((nb * s, MIX_W), f32), jax.ShapeDtypeStruct((nb * s, MIX_W), f32),
                   jax.ShapeDtypeStruct((nb, nh, M_DH, M_DH), f32), jax.ShapeDtypeStruct((nb, nh, M_DH), f32),
                   jax.ShapeDtypeStruct((nb, nh, LANES), f32)],
        scratch_shapes=[pltpu.VMEM((nh, M_DH, M_DH), f32), pltpu.VMEM((nh, M_DH), f32),
                        pltpu.VMEM((nh, LANES), f32)],
        compiler_params=_params(("parallel", "arbitrary")),
        name="mlstm",
    )(z, z, z, z, z, z, z, z, bias_row, c0, n0, m0)


def _half_placements(a):
    lane = lax.broadcasted_iota(jnp.int32, a.shape, 1)
    g0 = jnp.where(lane < A_DH, a, 0.0)
    g1 = jnp.where(lane >= A_DH, a, 0.0)
    return ((g0, pltpu.roll(g0, A_DH, 1)), (pltpu.roll(g1, A_DH, 1), g1))


def _group_attend(q, k_all, v_all, sink_ref, l, keep):
    kz = _half_placements(k_all)
    vz = _half_placements(v_all)
    slabs = []
    for slab in range(A_HEADS // 2):
        acc = None
        for pos in range(2):
            head = 2 * slab + pos
            g = head // A_GROUP
            s = _mm_nt(q[:, slab * LANES:(slab + 1) * LANES], kz[g][pos]) * (A_DH ** -0.5)
            if keep is not None:
                s = jnp.where(keep, s, NEG_INF)
            sk = sink_ref[l, head]
            m = jnp.maximum(jnp.max(s, axis=1, keepdims=True), sk)
            p = jnp.exp(s - m)
            den = jnp.sum(p, axis=1, keepdims=True) + jnp.exp(sk - m)
            o = _mm(p, vz[g][pos]) / den
            acc = o if acc is None else acc + o
        slabs.append(acc)
    return jnp.concatenate(slabs, axis=1)


def _ctx_attn_kernel(l, sink_ref, q_ref, k_ref, v_ref, o_ref):
    o_ref[...] = _group_attend(q_ref[...], k_ref[...], v_ref[...], sink_ref, l, None)


def _ctx_attn(lay, z, sink, l):
    s = lay.s_c
    return pl.pallas_call(
        functools.partial(_ctx_attn_kernel, l),
        grid=(lay.nb_c,),
        in_specs=[pl.BlockSpec(memory_space=pltpu.SMEM),
                  pl.BlockSpec((s, MIX_W), lambda b: (b, Z_QA // MIX_W)),
                  pl.BlockSpec((s, A_KVW), lambda b: (b, Z_KA // A_KVW)),
                  pl.BlockSpec((s, A_KVW), lambda b: (b, Z_VA // A_KVW))],
        out_specs=pl.BlockSpec((s, MIX_W), lambda b: (b, 0)),
        out_shape=jax.ShapeDtypeStruct((lay.t_c, MIX_W), f32),
        compiler_params=_params(("parallel",)),
        name="ctx_attn",
    )(sink, z, z, z)


def _rope(x, cos, sin):
    lane = lax.broadcasted_iota(jnp.int32, cos.shape, 1)
    first = (lane % (2 * ROPE_FREQS)) < ROPE_FREQS
    out = []
    for j in range(x.shape[1] // LANES):
        xs = x[:, j * LANES:(j + 1) * LANES]
        partner = jnp.where(first, pltpu.roll(xs, LANES - ROPE_FREQS, 1), pltpu.roll(xs, ROPE_FREQS, 1))
        out.append(xs * cos + partner * sin)
    return out[0] if len(out) == 1 else jnp.concatenate(out, axis=1)


def _lat_attn_kernel(l, nblk, sink_ref, q_ref, kp_ref, kc_ref, kn_ref, vp_ref, vc_ref, vn_ref,
                     cq_ref, sq_ref, cp_ref, sp_ref, cn_ref, sn_ref, ck_ref, cv_ref, o_ref):
    i = pl.program_id(1)
    q = _rope(q_ref[...], cq_ref[...], sq_ref[...])
    k_all = jnp.concatenate([ck_ref[0, 0],
                             _rope(kp_ref[...], cp_ref[...], sp_ref[...]),
                             _rope(kc_ref[...], cq_ref[...], sq_ref[...]),
                             _rope(kn_ref[...], cn_ref[...], sn_ref[...])], axis=0)
    v_all = jnp.concatenate([cv_ref[0, 0], vp_ref[...], vc_ref[...], vn_ref[...]], axis=0)
    lc = ck_ref.shape[2]
    nk = lc + 3 * CHUNK
    r = lax.broadcasted_iota(jnp.int32, (CHUNK, nk), 0)
    cc = lax.broadcasted_iota(jnp.int32, (CHUNK, nk), 1) - lc
    lo = jnp.maximum(r, jnp.where(i == 0, CHUNK, 0))
    hi = jnp.minimum(r + 2 * CHUNK, jnp.where(i == nblk - 1, 2 * CHUNK - 1, 3 * CHUNK))
    keep = (cc < 0) | ((cc >= lo) & (cc <= hi))
    o_ref[...] = _group_attend(q, k_all, v_all, sink_ref, l, keep)


def _lat_attn(lay, z, sink, l, cache_k, cache_v, cos, sin):
    s, nb = lay.s_l, lay.nb_l
    nblk = s // CHUNK
    r0 = lay.t_c // CHUNK
    lc = cache_k.shape[2]

    def rows(off):
        return lambda b, i: r0 + b * nblk + jnp.clip(i + off, 0, nblk - 1)

    def zspec(width, col, off):
        rf = rows(off)
        return pl.BlockSpec((CHUNK, width), lambda b, i: (rf(b, i), col))

    def tab(off):
        return pl.BlockSpec((CHUNK, LANES), lambda b, i: (jnp.clip(i + off, 0, nblk - 1), 0))

    cache = pl.BlockSpec((1, 1, lc, A_KVW), lambda b, i: (b, l, 0, 0))
    kcol, vcol = Z_KA // A_KVW, Z_VA // A_KVW
    return pl.pallas_call(
        functools.partial(_lat_attn_kernel, l, nblk),
        grid=(nb, nblk),
        in_specs=[pl.BlockSpec(memory_space=pltpu.SMEM),
                  zspec(MIX_W, Z_QA // MIX_W, 0),
                  zspec(A_KVW, kcol, -1), zspec(A_KVW, kcol, 0), zspec(A_KVW, kcol, 1),
                  zspec(A_KVW, vcol, -1), zspec(A_KVW, vcol, 0), zspec(A_KVW, vcol, 1),
                  tab(0), tab(0), tab(-1), tab(-1), tab(1), tab(1), cache, cache],
        out_specs=pl.BlockSpec((CHUNK, MIX_W), lambda b, i: (b * nblk + i, 0)),
        out_shape=jax.ShapeDtypeStruct((lay.t_l, MIX_W), f32),
        compiler_params=_params(("parallel", "parallel")),
        name="lat_attn",
    )(sink, z, z, z, z, z, z, z, cos, sin, cos, sin, cos, sin, cache_k, cache_v)


def _rope_tables(s):
    t = jnp.arange(s)
    freqs = ROPE_BASE ** (-jnp.arange(ROPE_FREQS, dtype=f32) / ROPE_FREQS)
    a_row = (t // GRID_W).astype(f32)[:, None] * freqs[None, :]
    a_col = (t % GRID_W).astype(f32)[:, None] * freqs[None, :]
    cos = jnp.concatenate([jnp.cos(a_row)] * 2 + [jnp.cos(a_col)] * 2, axis=1)
    sin = jnp.concatenate([-jnp.sin(a_row), jnp.sin(a_row), -jnp.sin(a_col), jnp.sin(a_col)], axis=1)
    return jnp.concatenate([cos] * A_KV, axis=1), jnp.concatenate([sin] * A_KV, axis=1)


def _merge_kernel(x_ref, mod_ref, g1_ref, g2_ref, hf_ref, hb_ref, om_ref, uc_ref, vc_ref, ya_ref,
                  gm_ref, ws_ref, bs_ref, wmerge_ref, bmerge_ref, wbr_ref, wout_ref, wq_ref, keys_ref,
                  x1_ref, h2t_ref, st_ref):
    x = x_ref[...]
    tm = x.shape[0]
    mod = mod_ref[0]
    h = _mod_norm(x, g1_ref[...], mod[1:2], mod[0:1])
    hm = hf_ref[...] + hb_ref[...]
    ym = jnp.concatenate(
        [_rms(hm[:, hh * M_DH:(hh + 1) * M_DH]) * gm_ref[:, hh * M_DH:(hh + 1) * M_DH] for hh in range(M_HEADS)],
        axis=1) * jax.nn.sigmoid(om_ref[...])
    vr = _rms(vc_ref[...])
    zc = []
    for n in range(tm // CHUNK):
        rs = slice(n * CHUNK, (n + 1) * CHUNK)
        zc.append(jnp.concatenate(
            [_mm(ws_ref[g], vr[rs, g * LANES:(g + 1) * LANES]) + _col(bs_ref[...], g) for g in range(C_GROUPS)],
            axis=1))
    yc = uc_ref[...] * jnp.concatenate(zc, axis=0)
    mixed = jnp.zeros((tm, D_MODEL), f32)
    for n, y in enumerate((ym, yc, ya_ref[...])):
        gate = jax.nn.sigmoid(_mm(h, wmerge_ref[:, n * D_MODEL:(n + 1) * D_MODEL])
                              + bmerge_ref[:, n * D_MODEL:(n + 1) * D_MODEL])
        mixed = mixed + gate * _mm(y, wbr_ref[n])
    x1 = x + mod[2:3] * _mm(mixed, wout_ref[...])
    x1_ref[...] = x1
    h2 = _mod_norm(x1, g2_ref[...], mod[4:5], mod[3:4])
    h2t_ref[...] = h2.T.astype(h2t_ref.dtype)
    qp = _mm(h2, wq_ref[...])
    for hp in range(2 * PEER_HEADS):
        st_ref[hp] = _mm_nt(keys_ref[hp], qp[:, hp * N_KEYS:(hp + 1) * N_KEYS])


def _merge(lay, x, mod_l, g1, g2, hf, hb, z, ya, gm, ws, bs_t, wmerge, bmerge, wbr, wout, wq, keys):
    tm = 256
    row = lay.mod_row(tm)
    tok = lambda w, col=0: pl.BlockSpec((tm, w), lambda i: (i, col))
    full = lambda a: pl.BlockSpec(a.shape, lambda i: (0,) * a.ndim, pipeline_mode=pl.Buffered(1))
    return pl.pallas_call(
        _merge_kernel,
        grid=(lay.t // tm,),
        in_specs=[tok(D_MODEL), pl.BlockSpec((1, 6, D_MODEL), lambda i: (row(i), 0, 0)), full(g1), full(g2),
                  tok(MIX_W), tok(MIX_W), tok(MIX_W, Z_OM // MIX_W), tok(MIX_W, Z_UC // MIX_W),
                  tok(MIX_W, Z_VC // MIX_W), tok(MIX_W),
                  full(gm), full(ws), full(bs_t), full(wmerge), full(bmerge), full(wbr), full(wout), full(wq),
                  full(keys)],
        out_specs=[tok(D_MODEL), pl.BlockSpec((D_MODEL, tm), lambda i: (0, i)),
                   pl.BlockSpec((2 * PEER_HEADS, N_KEYS, tm), lambda i: (0, 0, i))],
        out_shape=[jax.ShapeDtypeStruct((lay.t, D_MODEL), f32), jax.ShapeDtypeStruct((D_MODEL, lay.t), MXU_DTYPE),
                   jax.ShapeDtypeStruct((2 * PEER_HEADS, N_KEYS, lay.t), f32)],
        compiler_params=_params(("parallel",)),
        name="merge",
    )(x, mod_l, g1, g2, hf, hb, z, z, z, ya, gm, ws, bs_t, wmerge, bmerge, wbr, wout, wq, keys)


_CELLS = [(i, j) for i in range(PEER_TOPK) for j in range(PEER_TOPK) if (i + 1) * (j + 1) <= PEER_TOPK]


def _route_kernel(st_ref, r2_ref, b_ref, jd_ref, a_ref, val_s, rank_s, work_s):
    tb = st_ref.shape[2]
    nhp = 2 * PEER_HEADS
    unranked = jnp.full((N_KEYS, tb), float(PEER_TOPK), f32)

    def start():
        work_s[...] = st_ref[...]
        for hp in range(nhp):
            rank_s[hp] = unranked

    def extract(hp, r, rf, hit_of):
        s = work_s[hp]
        m = jnp.max(s, axis=0, keepdims=True)
        hit = hit_of(s, m)
        val_s[hp % 2, r, pl.ds(hp // 2, 1), :] = m
        rank_s[hp] = jnp.where(hit, rf, rank_s[hp])
        work_s[hp] = jnp.where(hit, -jnp.inf, s)

    start()

    def fast_round(r, _):
        rf = lax.convert_element_type(r, f32)
        for hp in range(nhp):
            extract(hp, r, rf, lambda s, m: s == m)
        return 0

    lax.fori_loop(0, PEER_TOPK, fast_round, 0)
    removed = jnp.zeros((1, tb), f32)
    for hp in range(nhp):
        removed = jnp.maximum(removed, jnp.sum(jnp.where(work_s[hp] == -jnp.inf, 1.0, 0.0), axis=0, keepdims=True))

    @pl.when(jnp.max(removed) > float(PEER_TOPK))
    def _():
        key = lax.broadcasted_iota(jnp.int32, (N_KEYS, tb), 0).astype(f32)

        def lowest_index_hit(s, m):
            return key == jnp.min(jnp.where(s == m, key, float(N_KEYS)), axis=0, keepdims=True)

        start()

        def exact_round(i, _):
            extract(i // PEER_TOPK, i % PEER_TOPK, lax.convert_element_type(i % PEER_TOPK, f32), lowest_index_hit)
            return 0

        lax.fori_loop(0, nhp * PEER_TOPK, exact_round, 0)

    v1 = [val_s[0, i] for i in range(PEER_TOPK)]
    v2 = [val_s[1, i] for i in range(PEER_TOPK)]
    cand = [v1[i] + v2[j] for (i, j) in _CELLS]
    ncell = len(_CELLS)
    before = [jnp.zeros((PEER_HEADS, tb), f32) for _ in range(ncell)]
    for x in range(ncell):
        for y in range(x + 1, ncell):
            x_first = (cand[x] >= cand[y]).astype(f32)
            before[y] = before[y] + x_first
            before[x] = before[x] + (1.0 - x_first)
    ea = [jnp.exp(v1[i] - v1[0]) for i in range(PEER_TOPK)]
    eb = [jnp.exp(v2[j] - v2[0]) for j in range(PEER_TOPK)]
    jcount = [jnp.zeros((PEER_HEADS, tb), f32) for _ in range(PEER_TOPK)]
    zsum = jnp.zeros((PEER_HEADS, tb), f32)
    for x, (i, j) in enumerate(_CELLS):
        sel = (before[x] < PEER_TOPK).astype(f32)
        jcount[i] = jcount[i] + sel
        zsum = zsum + sel * (ea[i] * eb[j])
    inv_z = 1.0 / zsum

    for h in range(PEER_HEADS):
        s1 = st_ref[2 * h]
        s2 = st_ref[2 * h + 1]
        rank1 = rank_s[2 * h]
        jd = jnp.zeros((N_KEYS, tb), f32)
        for i in range(PEER_TOPK):
            jd = jnp.where(rank1 == float(i), jcount[i][h:h + 1, :], jd)
        jd_ref[h] = _pair_words(jd)
        a_ref[h] = _pair_words(jnp.exp(s1 - v1[0][h:h + 1, :]) * inv_z[h:h + 1, :])
        b_ref[h] = jnp.exp(s2 - v2[0][h:h + 1, :]).astype(b_ref.dtype)
        r2_ref[h] = rank_s[2 * h + 1].astype(r2_ref.dtype)


def _route(st):
    t = st.shape[2]
    tb = LANES
    out = [jax.ShapeDtypeStruct((PEER_HEADS, N_KEYS, t), dt)
           for dt in (GATE_DTYPE, GATE_DTYPE, jnp.uint32, jnp.uint32)]
    spec = pl.BlockSpec((PEER_HEADS, N_KEYS, tb), lambda i: (0, 0, i))
    return pl.pallas_call(
        _route_kernel,
        grid=(t // tb,),
        in_specs=[pl.BlockSpec((2 * PEER_HEADS, N_KEYS, tb), lambda i: (0, 0, i))],
        out_specs=[spec] * 4,
        out_shape=out,
        scratch_shapes=[pltpu.VMEM((2, PEER_TOPK, PEER_HEADS, tb), f32),
                        pltpu.VMEM((2 * PEER_HEADS, N_KEYS, tb), f32),
                        pltpu.VMEM((2 * PEER_HEADS, N_KEYS, tb), f32)],
        compiler_params=_params(("parallel",)),
        name="peer_route",
    )(st)


PEER_TT = 512
PEER_E1 = 16
PEER_SUB = 4
PEER_ROWS = 16


def _pair_words(x):
    u = lax.bitcast_convert_type(x.astype(jnp.bfloat16).astype(f32), jnp.uint32)
    return u | (u >> 16)


def _bcast_pair_row(row):
    assert PEER_ROWS == 16
    return pltpu.bitcast(jnp.broadcast_to(row, (PEER_ROWS // 2, row.shape[1])), jnp.bfloat16)


def _peer_kernel(final, h2t_ref, r2_ref, b_ref, jd_ref, a_ref, u_ref, vt_ref, x1_ref, mod_ref, gf_ref,
                 o_ref, acc_s, a_s):
    j = pl.program_id(1)

    @pl.when(j == 0)
    def _():
        acc_s[...] = jnp.zeros_like(acc_s)

    sub_rows = PEER_SUB * N_KEYS
    n_sub = PEER_E1 // PEER_SUB
    srow = lambda sb: slice(sb * sub_rows, (sb + 1) * sub_rows)
    hdot = lambda sb: jnp.dot(u_ref[srow(sb), :], h2t_ref[...], preferred_element_type=f32)

    def accumulate(sb):
        acc_s[...] += jnp.dot(vt_ref[:, srow(sb)], a_s[srow(sb), :], preferred_element_type=f32)

    h_next = hdot(0)
    for sb in range(n_sub):
        h = h_next
        if sb + 1 < n_sub:
            h_next = hdot(sb + 1)
        if sb >= 1:
            accumulate(sb - 1)
        for e in range(PEER_SUB):
            e1 = sb * PEER_SUB + e
            jd = [_bcast_pair_row(jd_ref[hh, e1:e1 + 1, :]) for hh in range(PEER_HEADS)]
            aa = [_bcast_pair_row(a_ref[hh, e1:e1 + 1, :]) for hh in range(PEER_HEADS)]
            for r in range(N_KEYS // PEER_ROWS):
                rows = slice(r * PEER_ROWS, (r + 1) * PEER_ROWS)
                gate = jnp.zeros((PEER_ROWS, PEER_TT), GATE_DTYPE)
                for hh in range(PEER_HEADS):
                    hit = r2_ref[hh, rows, :] < jd[hh]
                    gate = gate + jnp.where(hit, b_ref[hh, rows, :], jnp.zeros((), GATE_DTYPE)) * aa[hh]
                lo = e * N_KEYS + r * PEER_ROWS
                act = gate.astype(f32) * jax.nn.gelu(h[lo:lo + PEER_ROWS, :])
                a_s[sb * sub_rows + lo:sb * sub_rows + lo + PEER_ROWS, :] = act.astype(a_s.dtype)
    accumulate(n_sub - 1)

    @pl.when(j == pl.num_programs(1) - 1)
    def _():
        x2 = x1_ref[...] + mod_ref[0, 5:6, :] * acc_s[...].T
        o_ref[...] = _rms(x2) * gf_ref[...] if final else x2


def _peer(lay, h2t, r2, b, jd, a, u, vt, x1, mod_l, g_final, final):
    tt = PEER_TT
    eb = PEER_E1 * N_KEYS
    row = lay.mod_row(tt)
    gate = pl.BlockSpec((PEER_HEADS, N_KEYS, tt), lambda i, j: (0, 0, i))
    gate1 = pl.BlockSpec((PEER_HEADS, PEER_E1, tt), lambda i, j: (0, j, i))
    return pl.pallas_call(
        functools.partial(_peer_kernel, final),
        grid=(lay.t // tt, N_EXPERTS // eb),
        in_specs=[pl.BlockSpec((D_MODEL, tt), lambda i, j: (0, i)), gate, gate, gate1, gate1,
                  pl.BlockSpec((eb, D_MODEL), lambda i, j: (j, 0)),
                  pl.BlockSpec((D_MODEL, eb), lambda i, j: (0, j)),
                  pl.BlockSpec((tt, D_MODEL), lambda i, j: (i, 0)),
                  pl.BlockSpec((1, 6, D_MODEL), lambda i, j: (row(i), 0, 0)),
                  pl.BlockSpec((1, D_MODEL), lambda i, j: (0, 0))],
        out_specs=pl.BlockSpec((tt, D_MODEL), lambda i, j: (i, 0)),
        out_shape=jax.ShapeDtypeStruct((lay.t, D_MODEL), f32),
        scratch_shapes=[pltpu.VMEM((D_MODEL, tt), f32), pltpu.VMEM((eb, tt), MXU_DTYPE)],
        compiler_params=_params(("parallel", "arbitrary")),
        name="peer",
    )(h2t, r2, b, jd, a, u, vt, x1, mod_l, g_final)


def _reorder_w_in(w_in_l):
    ng = 4 * M_HEADS
    a = w_in_l[:, :4 * MIX_W]
    g = w_in_l[:, 4 * MIX_W:4 * MIX_W + ng]
    rest = w_in_l[:, 4 * MIX_W + ng:]
    pad = jnp.zeros((D_MODEL, LANES - ng), w_in_l.dtype)
    return jnp.concatenate([a, rest, g, pad], axis=1).astype(MXU_DTYPE)


def _forward(x_prompt, x_sample, cache_k, cache_v, state_C, state_n, state_m, c, c_ctx,
             w_ada, b_ada, g_norm1, g_norm2, w_in, b_gates_m, g_mlstm, w_spatial, b_spatial, sink,
             w_branch, w_merge, b_merge, w_out, w_peer_q, peer_keys, peer_u, peer_v, g_final):
    nb_c, s_c, _ = x_prompt.shape
    nb_l, s_l, _ = x_sample.shape
    lay = _Layout(nb_c, s_c, nb_l, s_l)
    lc = cache_k.shape[2]
    nh = 2 * M_HEADS

    cond = jnp.concatenate([c_ctx[None, :], c], axis=0)
    cond = jnp.pad(cond, ((0, (-cond.shape[0]) % 8), (0, 0)))
    mod = _ada(cond, w_ada, b_ada).reshape(DEPTH, cond.shape[0], 6, D_MODEL)
    x = jnp.concatenate([x_prompt.reshape(lay.t_c, D_MODEL), x_sample.reshape(lay.t_l, D_MODEL)], axis=0)
    cos, sin = _rope_tables(s_l)
    ck = cache_k.reshape(nb_l, DEPTH, lc, A_KVW)
    cv = cache_v.reshape(nb_l, DEPTH, lc, A_KVW)
    zero_c = jnp.zeros((nb_c, nh, M_DH, M_DH), f32)
    zero_n = jnp.zeros((nb_c, nh, M_DH), f32)
    zero_m = jnp.zeros((nb_c, nh, LANES), f32)
    gfin = g_final.reshape(1, D_MODEL)

    ks, vs, cs, ns, ms = [], [], [], [], []
    for l in range(DEPTH):
        g1 = g_norm1[l].reshape(1, D_MODEL)
        g2 = g_norm2[l].reshape(1, D_MODEL)
        z = _inproj(lay, x, mod[l], g1, _reorder_w_in(w_in[l]))
        bias_row = jnp.pad(b_gates_m[l].reshape(1, 4 * M_HEADS), ((0, 0), (0, LANES - 4 * M_HEADS)))
        hf_c, hb_c, c_new, n_new, m_new = _mlstm(z, bias_row, zero_c, zero_n, zero_m, nb_c, s_c, 0)
        m0 = jnp.broadcast_to(state_m[:, l].reshape(nb_l, nh, 1), (nb_l, nh, LANES))
        hf_l, hb_l, _, _, _ = _mlstm(z, bias_row, state_C[:, l].reshape(nb_l, nh, M_DH, M_DH),
                                     state_n[:, l].reshape(nb_l, nh, M_DH), m0, nb_l, s_l, lay.t_c)
        ya_c = _ctx_attn(lay, z, sink, l)
        ya_l = _lat_attn(lay, z, sink, l, ck, cv, cos, sin)
        hf = jnp.concatenate([hf_c, hf_l], axis=0)
        hb = jnp.concatenate([hb_c, hb_l], axis=0)
        ya = jnp.concatenate([ya_c, ya_l], axis=0)
        x1, h2t, st = _merge(
            lay, x, mod[l], g1, g2, hf, hb, z, ya, g_mlstm[l].reshape(1, MIX_W),
            w_spatial[l].astype(MXU_DTYPE), jnp.pad(b_spatial[l].T, ((0, 0), (0, LANES - C_GROUPS))), w_merge[l].astype(MXU_DTYPE),
            b_merge[l].reshape(1, 3 * D_MODEL), w_branch[l].astype(MXU_DTYPE), w_out[l].astype(MXU_DTYPE),
            w_peer_q[l].astype(MXU_DTYPE), peer_keys[l].reshape(2 * PEER_HEADS, N_KEYS, N_KEYS).astype(MXU_DTYPE))
        r2, b, jd, a = _route(st)
        x = _peer(lay, h2t, r2, b, jd, a, peer_u[l].astype(MXU_DTYPE), peer_v[l].T.astype(MXU_DTYPE), x1, mod[l],
                  gfin, l == DEPTH - 1)
        ks.append(z[:lay.t_c, Z_KA:Z_KA + A_KVW].reshape(nb_c, s_c, A_KV, A_DH))
        vs.append(z[:lay.t_c, Z_VA:Z_VA + A_KVW].reshape(nb_c, s_c, A_KV, A_DH))
        cs.append(c_new.reshape(nb_c, 2, M_HEADS, M_DH, M_DH))
        ns.append(n_new.reshape(nb_c, 2, M_HEADS, M_DH))
        ms.append(m_new[:, :, 0].reshape(nb_c, 2, M_HEADS))
    return (x[:lay.t_c].reshape(nb_c, s_c, D_MODEL), x[lay.t_c:].reshape(nb_l, s_l, D_MODEL),
            jnp.stack(ks, axis=1), jnp.stack(vs, axis=1), jnp.stack(cs, axis=1), jnp.stack(ns, axis=1),
            jnp.stack(ms, axis=1))


def kernel(x_prompt, x_sample, cache_k, cache_v, state_C, state_n, state_m, c, c_ctx, w_ada, b_ada, g_norm1, g_norm2, w_in, b_gates_m, g_mlstm, w_spatial, b_spatial, sink, w_branch, w_merge, b_merge, w_out, w_peer_q, peer_keys, peer_u, peer_v, g_final):
    return _forward(x_prompt, x_sample, cache_k, cache_v, state_C, state_n, state_m, c, c_ctx, w_ada, b_ada,
                    g_norm1, g_norm2, w_in, b_gates_m, g_mlstm, w_spatial, b_spatial, sink, w_branch, w_merge,
                    b_merge, w_out, w_peer_q, peer_keys, peer_u, peer_v, g_final)
```

```python
import functools

import numpy as np
import jax
import jax.numpy as jnp
from jax import lax
from jax.experimental import pallas as pl
from jax.experimental.pallas import tpu as pltpu

D_MODEL = 1024
DEPTH = 2
GRID_W = 64
EPS = 1e-6
NEG_INF = -1e30
MIX_W = D_MODEL // 2
M_HEADS = 4
M_DH = MIX_W // M_HEADS
CHUNK = 128
C_GROUPS = 4
A_HEADS = 8
A_KV = 2
A_GROUP = A_HEADS // A_KV
A_DH = MIX_W // A_HEADS
A_KVW = A_KV * A_DH
ROPE_BASE = 10000.0
ROPE_FREQS = A_DH // 4
N_KEYS = 128
N_EXPERTS = N_KEYS * N_KEYS
PEER_HEADS = 8
PEER_TOPK = 16
PEER_QW = 2 * PEER_HEADS * N_KEYS

LANES = 128
MXU_DTYPE = jnp.bfloat16
GATE_DTYPE = jnp.bfloat16
VMEM_LIMIT = 56 * 1024 * 1024

Z_QM, Z_KM, Z_VM, Z_OM, Z_UC, Z_VC, Z_QA = (i * MIX_W for i in range(7))
Z_KA = 7 * MIX_W
Z_VA = Z_KA + A_KVW
Z_GM = Z_VA + A_KVW
Z_W = Z_GM + LANES

f32 = jnp.float32


def _params(sem):
    return pltpu.CompilerParams(dimension_semantics=sem, vmem_limit_bytes=VMEM_LIMIT)


def _mm(a, b):
    return jnp.dot(a.astype(MXU_DTYPE), b.astype(MXU_DTYPE), preferred_element_type=f32)


def _mm_nt(a, b):
    return lax.dot_general(a.astype(MXU_DTYPE), b.astype(MXU_DTYPE), (((1,), (1,)), ((), ())),
                           preferred_element_type=f32)


def _split3(x):
    hi = x.astype(jnp.bfloat16)
    r1 = x - hi.astype(f32)
    mid = r1.astype(jnp.bfloat16)
    lo = (r1 - mid.astype(f32)).astype(jnp.bfloat16)
    return hi, mid, lo


def _mm_exact_lhs(a01, x):
    a = a01.astype(jnp.bfloat16)
    hi, mid, lo = _split3(x)
    return (jnp.dot(a, hi, preferred_element_type=f32) + jnp.dot(a, mid, preferred_element_type=f32)
            + jnp.dot(a, lo, preferred_element_type=f32))


def _mm3(a, b):
    ah, am, al = _split3(a)
    bh, bm, bl = _split3(b)
    d = functools.partial(jnp.dot, preferred_element_type=f32)
    return (d(ah, bh) + (d(ah, bm) + d(am, bh)) + (d(ah, bl) + d(al, bh) + d(am, bm)))


def _rms(x):
    return x * lax.rsqrt(jnp.mean(x * x, axis=-1, keepdims=True) + EPS)


def _mod_norm(x, g, scale, shift):
    return _rms(x) * g * (1.0 + scale) + shift


def _ada_kernel(c_ref, w_ref, b_ref, o_ref):
    c = c_ref[...]
    o_ref[0] = _mm3(c * jax.nn.sigmoid(c), w_ref[0]) + b_ref[0]


def _ada(cond, w_ada, b_ada):
    rows = cond.shape[0]
    tn = 1536
    return pl.pallas_call(
        _ada_kernel,
        grid=(DEPTH, 6 * D_MODEL // tn),
        in_specs=[pl.BlockSpec((rows, D_MODEL), lambda l, j: (0, 0)),
                  pl.BlockSpec((1, D_MODEL, tn), lambda l, j: (l, 0, j)),
                  pl.BlockSpec((1, 1, tn), lambda l, j: (l, 0, j))],
        out_specs=pl.BlockSpec((1, rows, tn), lambda l, j: (l, 0, j)),
        out_shape=jax.ShapeDtypeStruct((DEPTH, rows, 6 * D_MODEL), f32),
        compiler_params=_params(("parallel", "parallel")),
        name="ada",
    )(cond, w_ada, b_ada.reshape(DEPTH, 1, 6 * D_MODEL))


class _Group:
    def __init__(self, nb, s, cond0, per_sequence):
        self.nb, self.s, self.cond0, self.per_sequence = nb, s, cond0, per_sequence
        self.t = nb * s

    def mod_row(self, tm):
        assert self.t % tm == 0
        if not self.per_sequence:
            return lambda i: self.cond0
        assert self.s % tm == 0
        return lambda i: self.cond0 + i // (self.s // tm)


def _inproj_kernel(x_ref, mod_ref, g1_ref, w_ref, z_ref, *kv_refs):
    h = _mod_norm(x_ref[...], g1_ref[...], mod_ref[0, 1:2, :], mod_ref[0, 0:1, :])
    z = _mm(h, w_ref[...])
    z_ref[...] = z
    if kv_refs:
        kv_refs[0][...] = z[:, Z_KA:Z_KA + A_KVW]
        kv_refs[1][...] = z[:, Z_VA:Z_VA + A_KVW]


def _inproj(grp, x, mod_l, g1, w, emit_kv):
    tm = 256
    row = grp.mod_row(tm)
    kv_spec = [pl.BlockSpec((tm, A_KVW), lambda i: (i, 0))] * 2 if emit_kv else []
    kv_shape = [jax.ShapeDtypeStruct((grp.t, A_KVW), f32)] * 2 if emit_kv else []
    return pl.pallas_call(
        _inproj_kernel,
        grid=(grp.t // tm,),
        in_specs=[pl.BlockSpec((tm, D_MODEL), lambda i: (i, 0)),
                  pl.BlockSpec((1, 6, D_MODEL), lambda i: (row(i), 0, 0)),
                  pl.BlockSpec((1, D_MODEL), lambda i: (0, 0)),
                  pl.BlockSpec((D_MODEL, Z_W), lambda i: (0, 0), pipeline_mode=pl.Buffered(1))],
        out_specs=[pl.BlockSpec((tm, Z_W), lambda i: (i, 0))] + kv_spec,
        out_shape=[jax.ShapeDtypeStruct((grp.t, Z_W), f32)] + kv_shape,
        compiler_params=_params(("parallel",)),
        name="inproj",
    )(x, mod_l, g1, w)


def _log_sigmoid(x):
    return jnp.minimum(x, 0.0) - jnp.log1p(jnp.exp(-jnp.abs(x)))


def _col(a, j):
    lane = lax.broadcasted_iota(jnp.int32, a.shape, 1)
    return jnp.sum(jnp.where(lane == j, a, 0.0), axis=1, keepdims=True)


def _mlstm_kernel(qf_ref, kf_ref, vf_ref, gf_ref, qb_ref, kb_ref, vb_ref, gb_ref, bias_ref,
                  c0_ref, n0_ref, m0_ref, hf_ref, hb_ref, c_out, n_out, m_out, ct_s, n_s, m_s):
    c = pl.program_id(1)

    @pl.when(c == 0)
    def _():
        for idx in range(2 * M_HEADS):
            ct_s[idx] = c0_ref[0, idx].T
        n_s[...] = n0_ref[0]
        m_s[...] = m0_ref[0]

    ri = lax.broadcasted_iota(jnp.int32, (CHUNK, CHUNK), 0)
    ci = lax.broadcasted_iota(jnp.int32, (CHUNK, CHUNK), 1)
    lane = lax.broadcasted_iota(jnp.int32, (CHUNK, LANES), 1)
    is_forget = ((lane // M_HEADS) % 2) == 1

    for d, (q_ref, k_ref, v_ref, g_ref, h_ref) in enumerate(
            ((qf_ref, kf_ref, vf_ref, gf_ref, hf_ref), (qb_ref, kb_ref, vb_ref, gb_ref, hb_ref))):
        rev = d == 1
        keep = (ri <= ci) if rev else (ri >= ci)
        g = g_ref[...] + bias_ref[...]
        lg = jnp.where(is_forget, _log_sigmoid(g), g)
        bc = _mm_exact_lhs(keep, lg)
        bct = bc.T
        lgt = lg.T
        te = 0 if rev else CHUNK - 1
        for hh in range(M_HEADS):
            idx = d * M_HEADS + hh
            j_li = 2 * d * M_HEADS + hh
            j_lf = j_li + M_HEADS
            sl = slice(hh * M_DH, (hh + 1) * M_DH)
            q = q_ref[:, sl]
            k = k_ref[:, sl] * (M_DH ** -0.5)
            v = v_ref[:, sl]
            b_col = _col(bc, j_lf)
            li_col = _col(lg, j_li)
            b_row = bct[j_lf:j_lf + 1, :]
            li_row = lgt[j_li:j_li + 1, :]
            mp = m_s[idx:idx + 1, 0:1]
            a_col = b_col + mp
            dm = jnp.where(keep, b_col - b_row + li_row, -jnp.inf)
            m_t = jnp.maximum(a_col, jnp.max(dm, axis=1, keepdims=True))
            w = jnp.exp(dm - m_t)
            w0 = jnp.exp(a_col - m_t)
            kt = k.T
            s = w * _mm(q, kt)
            ct = ct_s[idx]
            n_row = n_s[idx:idx + 1, :]
            num = w0 * _mm(q, ct) + _mm(s, v)
            den = w0 * jnp.sum(q * n_row, axis=1, keepdims=True) + jnp.sum(s, axis=1, keepdims=True)
            h_ref[:, sl] = num / jnp.maximum(jnp.abs(den), jnp.exp(-m_t))
            m_end = m_t[te:te + 1, :]
            w0_end = w0[te:te + 1, :]
            w_end = jnp.exp(b_col[te:te + 1, :] - b_col + li_col - m_end)
            ct_s[idx] = w0_end * ct + _mm(kt, v * w_end)
            n_s[idx:idx + 1, :] = w0_end * n_row + jnp.sum(k * w_end, axis=0, keepdims=True)
            m_s[idx:idx + 1, :] = jnp.broadcast_to(m_end, (1, LANES))

    @pl.when(c == pl.num_programs(1) - 1)
    def _():
        for idx in range(2 * M_HEADS):
            c_out[0, idx] = ct_s[idx].T
        n_out[0] = n_s[...]
        m_out[0] = m_s[...]


def _mlstm(z, bias_row, c0, n0, m0, nb, s):
    nc = s // CHUNK
    nh = 2 * M_HEADS

    def fwd(col):
        return lambda b, c: (b * nc + c, col)

    def bwd(col):
        return lambda b, c: (b * nc + (nc - 1 - c), col)

    wide = lambda im: pl.BlockSpec((CHUNK, MIX_W), im)
    gate = lambda im: pl.BlockSpec((CHUNK, LANES), im)
    state = lambda shape: pl.BlockSpec((1,) + shape, lambda b, c: (b,) + (0,) * len(shape))
    gcol = Z_GM // LANES
    return pl.pallas_call(
        _mlstm_kernel,
        grid=(nb, nc),
        in_specs=[wide(fwd(0)), wide(fwd(1)), wide(fwd(2)), gate(fwd(gcol)),
                  wide(bwd(0)), wide(bwd(1)), wide(bwd(2)), gate(bwd(gcol)),
                  pl.BlockSpec((1, LANES), lambda b, c: (0, 0)),
                  state((nh, M_DH, M_DH)), state((nh, M_DH)), state((nh, LANES))],
        out_specs=[pl.BlockSpec((CHUNK, MIX_W), lambda b, c: (b * nc + c, 0)),
                   pl.BlockSpec((CHUNK, MIX_W), lambda b, c: (b * nc + (nc - 1 - c), 0)),
                   state((nh, M_DH, M_DH)), state((nh, M_DH)), state((nh, LANES))],
        out_shape=[jax.ShapeDtypeStruct((nb * s, MIX_W), f32), jax.ShapeDtypeStruct((nb * s, MIX_W), f32),
                   jax.ShapeDtypeStruct((nb, nh, M_DH, M_DH), f32), jax.ShapeDtypeStruct((nb, nh, M_DH), f32),
                   jax.ShapeDtypeStruct((nb, nh, LANES), f32)],
        scratch_shapes=[pltpu.VMEM((nh, M_DH, M_DH), f32), pltpu.VMEM((nh, M_DH), f32),
                        pltpu.VMEM((nh, LANES), f32)],
        compiler_params=_params(("parallel", "arbitrary")),
        name="mlstm",
    )(z, z, z, z, z, z, z, z, bias_row, c0, n0, m0)


def _half_placements(a):
    lane = lax.broadcasted_iota(jnp.int32, a.shape, 1)
    g0 = jnp.where(lane < A_DH, a, 0.0)
    g1 = jnp.where(lane >= A_DH, a, 0.0)
    return ((g0, pltpu.roll(g0, A_DH, 1)), (pltpu.roll(g1, A_DH, 1), g1))


def _group_attend(q, k_all, v_all, sink_ref, l, keep):
    kz = _half_placements(k_all)
    vz = _half_placements(v_all)
    slabs = []
    for slab in range(A_HEADS // 2):
        acc = None
        for pos in range(2):
            head = 2 * slab + pos
            g = head // A_GROUP
            s = _mm_nt(q[:, slab * LANES:(slab + 1) * LANES], kz[g][pos]) * (A_DH ** -0.5)
            if keep is not None:
                s = jnp.where(keep, s, NEG_INF)
            sk = sink_ref[l, head]
            m = jnp.maximum(jnp.max(s, axis=1, keepdims=True), sk)
            p = jnp.exp(s - m)
            den = jnp.sum(p, axis=1, keepdims=True) + jnp.exp(sk - m)
            o = _mm(p, vz[g][pos]) / den
            acc = o if acc is None else acc + o
        slabs.append(acc)
    return jnp.concatenate(slabs, axis=1)


def _ctx_attn_kernel(l, sink_ref, q_ref, k_ref, v_ref, o_ref):
    o_ref[...] = _group_attend(q_ref[...], k_ref[...], v_ref[...], sink_ref, l, None)


def _ctx_attn(grp, z, sink, l):
    s = grp.s
    return pl.pallas_call(
        functools.partial(_ctx_attn_kernel, l),
        grid=(grp.nb,),
        in_specs=[pl.BlockSpec(memory_space=pltpu.SMEM),
                  pl.BlockSpec((s, MIX_W), lambda b: (b, Z_QA // MIX_W)),
                  pl.BlockSpec((s, A_KVW), lambda b: (b, Z_KA // A_KVW)),
                  pl.BlockSpec((s, A_KVW), lambda b: (b, Z_VA // A_KVW))],
        out_specs=pl.BlockSpec((s, MIX_W), lambda b: (b, 0)),
        out_shape=jax.ShapeDtypeStruct((grp.t, MIX_W), f32),
        compiler_params=_params(("parallel",)),
        name="ctx_attn",
    )(sink, z, z, z)


def _rope(x, cos, sin):
    lane = lax.broadcasted_iota(jnp.int32, cos.shape, 1)
    first = (lane % (2 * ROPE_FREQS)) < ROPE_FREQS
    out = []
    for j in range(x.shape[1] // LANES):
        xs = x[:, j * LANES:(j + 1) * LANES]
        partner = jnp.where(first, pltpu.roll(xs, LANES - ROPE_FREQS, 1), pltpu.roll(xs, ROPE_FREQS, 1))
        out.append(xs * cos + partner * sin)
    return out[0] if len(out) == 1 else jnp.concatenate(out, axis=1)


def _lat_attn_kernel(l, nblk, sink_ref, q_ref, kp_ref, kc_ref, kn_ref, vp_ref, vc_ref, vn_ref,
                     cq_ref, sq_ref, cp_ref, sp_ref, cn_ref, sn_ref, ck_ref, cv_ref, o_ref):
    i = pl.program_id(1)
    q = _rope(q_ref[...], cq_ref[...], sq_ref[...])
    k_all = jnp.concatenate([ck_ref[0, 0],
                             _rope(kp_ref[...], cp_ref[...], sp_ref[...]),
                             _rope(kc_ref[...], cq_ref[...], sq_ref[...]),
                             _rope(kn_ref[...], cn_ref[...], sn_ref[...])], axis=0)
    v_all = jnp.concatenate([cv_ref[0, 0], vp_ref[...], vc_ref[...], vn_ref[...]], axis=0)
    lc = ck_ref.shape[2]
    nk = lc + 3 * CHUNK
    r = lax.broadcasted_iota(jnp.int32, (CHUNK, nk), 0)
    cc = lax.broadcasted_iota(jnp.int32, (CHUNK, nk), 1) - lc
    lo = jnp.maximum(r, jnp.where(i == 0, CHUNK, 0))
    hi = jnp.minimum(r + 2 * CHUNK, jnp.where(i == nblk - 1, 2 * CHUNK - 1, 3 * CHUNK))
    keep = (cc < 0) | ((cc >= lo) & (cc <= hi))
    o_ref[...] = _group_attend(q, k_all, v_all, sink_ref, l, keep)


def _lat_attn(grp, z, sink, l, cache_k, cache_v, cos, sin):
    s, nb = grp.s, grp.nb
    nblk = s // CHUNK
    lc = cache_k.shape[2]

    def rows(off):
        return lambda b, i: b * nblk + jnp.clip(i + off, 0, nblk - 1)

    def zspec(width, col, off):
        rf = rows(off)
        return pl.BlockSpec((CHUNK, width), lambda b, i: (rf(b, i), col))

    def tab(off):
        return pl.BlockSpec((CHUNK, LANES), lambda b, i: (jnp.clip(i + off, 0, nblk - 1), 0))

    cache = pl.BlockSpec((1, 1, lc, A_KVW), lambda b, i: (b, l, 0, 0))
    kcol, vcol = Z_KA // A_KVW, Z_VA // A_KVW
    return pl.pallas_call(
        functools.partial(_lat_attn_kernel, l, nblk),
        grid=(nb, nblk),
        in_specs=[pl.BlockSpec(memory_space=pltpu.SMEM),
                  zspec(MIX_W, Z_QA // MIX_W, 0),
                  zspec(A_KVW, kcol, -1), zspec(A_KVW, kcol, 0), zspec(A_KVW, kcol, 1),
                  zspec(A_KVW, vcol, -1), zspec(A_KVW, vcol, 0), zspec(A_KVW, vcol, 1),
                  tab(0), tab(0), tab(-1), tab(-1), tab(1), tab(1), cache, cache],
        out_specs=pl.BlockSpec((CHUNK, MIX_W), lambda b, i: (b * nblk + i, 0)),
        out_shape=jax.ShapeDtypeStruct((grp.t, MIX_W), f32),
        compiler_params=_params(("parallel", "parallel")),
        name="lat_attn",
    )(sink, z, z, z, z, z, z, z, cos, sin, cos, sin, cos, sin, cache_k, cache_v)


def _rope_tables(s):
    t = jnp.arange(s)
    freqs = ROPE_BASE ** (-jnp.arange(ROPE_FREQS, dtype=f32) / ROPE_FREQS)
    a_row = (t // GRID_W).astype(f32)[:, None] * freqs[None, :]
    a_col = (t % GRID_W).astype(f32)[:, None] * freqs[None, :]
    cos = jnp.concatenate([jnp.cos(a_row)] * 2 + [jnp.cos(a_col)] * 2, axis=1)
    sin = jnp.concatenate([-jnp.sin(a_row), jnp.sin(a_row), -jnp.sin(a_col), jnp.sin(a_col)], axis=1)
    return jnp.concatenate([cos] * A_KV, axis=1), jnp.concatenate([sin] * A_KV, axis=1)


def _merge_kernel(x_ref, mod_ref, g1_ref, g2_ref, hf_ref, hb_ref, om_ref, uc_ref, vc_ref, ya_ref,
                  gm_ref, ws_ref, bs_ref, wmerge_ref, bmerge_ref, wbr_ref, wout_ref, wq_ref, keys_ref,
                  x1_ref, h2t_ref, st_ref):
    x = x_ref[...]
    tm = x.shape[0]
    mod = mod_ref[0]
    h = _mod_norm(x, g1_ref[...], mod[1:2], mod[0:1])
    hm = hf_ref[...] + hb_ref[...]
    ym = jnp.concatenate(
        [_rms(hm[:, hh * M_DH:(hh + 1) * M_DH]) * gm_ref[:, hh * M_DH:(hh + 1) * M_DH] for hh in range(M_HEADS)],
        axis=1) * jax.nn.sigmoid(om_ref[...])
    vr = _rms(vc_ref[...])
    zc = []
    for n in range(tm // CHUNK):
        rs = slice(n * CHUNK, (n + 1) * CHUNK)
        zc.append(jnp.concatenate(
            [_mm(ws_ref[g], vr[rs, g * LANES:(g + 1) * LANES]) + _col(bs_ref[...], g) for g in range(C_GROUPS)],
            axis=1))
    yc = uc_ref[...] * jnp.concatenate(zc, axis=0)
    mixed = jnp.zeros((tm, D_MODEL), f32)
    for n, y in enumerate((ym, yc, ya_ref[...])):
        gate = jax.nn.sigmoid(_mm(h, wmerge_ref[:, n * D_MODEL:(n + 1) * D_MODEL])
                              + bmerge_ref[:, n * D_MODEL:(n + 1) * D_MODEL])
        mixed = mixed + gate * _mm(y, wbr_ref[n])
    x1 = x + mod[2:3] * _mm(mixed, wout_ref[...])
    x1_ref[...] = x1
    h2 = _mod_norm(x1, g2_ref[...], mod[4:5], mod[3:4])
    h2t_ref[...] = h2.T.astype(h2t_ref.dtype)
    qp = _mm(h2, wq_ref[...])
    for hp in range(2 * PEER_HEADS):
        st_ref[hp] = _mm_nt(keys_ref[hp], qp[:, hp * N_KEYS:(hp + 1) * N_KEYS])


def _merge(grp, x, mod_l, g1, g2, hf, hb, z, ya, gm, ws, bs_t, wmerge, bmerge, wbr, wout, wq, keys):
    tm = 256
    row = grp.mod_row(tm)
    tok = lambda w, col=0: pl.BlockSpec((tm, w), lambda i: (i, col))
    full = lambda a: pl.BlockSpec(a.shape, lambda i: (0,) * a.ndim, pipeline_mode=pl.Buffered(1))
    return pl.pallas_call(
        _merge_kernel,
        grid=(grp.t // tm,),
        in_specs=[tok(D_MODEL), pl.BlockSpec((1, 6, D_MODEL), lambda i: (row(i), 0, 0)), full(g1), full(g2),
                  tok(MIX_W), tok(MIX_W), tok(MIX_W, Z_OM // MIX_W), tok(MIX_W, Z_UC // MIX_W),
                  tok(MIX_W, Z_VC // MIX_W), tok(MIX_W),
                  full(gm), full(ws), full(bs_t), full(wmerge), full(bmerge), full(wbr), full(wout), full(wq),
                  full(keys)],
        out_specs=[tok(D_MODEL), pl.BlockSpec((D_MODEL, tm), lambda i: (0, i)),
                   pl.BlockSpec((2 * PEER_HEADS, N_KEYS, tm), lambda i: (0, 0, i))],
        out_shape=[jax.ShapeDtypeStruct((grp.t, D_MODEL), f32), jax.ShapeDtypeStruct((D_MODEL, grp.t), MXU_DTYPE),
                   jax.ShapeDtypeStruct((2 * PEER_HEADS, N_KEYS, grp.t), f32)],
        compiler_params=_params(("parallel",)),
        name="merge",
    )(x, mod_l, g1, g2, hf, hb, z, z, z, ya, gm, ws, bs_t, wmerge, bmerge, wbr, wout, wq, keys)


_CELLS = [(i, j) for i in range(PEER_TOPK) for j in range(PEER_TOPK) if (i + 1) * (j + 1) <= PEER_TOPK]


def _route_kernel(st_ref, r2_ref, b_ref, jd_ref, a_ref, val_s, rank_s, work_s):
    tb = st_ref.shape[2]
    nhp = 2 * PEER_HEADS
    unranked = jnp.full((N_KEYS, tb), float(PEER_TOPK), f32)

    def start():
        work_s[...] = st_ref[...]
        for hp in range(nhp):
            rank_s[hp] = unranked

    def extract(hp, r, rf, hit_of):
        s = work_s[hp]
        m = jnp.max(s, axis=0, keepdims=True)
        hit = hit_of(s, m)
        val_s[hp % 2, r, pl.ds(hp // 2, 1), :] = m
        rank_s[hp] = jnp.where(hit, rf, rank_s[hp])
        work_s[hp] = jnp.where(hit, -jnp.inf, s)

    start()

    def fast_round(r, _):
        rf = lax.convert_element_type(r, f32)
        for hp in range(nhp):
            extract(hp, r, rf, lambda s, m: s == m)
        return 0

    lax.fori_loop(0, PEER_TOPK, fast_round, 0)
    removed = jnp.zeros((1, tb), f32)
    for hp in range(nhp):
        removed = jnp.maximum(removed, jnp.sum(jnp.where(work_s[hp] == -jnp.inf, 1.0, 0.0), axis=0, keepdims=True))

    @pl.when(jnp.max(removed) > float(PEER_TOPK))
    def _():
        key = lax.broadcasted_iota(jnp.int32, (N_KEYS, tb), 0).astype(f32)

        def lowest_index_hit(s, m):
            return key == jnp.min(jnp.where(s == m, key, float(N_KEYS)), axis=0, keepdims=True)

        start()

        def exact_round(i, _):
            extract(i // PEER_TOPK, i % PEER_TOPK, lax.convert_element_type(i % PEER_TOPK, f32), lowest_index_hit)
            return 0

        lax.fori_loop(0, nhp * PEER_TOPK, exact_round, 0)

    v1 = [val_s[0, i] for i in range(PEER_TOPK)]
    v2 = [val_s[1, i] for i in range(PEER_TOPK)]
    cand = [v1[i] + v2[j] for (i, j) in _CELLS]
    ncell = len(_CELLS)
    before = [jnp.zeros((PEER_HEADS, tb), f32) for _ in range(ncell)]
    for x in range(ncell):
        for y in range(x + 1, ncell):
            x_first = (cand[x] >= cand[y]).astype(f32)
            before[y] = before[y] + x_first
            before[x] = before[x] + (1.0 - x_first)
    ea = [jnp.exp(v1[i] - v1[0]) for i in range(PEER_TOPK)]
    eb = [jnp.exp(v2[j] - v2[0]) for j in range(PEER_TOPK)]
    jcount = [jnp.zeros((PEER_HEADS, tb), f32) for _ in range(PEER_TOPK)]
    zsum = jnp.zeros((PEER_HEADS, tb), f32)
    for x, (i, j) in enumerate(_CELLS):
        sel = (before[x] < PEER_TOPK).astype(f32)
        jcount[i] = jcount[i] + sel
        zsum = zsum + sel * (ea[i] * eb[j])
    inv_z = 1.0 / zsum

    for h in range(PEER_HEADS):
        s1 = st_ref[2 * h]
        s2 = st_ref[2 * h + 1]
        rank1 = rank_s[2 * h]
        jd = jnp.zeros((N_KEYS, tb), f32)
        for i in range(PEER_TOPK):
            jd = jnp.where(rank1 == float(i), jcount[i][h:h + 1, :], jd)
        jd_ref[h] = _pair_words(jd)
        a_ref[h] = _pair_words(jnp.exp(s1 - v1[0][h:h + 1, :]) * inv_z[h:h + 1, :])
        b_ref[h] = jnp.exp(s2 - v2[0][h:h + 1, :]).astype(b_ref.dtype)
        r2_ref[h] = rank_s[2 * h + 1].astype(r2_ref.dtype)


def _route(st):
    t = st.shape[2]
    tb = LANES
    out = [jax.ShapeDtypeStruct((PEER_HEADS, N_KEYS, t), dt)
           for dt in (GATE_DTYPE, GATE_DTYPE, jnp.uint32, jnp.uint32)]
    spec = pl.BlockSpec((PEER_HEADS, N_KEYS, tb), lambda i: (0, 0, i))
    return pl.pallas_call(
        _route_kernel,
        grid=(t // tb,),
        in_specs=[pl.BlockSpec((2 * PEER_HEADS, N_KEYS, tb), lambda i: (0, 0, i))],
        out_specs=[spec] * 4,
        out_shape=out,
        scratch_shapes=[pltpu.VMEM((2, PEER_TOPK, PEER_HEADS, tb), f32),
                        pltpu.VMEM((2 * PEER_HEADS, N_KEYS, tb), f32),
                        pltpu.VMEM((2 * PEER_HEADS, N_KEYS, tb), f32)],
        compiler_params=_params(("parallel",)),
        name="peer_route",
    )(st)


PEER_TT = 512
PEER_E1 = 16
PEER_SUBS = (4, 4, 4, 4)
PEER_ROWS = 16


def _pair_words(x):
    u = lax.bitcast_convert_type(x.astype(jnp.bfloat16).astype(f32), jnp.uint32)
    return u | (u >> 16)


def _bcast_pair_row(row):
    assert PEER_ROWS == 16
    return pltpu.bitcast(jnp.broadcast_to(row, (PEER_ROWS // 2, row.shape[1])), jnp.bfloat16)


def _peer_kernel(final, h2t_ref, r2_ref, b_ref, jd_ref, a_ref, u_ref, vt_ref, x1_ref, mod_ref, gf_ref,
                 o_ref, acc_s, a_s, h_s):
    j = pl.program_id(1)

    @pl.when(j == 0)
    def _():
        acc_s[...] = jnp.zeros_like(acc_s)

    assert sum(PEER_SUBS) == PEER_E1
    starts = [sum(PEER_SUBS[:sb]) for sb in range(len(PEER_SUBS))]
    n_sub = len(PEER_SUBS)
    srow = lambda sb: slice(starts[sb] * N_KEYS, (starts[sb] + PEER_SUBS[sb]) * N_KEYS)
    def hdot(sb):
        h_s[sb % 2, :PEER_SUBS[sb] * N_KEYS, :] = jnp.dot(u_ref[srow(sb), :], h2t_ref[...], preferred_element_type=f32)

    def accumulate(sb):
        acc_s[...] += jnp.dot(vt_ref[:, srow(sb)], a_s[srow(sb), :], preferred_element_type=f32)

    hdot(0)
    for sb in range(n_sub):
        h = h_s.at[sb % 2]
        if sb + 1 < n_sub:
            hdot(sb + 1)
        if sb >= 1:
            accumulate(sb - 1)
        for e in range(PEER_SUBS[sb]):
            e1 = starts[sb] + e
            n_chunk = N_KEYS // PEER_ROWS
            gates = [None] * n_chunk
            for hh in range(PEER_HEADS):
                jd = _bcast_pair_row(jd_ref[hh, e1:e1 + 1, :])
                aa = _bcast_pair_row(a_ref[hh, e1:e1 + 1, :])
                for r in range(n_chunk):
                    rows = slice(r * PEER_ROWS, (r + 1) * PEER_ROWS)
                    term = jnp.where(r2_ref[hh, rows, :] < jd, b_ref[hh, rows, :], jnp.zeros((), GATE_DTYPE)) * aa
                    gates[r] = term if gates[r] is None else gates[r] + term
            for r in range(n_chunk):
                lo = e * N_KEYS + r * PEER_ROWS
                act = gates[r].astype(f32) * jax.nn.gelu(h[lo:lo + PEER_ROWS, :])
                a_s[e1 * N_KEYS + r * PEER_ROWS:e1 * N_KEYS + (r + 1) * PEER_ROWS, :] = act.astype(a_s.dtype)
    accumulate(n_sub - 1)

    @pl.when(j == pl.num_programs(1) - 1)
    def _():
        x2 = x1_ref[...] + mod_ref[0, 5:6, :] * acc_s[...].T
        o_ref[...] = _rms(x2) * gf_ref[...] if final else x2


def _peer(grp, h2t, r2, b, jd, a, u, vt, x1, mod_l, g_final, final):
    tt = PEER_TT
    eb = PEER_E1 * N_KEYS
    row = grp.mod_row(tt)
    gate = pl.BlockSpec((PEER_HEADS, N_KEYS, tt), lambda i, j: (0, 0, i))
    gate1 = pl.BlockSpec((PEER_HEADS, PEER_E1, tt), lambda i, j: (0, j, i))
    return pl.pallas_call(
        functools.partial(_peer_kernel, final),
        grid=(grp.t // tt, N_EXPERTS // eb),
        in_specs=[pl.BlockSpec((D_MODEL, tt), lambda i, j: (0, i)), gate, gate, gate1, gate1,
                  pl.BlockSpec((eb, D_MODEL), lambda i, j: (j, 0)),
                  pl.BlockSpec((D_MODEL, eb), lambda i, j: (0, j)),
                  pl.BlockSpec((tt, D_MODEL), lambda i, j: (i, 0)),
                  pl.BlockSpec((1, 6, D_MODEL), lambda i, j: (row(i), 0, 0)),
                  pl.BlockSpec((1, D_MODEL), lambda i, j: (0, 0))],
        out_specs=pl.BlockSpec((tt, D_MODEL), lambda i, j: (i, 0)),
        out_shape=jax.ShapeDtypeStruct((grp.t, D_MODEL), f32),
        scratch_shapes=[pltpu.VMEM((D_MODEL, tt), f32), pltpu.VMEM((eb, tt), MXU_DTYPE),
                        pltpu.VMEM((2, max(PEER_SUBS) * N_KEYS, tt), f32)],
        compiler_params=_params(("parallel", "arbitrary")),
        name="peer",
    )(h2t, r2, b, jd, a, u, vt, x1, mod_l, g_final)


def _reorder_w_in(w_in_l):
    ng = 4 * M_HEADS
    a = w_in_l[:, :4 * MIX_W]
    g = w_in_l[:, 4 * MIX_W:4 * MIX_W + ng]
    rest = w_in_l[:, 4 * MIX_W + ng:]
    pad = jnp.zeros((D_MODEL, LANES - ng), w_in_l.dtype)
    return jnp.concatenate([a, rest, g, pad], axis=1).astype(MXU_DTYPE)


def _forward(x_prompt, x_sample, cache_k, cache_v, state_C, state_n, state_m, c, c_ctx,
             w_ada, b_ada, g_norm1, g_norm2, w_in, b_gates_m, g_mlstm, w_spatial, b_spatial, sink,
             w_branch, w_merge, b_merge, w_out, w_peer_q, peer_keys, peer_u, peer_v, g_final):
    nb_c, s_c, _ = x_prompt.shape
    nb_l, s_l, _ = x_sample.shape
    ctx = _Group(nb_c, s_c, 0, False)
    lat = _Group(nb_l, s_l, 1, True)
    lc = cache_k.shape[2]
    nh = 2 * M_HEADS

    cond = jnp.concatenate([c_ctx[None, :], c], axis=0)
    cond = jnp.pad(cond, ((0, (-cond.shape[0]) % 8), (0, 0)))
    mod = _ada(cond, w_ada, b_ada).reshape(DEPTH, cond.shape[0], 6, D_MODEL)
    xc = x_prompt.reshape(ctx.t, D_MODEL)
    xl = x_sample.reshape(lat.t, D_MODEL)
    cos, sin = _rope_tables(s_l)
    ck = cache_k.reshape(nb_l, DEPTH, lc, A_KVW)
    cv = cache_v.reshape(nb_l, DEPTH, lc, A_KVW)
    zero_c = jnp.zeros((nb_c, nh, M_DH, M_DH), f32)
    zero_n = jnp.zeros((nb_c, nh, M_DH), f32)
    zero_m = jnp.zeros((nb_c, nh, LANES), f32)
    gfin = g_final.reshape(1, D_MODEL)

    ks, vs, cs, ns, ms = [], [], [], [], []
    for l in range(DEPTH):
        g1 = g_norm1[l].reshape(1, D_MODEL)
        g2 = g_norm2[l].reshape(1, D_MODEL)
        w_in_l = _reorder_w_in(w_in[l])
        bias_row = jnp.pad(b_gates_m[l].reshape(1, 4 * M_HEADS), ((0, 0), (0, LANES - 4 * M_HEADS)))
        merge_w = (g_mlstm[l].reshape(1, MIX_W), w_spatial[l].astype(MXU_DTYPE),
                   jnp.pad(b_spatial[l].T, ((0, 0), (0, LANES - C_GROUPS))), w_merge[l].astype(MXU_DTYPE),
                   b_merge[l].reshape(1, 3 * D_MODEL), w_branch[l].astype(MXU_DTYPE), w_out[l].astype(MXU_DTYPE),
                   w_peer_q[l].astype(MXU_DTYPE),
                   peer_keys[l].reshape(2 * PEER_HEADS, N_KEYS, N_KEYS).astype(MXU_DTYPE))
        u = peer_u[l].astype(MXU_DTYPE)
        vt = peer_v[l].T.astype(MXU_DTYPE)
        final = l == DEPTH - 1

        zc, k_new, v_new = _inproj(ctx, xc, mod[l], g1, w_in_l, True)
        hf, hb, c_new, n_new, m_new = _mlstm(zc, bias_row, zero_c, zero_n, zero_m, nb_c, s_c)
        ya = _ctx_attn(ctx, zc, sink, l)
        x1, h2t, st = _merge(ctx, xc, mod[l], g1, g2, hf, hb, zc, ya, *merge_w)
        xc = _peer(ctx, h2t, *_route(st), u, vt, x1, mod[l], gfin, final)
        ks.append(k_new.reshape(nb_c, s_c, A_KV, A_DH))
        vs.append(v_new.reshape(nb_c, s_c, A_KV, A_DH))
        cs.append(c_new.reshape(nb_c, 2, M_HEADS, M_DH, M_DH))
        ns.append(n_new.reshape(nb_c, 2, M_HEADS, M_DH))
        ms.append(m_new[:, :, 0].reshape(nb_c, 2, M_HEADS))

        zl = _inproj(lat, xl, mod[l], g1, w_in_l, False)[0]
        m0 = jnp.broadcast_to(state_m[:, l].reshape(nb_l, nh, 1), (nb_l, nh, LANES))
        hf, hb, _, _, _ = _mlstm(zl, bias_row, state_C[:, l].reshape(nb_l, nh, M_DH, M_DH),
                                 state_n[:, l].reshape(nb_l, nh, M_DH), m0, nb_l, s_l)
        ya = _lat_attn(lat, zl, sink, l, ck, cv, cos, sin)
        x1, h2t, st = _merge(lat, xl, mod[l], g1, g2, hf, hb, zl, ya, *merge_w)
        xl = _peer(lat, h2t, *_route(st), u, vt, x1, mod[l], gfin, final)
    return (xc.reshape(nb_c, s_c, D_MODEL), xl.reshape(nb_l, s_l, D_MODEL),
            jnp.stack(ks, axis=1), jnp.stack(vs, axis=1), jnp.stack(cs, axis=1), jnp.stack(ns, axis=1),
            jnp.stack(ms, axis=1))


def kernel(x_prompt, x_sample, cache_k, cache_v, state_C, state_n, state_m, c, c_ctx, w_ada, b_ada, g_norm1, g_norm2, w_in, b_gates_m, g_mlstm, w_spatial, b_spatial, sink, w_branch, w_merge, b_merge, w_out, w_peer_q, peer_keys, peer_u, peer_v, g_final):
    return _forward(x_prompt, x_sample, cache_k, cache_v, state_C, state_n, state_m, c, c_ctx, w_ada, b_ada,
                    g_norm1, g_norm2, w_in, b_gates_m, g_mlstm, w_spatial, b_spatial, sink, w_branch, w_merge,
                    b_merge, w_out, w_peer_q, peer_keys, peer_u, peer_v, g_final)
```

```python
import functools

import numpy as np
import jax
import jax.numpy as jnp
from jax import lax
from jax.experimental import pallas as pl
from jax.experimental.pallas import tpu as pltpu

D_MODEL = 1024
DEPTH = 2
GRID_W = 64
EPS = 1e-6
NEG_INF = -1e30
MIX_W = D_MODEL // 2
M_HEADS = 4
M_DH = MIX_W // M_HEADS
CHUNK = 128
C_GROUPS = 4
A_HEADS = 8
A_KV = 2
A_GROUP = A_HEADS // A_KV
A_DH = MIX_W // A_HEADS
A_KVW = A_KV * A_DH
ROPE_BASE = 10000.0
ROPE_FREQS = A_DH // 4
N_KEYS = 128
N_EXPERTS = N_KEYS * N_KEYS
PEER_HEADS = 8
PEER_TOPK = 16
PEER_QW = 2 * PEER_HEADS * N_KEYS

LANES = 128
MXU_DTYPE = jnp.bfloat16
GATE_DTYPE = jnp.bfloat16
VMEM_LIMIT = 56 * 1024 * 1024

Z_QM, Z_KM, Z_VM, Z_OM, Z_UC, Z_VC, Z_QA = (i * MIX_W for i in range(7))
Z_KA = 7 * MIX_W
Z_VA = Z_KA + A_KVW
Z_GM = Z_VA + A_KVW
Z_W = Z_GM + LANES

f32 = jnp.float32


def _params(sem):
    return pltpu.CompilerParams(dimension_semantics=sem, vmem_limit_bytes=VMEM_LIMIT)


def _mm(a, b):
    return jnp.dot(a.astype(MXU_DTYPE), b.astype(MXU_DTYPE), preferred_element_type=f32)


def _mm_nt(a, b):
    return lax.dot_general(a.astype(MXU_DTYPE), b.astype(MXU_DTYPE), (((1,), (1,)), ((), ())),
                           preferred_element_type=f32)


def _split3(x):
    hi = x.astype(jnp.bfloat16)
    r1 = x - hi.astype(f32)
    mid = r1.astype(jnp.bfloat16)
    lo = (r1 - mid.astype(f32)).astype(jnp.bfloat16)
    return hi, mid, lo


def _mm_exact_lhs(a01, x):
    a = a01.astype(jnp.bfloat16)
    hi, mid, lo = _split3(x)
    return (jnp.dot(a, hi, preferred_element_type=f32) + jnp.dot(a, mid, preferred_element_type=f32)
            + jnp.dot(a, lo, preferred_element_type=f32))


def _mm3(a, b):
    ah, am, al = _split3(a)
    bh, bm, bl = _split3(b)
    d = functools.partial(jnp.dot, preferred_element_type=f32)
    return (d(ah, bh) + (d(ah, bm) + d(am, bh)) + (d(ah, bl) + d(al, bh) + d(am, bm)))


def _rms(x):
    return x * lax.rsqrt(jnp.mean(x * x, axis=-1, keepdims=True) + EPS)


def _mod_norm(x, g, scale, shift):
    return _rms(x) * g * (1.0 + scale) + shift


def _ada_kernel(c_ref, w_ref, b_ref, o_ref):
    c = c_ref[...]
    o_ref[0] = _mm3(c * jax.nn.sigmoid(c), w_ref[0]) + b_ref[0]


def _ada(cond, w_ada, b_ada):
    rows = cond.shape[0]
    tn = 1536
    return pl.pallas_call(
        _ada_kernel,
        grid=(DEPTH, 6 * D_MODEL // tn),
        in_specs=[pl.BlockSpec((rows, D_MODEL), lambda l, j: (0, 0)),
                  pl.BlockSpec((1, D_MODEL, tn), lambda l, j: (l, 0, j)),
                  pl.BlockSpec((1, 1, tn), lambda l, j: (l, 0, j))],
        out_specs=pl.BlockSpec((1, rows, tn), lambda l, j: (l, 0, j)),
        out_shape=jax.ShapeDtypeStruct((DEPTH, rows, 6 * D_MODEL), f32),
        compiler_params=_params(("parallel", "parallel")),
        name="ada",
    )(cond, w_ada, b_ada.reshape(DEPTH, 1, 6 * D_MODEL))


class _Group:
    def __init__(self, nb, s, cond0, per_sequence):
        self.nb, self.s, self.cond0, self.per_sequence = nb, s, cond0, per_sequence
        self.t = nb * s

    def mod_row(self, tm):
        assert self.t % tm == 0
        if not self.per_sequence:
            return lambda i: self.cond0
        assert self.s % tm == 0
        return lambda i: self.cond0 + i // (self.s // tm)


def _inproj_kernel(x_ref, mod_ref, g1_ref, w_ref, z_ref, *kv_refs):
    h = _mod_norm(x_ref[...], g1_ref[...], mod_ref[0, 1:2, :], mod_ref[0, 0:1, :])
    z = _mm(h, w_ref[...])
    z_ref[...] = z
    if kv_refs:
        kv_refs[0][...] = z[:, Z_KA:Z_KA + A_KVW]
        kv_refs[1][...] = z[:, Z_VA:Z_VA + A_KVW]


def _inproj(grp, x, mod_l, g1, w, emit_kv):
    tm = 256
    row = grp.mod_row(tm)
    kv_spec = [pl.BlockSpec((tm, A_KVW), lambda i: (i, 0))] * 2 if emit_kv else []
    kv_shape = [jax.ShapeDtypeStruct((grp.t, A_KVW), f32)] * 2 if emit_kv else []
    return pl.pallas_call(
        _inproj_kernel,
        grid=(grp.t // tm,),
        in_specs=[pl.BlockSpec((tm, D_MODEL), lambda i: (i, 0)),
                  pl.BlockSpec((1, 6, D_MODEL), lambda i: (row(i), 0, 0)),
                  pl.BlockSpec((1, D_MODEL), lambda i: (0, 0)),
                  pl.BlockSpec((D_MODEL, Z_W), lambda i: (0, 0), pipeline_mode=pl.Buffered(1))],
        out_specs=[pl.BlockSpec((tm, Z_W), lambda i: (i, 0))] + kv_spec,
        out_shape=[jax.ShapeDtypeStruct((grp.t, Z_W), f32)] + kv_shape,
        compiler_params=_params(("parallel",)),
        name="inproj",
    )(x, mod_l, g1, w)


def _log_sigmoid(x):
    return jnp.minimum(x, 0.0) - jnp.log1p(jnp.exp(-jnp.abs(x)))


def _col(a, j):
    lane = lax.broadcasted_iota(jnp.int32, a.shape, 1)
    return jnp.sum(jnp.where(lane == j, a, 0.0), axis=1, keepdims=True)


def _mlstm_kernel(qf_ref, kf_ref, vf_ref, gf_ref, qb_ref, kb_ref, vb_ref, gb_ref, bias_ref,
                  c0_ref, n0_ref, m0_ref, hf_ref, hb_ref, c_out, n_out, m_out, ct_s, n_s, m_s):
    c = pl.program_id(1)
    nseq = qf_ref.shape[0]
    nh = 2 * M_HEADS

    @pl.when(c == 0)
    def _():
        for p in range(nseq):
            for i in range(nh):
                ct_s[p * nh + i] = c0_ref[p, i].T
            n_s[p * nh:(p + 1) * nh, :] = n0_ref[p]
            m_s[p * nh:(p + 1) * nh, :] = m0_ref[p]

    ri = lax.broadcasted_iota(jnp.int32, (CHUNK, CHUNK), 0)
    ci = lax.broadcasted_iota(jnp.int32, (CHUNK, CHUNK), 1)
    lane = lax.broadcasted_iota(jnp.int32, (CHUNK, LANES), 1)
    is_forget = ((lane // M_HEADS) % 2) == 1

    refs = ((qf_ref, kf_ref, vf_ref, gf_ref, hf_ref), (qb_ref, kb_ref, vb_ref, gb_ref, hb_ref))
    keeps = (ri >= ci, ri <= ci)
    gates = {}
    for p in range(nseq):
        for d in range(2):
            g = refs[d][3][p] + bias_ref[...]
            lg = jnp.where(is_forget, _log_sigmoid(g), g)
            bc = _mm_exact_lhs(keeps[d], lg)
            gates[p, d] = (lg, bc, lg.T, bc.T)

    chains = [(p, d, hh) for p in range(nseq) for d in range(2) for hh in range(M_HEADS)]
    st = {}
    for ch in chains:
        p, d, hh = ch
        lg, bc, lgt, bct = gates[p, d]
        idx = p * nh + d * M_HEADS + hh
        j_li = 2 * d * M_HEADS + hh
        j_lf = j_li + M_HEADS
        sl = slice(hh * M_DH, (hh + 1) * M_DH)
        q = refs[d][0][p, :, sl]
        k = refs[d][1][p, :, sl] * (M_DH ** -0.5)
        v = refs[d][2][p, :, sl]
        b_col = _col(bc, j_lf)
        li_col = _col(lg, j_li)
        a_col = b_col + m_s[idx:idx + 1, 0:1]
        dm = jnp.where(keeps[d], b_col - bct[j_lf:j_lf + 1, :] + lgt[j_li:j_li + 1, :], -jnp.inf)
        m_t = jnp.maximum(a_col, jnp.max(dm, axis=1, keepdims=True))
        st[ch] = dict(idx=idx, sl=sl, q=q, k=k, v=v, kt=k.T, b_col=b_col, li_col=li_col, m_t=m_t,
                      w=jnp.exp(dm - m_t), w0=jnp.exp(a_col - m_t))
    for ch in chains:
        c_ = st[ch]
        c_["s"] = c_["w"] * _mm(c_["q"], c_["kt"])
        c_["ct"] = ct_s[c_["idx"]]
        c_["qc"] = _mm(c_["q"], c_["ct"])
    for ch in chains:
        p, d, hh = ch
        c_ = st[ch]
        n_row = n_s[c_["idx"]:c_["idx"] + 1, :]
        num = c_["w0"] * c_["qc"] + _mm(c_["s"], c_["v"])
        den = (c_["w0"] * jnp.sum(c_["q"] * n_row, axis=1, keepdims=True)
               + jnp.sum(c_["s"], axis=1, keepdims=True))
        refs[d][4][p, :, c_["sl"]] = num / jnp.maximum(jnp.abs(den), jnp.exp(-c_["m_t"]))
        c_["n_row"] = n_row
    for ch in chains:
        p, d, hh = ch
        c_ = st[ch]
        idx = c_["idx"]
        te = 0 if d == 1 else CHUNK - 1
        m_end = c_["m_t"][te:te + 1, :]
        w0_end = c_["w0"][te:te + 1, :]
        w_end = jnp.exp(c_["b_col"][te:te + 1, :] - c_["b_col"] + c_["li_col"] - m_end)
        ct_s[idx] = w0_end * c_["ct"] + _mm(c_["kt"], c_["v"] * w_end)
        n_s[idx:idx + 1, :] = w0_end * c_["n_row"] + jnp.sum(c_["k"] * w_end, axis=0, keepdims=True)
        m_s[idx:idx + 1, :] = jnp.broadcast_to(m_end, (1, LANES))

    @pl.when(c == pl.num_programs(1) - 1)
    def _():
        for p in range(nseq):
            for i in range(nh):
                c_out[p, i] = ct_s[p * nh + i].T
            n_out[p] = n_s[p * nh:(p + 1) * nh, :]
            m_out[p] = m_s[p * nh:(p + 1) * nh, :]


MLSTM_SEQS = 2


def _mlstm(z, bias_row, c0, n0, m0, nb, s):
    nc = s // CHUNK
    nh = 2 * M_HEADS
    ps = MLSTM_SEQS
    assert nb % ps == 0
    z3 = z.reshape(nb, s, Z_W)
    fwd = lambda col: (lambda b, c: (b, c, col))
    bwd = lambda col: (lambda b, c: (b, nc - 1 - c, col))
    wide = lambda im: pl.BlockSpec((ps, CHUNK, MIX_W), im)
    gate = lambda im: pl.BlockSpec((ps, CHUNK, LANES), im)
    state = lambda shape: pl.BlockSpec((ps,) + shape, lambda b, c: (b,) + (0,) * len(shape))
    gcol = Z_GM // LANES
    hf, hb, c_new, n_new, m_new = pl.pallas_call(
        _mlstm_kernel,
        grid=(nb // ps, nc),
        in_specs=[wide(fwd(0)), wide(fwd(1)), wide(fwd(2)), gate(fwd(gcol)),
                  wide(bwd(0)), wide(bwd(1)), wide(bwd(2)), gate(bwd(gcol)),
                  pl.BlockSpec((1, LANES), lambda b, c: (0, 0)),
                  state((nh, M_DH, M_DH)), state((nh, M_DH)), state((nh, LANES))],
        out_specs=[wide(fwd(0)), wide(bwd(0)),
                   state((nh, M_DH, M_DH)), state((nh, M_DH)), state((nh, LANES))],
        out_shape=[jax.ShapeDtypeStruct((nb, s, MIX_W), f32), jax.ShapeDtypeStruct((nb, s, MIX_W), f32),
                   jax.ShapeDtypeStruct((nb, nh, M_DH, M_DH), f32), jax.ShapeDtypeStruct((nb, nh, M_DH), f32),
                   jax.ShapeDtypeStruct((nb, nh, LANES), f32)],
        scratch_shapes=[pltpu.VMEM((ps * nh, M_DH, M_DH), f32), pltpu.VMEM((ps * nh, M_DH), f32),
                        pltpu.VMEM((ps * nh, LANES), f32)],
        compiler_params=_params(("parallel", "arbitrary")),
        name="mlstm",
    )(z3, z3, z3, z3, z3, z3, z3, z3, bias_row, c0, n0, m0)
    return hf.reshape(nb * s, MIX_W), hb.reshape(nb * s, MIX_W), c_new, n_new, m_new


def _half_placements(a):
    lane = lax.broadcasted_iota(jnp.int32, a.shape, 1)
    g0 = jnp.where(lane < A_DH, a, 0.0)
    g1 = jnp.where(lane >= A_DH, a, 0.0)
    return ((g0, pltpu.roll(g0, A_DH, 1)), (pltpu.roll(g1, A_DH, 1), g1))


def _group_attend(q, k_all, v_all, sink_ref, l, keep):
    kz = _half_placements(k_all)
    vz = _half_placements(v_all)
    slabs = []
    for slab in range(A_HEADS // 2):
        acc = None
        for pos in range(2):
            head = 2 * slab + pos
            g = head // A_GROUP
            s = _mm_nt(q[:, slab * LANES:(slab + 1) * LANES], kz[g][pos]) * (A_DH ** -0.5)
            if keep is not None:
                s = jnp.where(keep, s, NEG_INF)
            sk = sink_ref[l, head]
            m = jnp.maximum(jnp.max(s, axis=1, keepdims=True), sk)
            p = jnp.exp(s - m)
            den = jnp.sum(p, axis=1, keepdims=True) + jnp.exp(sk - m)
            o = _mm(p, vz[g][pos]) / den
            acc = o if acc is None else acc + o
        slabs.append(acc)
    return jnp.concatenate(slabs, axis=1)


def _ctx_attn_kernel(l, sink_ref, q_ref, k_ref, v_ref, o_ref):
    o_ref[...] = _group_attend(q_ref[...], k_ref[...], v_ref[...], sink_ref, l, None)


def _ctx_attn(grp, z, sink, l):
    s = grp.s
    return pl.pallas_call(
        functools.partial(_ctx_attn_kernel, l),
        grid=(grp.nb,),
        in_specs=[pl.BlockSpec(memory_space=pltpu.SMEM),
                  pl.BlockSpec((s, MIX_W), lambda b: (b, Z_QA // MIX_W)),
                  pl.BlockSpec((s, A_KVW), lambda b: (b, Z_KA // A_KVW)),
                  pl.BlockSpec((s, A_KVW), lambda b: (b, Z_VA // A_KVW))],
        out_specs=pl.BlockSpec((s, MIX_W), lambda b: (b, 0)),
        out_shape=jax.ShapeDtypeStruct((grp.t, MIX_W), f32),
        compiler_params=_params(("parallel",)),
        name="ctx_attn",
    )(sink, z, z, z)


def _rope(x, cos, sin):
    lane = lax.broadcasted_iota(jnp.int32, cos.shape, 1)
    first = (lane % (2 * ROPE_FREQS)) < ROPE_FREQS
    out = []
    for j in range(x.shape[1] // LANES):
        xs = x[:, j * LANES:(j + 1) * LANES]
        partner = jnp.where(first, pltpu.roll(xs, LANES - ROPE_FREQS, 1), pltpu.roll(xs, ROPE_FREQS, 1))
        out.append(xs * cos + partner * sin)
    return out[0] if len(out) == 1 else jnp.concatenate(out, axis=1)


def _lat_attn_kernel(l, nblk, sink_ref, q_ref, kp_ref, kc_ref, kn_ref, vp_ref, vc_ref, vn_ref,
                     cq_ref, sq_ref, cp_ref, sp_ref, cn_ref, sn_ref, ck_ref, cv_ref, o_ref):
    i = pl.program_id(1)
    q = _rope(q_ref[...], cq_ref[...], sq_ref[...])
    k_all = jnp.concatenate([ck_ref[0, 0],
                             _rope(kp_ref[...], cp_ref[...], sp_ref[...]),
                             _rope(kc_ref[...], cq_ref[...], sq_ref[...]),
                             _rope(kn_ref[...], cn_ref[...], sn_ref[...])], axis=0)
    v_all = jnp.concatenate([cv_ref[0, 0], vp_ref[...], vc_ref[...], vn_ref[...]], axis=0)
    lc = ck_ref.shape[2]
    nk = lc + 3 * CHUNK
    r = lax.broadcasted_iota(jnp.int32, (CHUNK, nk), 0)
    cc = lax.broadcasted_iota(jnp.int32, (CHUNK, nk), 1) - lc
    lo = jnp.maximum(r, jnp.where(i == 0, CHUNK, 0))
    hi = jnp.minimum(r + 2 * CHUNK, jnp.where(i == nblk - 1, 2 * CHUNK - 1, 3 * CHUNK))
    keep = (cc < 0) | ((cc >= lo) & (cc <= hi))
    o_ref[...] = _group_attend(q, k_all, v_all, sink_ref, l, keep)


def _lat_attn(grp, z, sink, l, cache_k, cache_v, cos, sin):
    s, nb = grp.s, grp.nb
    nblk = s // CHUNK
    lc = cache_k.shape[2]

    def rows(off):
        return lambda b, i: b * nblk + jnp.clip(i + off, 0, nblk - 1)

    def zspec(width, col, off):
        rf = rows(off)
        return pl.BlockSpec((CHUNK, width), lambda b, i: (rf(b, i), col))

    def tab(off):
        return pl.BlockSpec((CHUNK, LANES), lambda b, i: (jnp.clip(i + off, 0, nblk - 1), 0))

    cache = pl.BlockSpec((1, 1, lc, A_KVW), lambda b, i: (b, l, 0, 0))
    kcol, vcol = Z_KA // A_KVW, Z_VA // A_KVW
    return pl.pallas_call(
        functools.partial(_lat_attn_kernel, l, nblk),
        grid=(nb, nblk),
        in_specs=[pl.BlockSpec(memory_space=pltpu.SMEM),
                  zspec(MIX_W, Z_QA // MIX_W, 0),
                  zspec(A_KVW, kcol, -1), zspec(A_KVW, kcol, 0), zspec(A_KVW, kcol, 1),
                  zspec(A_KVW, vcol, -1), zspec(A_KVW, vcol, 0), zspec(A_KVW, vcol, 1),
                  tab(0), tab(0), tab(-1), tab(-1), tab(1), tab(1), cache, cache],
        out_specs=pl.BlockSpec((CHUNK, MIX_W), lambda b, i: (b * nblk + i, 0)),
        out_shape=jax.ShapeDtypeStruct((grp.t, MIX_W), f32),
        compiler_params=_params(("parallel", "parallel")),
        name="lat_attn",
    )(sink, z, z, z, z, z, z, z, cos, sin, cos, sin, cos, sin, cache_k, cache_v)


def _rope_tables(s):
    t = jnp.arange(s)
    freqs = ROPE_BASE ** (-jnp.arange(ROPE_FREQS, dtype=f32) / ROPE_FREQS)
    a_row = (t // GRID_W).astype(f32)[:, None] * freqs[None, :]
    a_col = (t % GRID_W).astype(f32)[:, None] * freqs[None, :]
    cos = jnp.concatenate([jnp.cos(a_row)] * 2 + [jnp.cos(a_col)] * 2, axis=1)
    sin = jnp.concatenate([-jnp.sin(a_row), jnp.sin(a_row), -jnp.sin(a_col), jnp.sin(a_col)], axis=1)
    return jnp.concatenate([cos] * A_KV, axis=1), jnp.concatenate([sin] * A_KV, axis=1)


def _merge_kernel(x_ref, mod_ref, g1_ref, g2_ref, hf_ref, hb_ref, om_ref, uc_ref, vc_ref, ya_ref,
                  gm_ref, ws_ref, bs_ref, wmerge_ref, bmerge_ref, wbr_ref, wout_ref, wq_ref, keys_ref,
                  x1_ref, h2t_ref, st_ref):
    x = x_ref[...]
    tm = x.shape[0]
    mod = mod_ref[0]
    h = _mod_norm(x, g1_ref[...], mod[1:2], mod[0:1])
    hm = hf_ref[...] + hb_ref[...]
    ym = jnp.concatenate(
        [_rms(hm[:, hh * M_DH:(hh + 1) * M_DH]) * gm_ref[:, hh * M_DH:(hh + 1) * M_DH] for hh in range(M_HEADS)],
        axis=1) * jax.nn.sigmoid(om_ref[...])
    vr = _rms(vc_ref[...])
    zc = []
    for n in range(tm // CHUNK):
        rs = slice(n * CHUNK, (n + 1) * CHUNK)
        zc.append(jnp.concatenate(
            [_mm(ws_ref[g], vr[rs, g * LANES:(g + 1) * LANES]) + _col(bs_ref[...], g) for g in range(C_GROUPS)],
            axis=1))
    yc = uc_ref[...] * jnp.concatenate(zc, axis=0)
    mixed = jnp.zeros((tm, D_MODEL), f32)
    for n, y in enumerate((ym, yc, ya_ref[...])):
        gate = jax.nn.sigmoid(_mm(h, wmerge_ref[:, n * D_MODEL:(n + 1) * D_MODEL])
                              + bmerge_ref[:, n * D_MODEL:(n + 1) * D_MODEL])
        mixed = mixed + gate * _mm(y, wbr_ref[n])
    x1 = x + mod[2:3] * _mm(mixed, wout_ref[...])
    x1_ref[...] = x1
    h2 = _mod_norm(x1, g2_ref[...], mod[4:5], mod[3:4])
    h2t_ref[...] = h2.T.astype(h2t_ref.dtype)
    qp = _mm(h2, wq_ref[...])
    for hp in range(2 * PEER_HEADS):
        st_ref[hp] = _mm_nt(keys_ref[hp], qp[:, hp * N_KEYS:(hp + 1) * N_KEYS])


def _merge(grp, x, mod_l, g1, g2, hf, hb, z, ya, gm, ws, bs_t, wmerge, bmerge, wbr, wout, wq, keys):
    tm = 256
    row = grp.mod_row(tm)
    tok = lambda w, col=0: pl.BlockSpec((tm, w), lambda i: (i, col))
    full = lambda a: pl.BlockSpec(a.shape, lambda i: (0,) * a.ndim, pipeline_mode=pl.Buffered(1))
    return pl.pallas_call(
        _merge_kernel,
        grid=(grp.t // tm,),
        in_specs=[tok(D_MODEL), pl.BlockSpec((1, 6, D_MODEL), lambda i: (row(i), 0, 0)), full(g1), full(g2),
                  tok(MIX_W), tok(MIX_W), tok(MIX_W, Z_OM // MIX_W), tok(MIX_W, Z_UC // MIX_W),
                  tok(MIX_W, Z_VC // MIX_W), tok(MIX_W),
                  full(gm), full(ws), full(bs_t), full(wmerge), full(bmerge), full(wbr), full(wout), full(wq),
                  full(keys)],
        out_specs=[tok(D_MODEL), pl.BlockSpec((D_MODEL, tm), lambda i: (0, i)),
                   pl.BlockSpec((2 * PEER_HEADS, N_KEYS, tm), lambda i: (0, 0, i))],
        out_shape=[jax.ShapeDtypeStruct((grp.t, D_MODEL), f32), jax.ShapeDtypeStruct((D_MODEL, grp.t), MXU_DTYPE),
                   jax.ShapeDtypeStruct((2 * PEER_HEADS, N_KEYS, grp.t), f32)],
        compiler_params=_params(("parallel",)),
        name="merge",
    )(x, mod_l, g1, g2, hf, hb, z, z, z, ya, gm, ws, bs_t, wmerge, bmerge, wbr, wout, wq, keys)


_CELLS = [(i, j) for i in range(PEER_TOPK) for j in range(PEER_TOPK) if (i + 1) * (j + 1) <= PEER_TOPK]


def _route_kernel(st_ref, r2_ref, b_ref, jd_ref, a_ref, val_s, rank_s, work_s):
    tb = st_ref.shape[2]
    nhp = 2 * PEER_HEADS
    unranked = jnp.full((N_KEYS, tb), float(PEER_TOPK), f32)

    def start():
        work_s[...] = st_ref[...]
        for hp in range(nhp):
            rank_s[hp] = unranked

    def extract(hp, r, rf, hit_of):
        s = work_s[hp]
        m = jnp.max(s, axis=0, keepdims=True)
        hit = hit_of(s, m)
        val_s[hp % 2, r, pl.ds(hp // 2, 1), :] = m
        rank_s[hp] = jnp.where(hit, rf, rank_s[hp])
        work_s[hp] = jnp.where(hit, -jnp.inf, s)

    start()

    def fast_round(r, _):
        rf = lax.convert_element_type(r, f32)
        for hp in range(nhp):
            extract(hp, r, rf, lambda s, m: s == m)
        return 0

    lax.fori_loop(0, PEER_TOPK, fast_round, 0)
    removed = jnp.zeros((1, tb), f32)
    for hp in range(nhp):
        removed = jnp.maximum(removed, jnp.sum(jnp.where(work_s[hp] == -jnp.inf, 1.0, 0.0), axis=0, keepdims=True))

    @pl.when(jnp.max(removed) > float(PEER_TOPK))
    def _():
        key = lax.broadcasted_iota(jnp.int32, (N_KEYS, tb), 0).astype(f32)

        def lowest_index_hit(s, m):
            return key == jnp.min(jnp.where(s == m, key, float(N_KEYS)), axis=0, keepdims=True)

        start()

        def exact_round(i, _):
            extract(i // PEER_TOPK, i % PEER_TOPK, lax.convert_element_type(i % PEER_TOPK, f32), lowest_index_hit)
            return 0

        lax.fori_loop(0, nhp * PEER_TOPK, exact_round, 0)

    v1 = [val_s[0, i] for i in range(PEER_TOPK)]
    v2 = [val_s[1, i] for i in range(PEER_TOPK)]
    cand = [v1[i] + v2[j] for (i, j) in _CELLS]
    ncell = len(_CELLS)
    before = [jnp.zeros((PEER_HEADS, tb), f32) for _ in range(ncell)]
    for x in range(ncell):
        for y in range(x + 1, ncell):
            x_first = (cand[x] >= cand[y]).astype(f32)
            before[y] = before[y] + x_first
            before[x] = before[x] + (1.0 - x_first)
    ea = [jnp.exp(v1[i] - v1[0]) for i in range(PEER_TOPK)]
    eb = [jnp.exp(v2[j] - v2[0]) for j in range(PEER_TOPK)]
    jcount = [jnp.zeros((PEER_HEADS, tb), f32) for _ in range(PEER_TOPK)]
    zsum = jnp.zeros((PEER_HEADS, tb), f32)
    for x, (i, j) in enumerate(_CELLS):
        sel = (before[x] < PEER_TOPK).astype(f32)
        jcount[i] = jcount[i] + sel
        zsum = zsum + sel * (ea[i] * eb[j])
    inv_z = 1.0 / zsum

    for h in range(PEER_HEADS):
        s1 = st_ref[2 * h]
        s2 = st_ref[2 * h + 1]
        rank1 = rank_s[2 * h]
        jd = jnp.zeros((N_KEYS, tb), f32)
        for i in range(PEER_TOPK):
            jd = jnp.where(rank1 == float(i), jcount[i][h:h + 1, :], jd)
        jd_ref[h] = _pair_words(jd)
        a_ref[h] = _pair_words(jnp.exp(s1 - v1[0][h:h + 1, :]) * inv_z[h:h + 1, :])
        b_ref[h] = jnp.exp(s2 - v2[0][h:h + 1, :]).astype(b_ref.dtype)
        r2_ref[h] = rank_s[2 * h + 1].astype(r2_ref.dtype)


def _route(st):
    t = st.shape[2]
    tb = LANES
    out = [jax.ShapeDtypeStruct((PEER_HEADS, N_KEYS, t), dt)
           for dt in (GATE_DTYPE, GATE_DTYPE, jnp.uint32, jnp.uint32)]
    spec = pl.BlockSpec((PEER_HEADS, N_KEYS, tb), lambda i: (0, 0, i))
    return pl.pallas_call(
        _route_kernel,
        grid=(t // tb,),
        in_specs=[pl.BlockSpec((2 * PEER_HEADS, N_KEYS, tb), lambda i: (0, 0, i))],
        out_specs=[spec] * 4,
        out_shape=out,
        scratch_shapes=[pltpu.VMEM((2, PEER_TOPK, PEER_HEADS, tb), f32),
                        pltpu.VMEM((2 * PEER_HEADS, N_KEYS, tb), f32),
                        pltpu.VMEM((2 * PEER_HEADS, N_KEYS, tb), f32)],
        compiler_params=_params(("parallel",)),
        name="peer_route",
    )(st)


PEER_TT = 512
PEER_E1 = 16
PEER_SUBS = (4, 4, 4, 4)
PEER_ROWS = 16


def _pair_words(x):
    u = lax.bitcast_convert_type(x.astype(jnp.bfloat16).astype(f32), jnp.uint32)
    return u | (u >> 16)


def _bcast_pair_row(ref, hh, e1):
    assert PEER_ROWS == 16
    row = ref[hh, e1:e1 + 1, :]
    return pltpu.bitcast(jnp.broadcast_to(row, (PEER_ROWS // 2, row.shape[1])), jnp.bfloat16)


def _peer_kernel(final, h2t_ref, r2_ref, b_ref, jd_ref, a_ref, u_ref, vt_ref, x1_ref, mod_ref, gf_ref,
                 o_ref, acc_s, a_s, h_s):
    j = pl.program_id(1)

    @pl.when(j == 0)
    def _():
        acc_s[...] = jnp.zeros_like(acc_s)

    assert sum(PEER_SUBS) == PEER_E1
    starts = [sum(PEER_SUBS[:sb]) for sb in range(len(PEER_SUBS))]
    n_sub = len(PEER_SUBS)
    srow = lambda sb: slice(starts[sb] * N_KEYS, (starts[sb] + PEER_SUBS[sb]) * N_KEYS)
    def hdot(sb):
        h_s[sb % 2, :PEER_SUBS[sb] * N_KEYS, :] = jnp.dot(u_ref[srow(sb), :], h2t_ref[...],
                                                         preferred_element_type=f32).astype(h_s.dtype)

    def accumulate(sb):
        acc_s[...] += jnp.dot(vt_ref[:, srow(sb)], a_s[srow(sb), :], preferred_element_type=f32)

    hdot(0)
    for sb in range(n_sub):
        h = h_s.at[sb % 2]
        if sb + 1 < n_sub:
            hdot(sb + 1)
        if sb >= 1:
            accumulate(sb - 1)
        for e in range(PEER_SUBS[sb]):
            e1 = starts[sb] + e
            n_chunk = N_KEYS // PEER_ROWS
            gates = [None] * n_chunk
            for hh in range(PEER_HEADS):
                jd = _bcast_pair_row(jd_ref, hh, e1)
                aa = _bcast_pair_row(a_ref, hh, e1)
                for r in range(n_chunk):
                    rows = slice(r * PEER_ROWS, (r + 1) * PEER_ROWS)
                    term = jnp.where(r2_ref[hh, rows, :] < jd, b_ref[hh, rows, :], jnp.zeros((), GATE_DTYPE)) * aa
                    gates[r] = term if gates[r] is None else gates[r] + term
            for r in range(n_chunk):
                lo = e * N_KEYS + r * PEER_ROWS
                act = gates[r] * jax.nn.gelu(h[lo:lo + PEER_ROWS, :])
                a_s[e1 * N_KEYS + r * PEER_ROWS:e1 * N_KEYS + (r + 1) * PEER_ROWS, :] = act.astype(a_s.dtype)
    accumulate(n_sub - 1)

    @pl.when(j == pl.num_programs(1) - 1)
    def _():
        x2 = x1_ref[...] + mod_ref[0, 5:6, :] * acc_s[...].T
        o_ref[...] = _rms(x2) * gf_ref[...] if final else x2


def _peer(grp, h2t, r2, b, jd, a, u, vt, x1, mod_l, g_final, final):
    tt = PEER_TT
    eb = PEER_E1 * N_KEYS
    row = grp.mod_row(tt)
    gate = pl.BlockSpec((PEER_HEADS, N_KEYS, tt), lambda i, j: (0, 0, i))
    gate1 = pl.BlockSpec((PEER_HEADS, PEER_E1, tt), lambda i, j: (0, j, i))
    return pl.pallas_call(
        functools.partial(_peer_kernel, final),
        grid=(grp.t // tt, N_EXPERTS // eb),
        in_specs=[pl.BlockSpec((D_MODEL, tt), lambda i, j: (0, i)), gate, gate, gate1, gate1,
                  pl.BlockSpec((eb, D_MODEL), lambda i, j: (j, 0)),
                  pl.BlockSpec((D_MODEL, eb), lambda i, j: (0, j)),
                  pl.BlockSpec((tt, D_MODEL), lambda i, j: (i, 0)),
                  pl.BlockSpec((1, 6, D_MODEL), lambda i, j: (row(i), 0, 0)),
                  pl.BlockSpec((1, D_MODEL), lambda i, j: (0, 0))],
        out_specs=pl.BlockSpec((tt, D_MODEL), lambda i, j: (i, 0)),
        out_shape=jax.ShapeDtypeStruct((grp.t, D_MODEL), f32),
        scratch_shapes=[pltpu.VMEM((D_MODEL, tt), f32), pltpu.VMEM((eb, tt), MXU_DTYPE),
                        pltpu.VMEM((2, max(PEER_SUBS) * N_KEYS, tt), GATE_DTYPE)],
        compiler_params=_params(("parallel", "arbitrary")),
        name="peer",
    )(h2t, r2, b, jd, a, u, vt, x1, mod_l, g_final)


def _reorder_w_in(w_in_l):
    ng = 4 * M_HEADS
    a = w_in_l[:, :4 * MIX_W]
    g = w_in_l[:, 4 * MIX_W:4 * MIX_W + ng]
    rest = w_in_l[:, 4 * MIX_W + ng:]
    pad = jnp.zeros((D_MODEL, LANES - ng), w_in_l.dtype)
    return jnp.concatenate([a, rest, g, pad], axis=1).astype(MXU_DTYPE)


def _forward(x_prompt, x_sample, cache_k, cache_v, state_C, state_n, state_m, c, c_ctx,
             w_ada, b_ada, g_norm1, g_norm2, w_in, b_gates_m, g_mlstm, w_spatial, b_spatial, sink,
             w_branch, w_merge, b_merge, w_out, w_peer_q, peer_keys, peer_u, peer_v, g_final):
    nb_c, s_c, _ = x_prompt.shape
    nb_l, s_l, _ = x_sample.shape
    ctx = _Group(nb_c, s_c, 0, False)
    lat = _Group(nb_l, s_l, 1, True)
    lc = cache_k.shape[2]
    nh = 2 * M_HEADS

    cond = jnp.concatenate([c_ctx[None, :], c], axis=0)
    cond = jnp.pad(cond, ((0, (-cond.shape[0]) % 8), (0, 0)))
    mod = _ada(cond, w_ada, b_ada).reshape(DEPTH, cond.shape[0], 6, D_MODEL)
    xc = x_prompt.reshape(ctx.t, D_MODEL)
    xl = x_sample.reshape(lat.t, D_MODEL)
    cos, sin = _rope_tables(s_l)
    ck = cache_k.reshape(nb_l, DEPTH, lc, A_KVW)
    cv = cache_v.reshape(nb_l, DEPTH, lc, A_KVW)
    zero_c = jnp.zeros((nb_c, nh, M_DH, M_DH), f32)
    zero_n = jnp.zeros((nb_c, nh, M_DH), f32)
    zero_m = jnp.zeros((nb_c, nh, LANES), f32)
    gfin = g_final.reshape(1, D_MODEL)

    ks, vs, cs, ns, ms = [], [], [], [], []
    for l in range(DEPTH):
        g1 = g_norm1[l].reshape(1, D_MODEL)
        g2 = g_norm2[l].reshape(1, D_MODEL)
        w_in_l = _reorder_w_in(w_in[l])
        bias_row = jnp.pad(b_gates_m[l].reshape(1, 4 * M_HEADS), ((0, 0), (0, LANES - 4 * M_HEADS)))
        merge_w = (g_mlstm[l].reshape(1, MIX_W), w_spatial[l].astype(MXU_DTYPE),
                   jnp.pad(b_spatial[l].T, ((0, 0), (0, LANES - C_GROUPS))), w_merge[l].astype(MXU_DTYPE),
                   b_merge[l].reshape(1, 3 * D_MODEL), w_branch[l].astype(MXU_DTYPE), w_out[l].astype(MXU_DTYPE),
                   w_peer_q[l].astype(MXU_DTYPE),
                   peer_keys[l].reshape(2 * PEER_HEADS, N_KEYS, N_KEYS).astype(MXU_DTYPE))
        u = peer_u[l].astype(MXU_DTYPE)
        vt = peer_v[l].T.astype(MXU_DTYPE)
        final = l == DEPTH - 1

        zc, k_new, v_new = _inproj(ctx, xc, mod[l], g1, w_in_l, True)
        hf, hb, c_new, n_new, m_new = _mlstm(zc, bias_row, zero_c, zero_n, zero_m, nb_c, s_c)
        ya = _ctx_attn(ctx, zc, sink, l)
        x1, h2t, st = _merge(ctx, xc, mod[l], g1, g2, hf, hb, zc, ya, *merge_w)
        xc = _peer(ctx, h2t, *_route(st), u, vt, x1, mod[l], gfin, final)
        ks.append(k_new.reshape(nb_c, s_c, A_KV, A_DH))
        vs.append(v_new.reshape(nb_c, s_c, A_KV, A_DH))
        cs.append(c_new.reshape(nb_c, 2, M_HEADS, M_DH, M_DH))
        ns.append(n_new.reshape(nb_c, 2, M_HEADS, M_DH))
        ms.append(m_new[:, :, 0].reshape(nb_c, 2, M_HEADS))

        zl = _inproj(lat, xl, mod[l], g1, w_in_l, False)[0]
        m0 = jnp.broadcast_to(state_m[:, l].reshape(nb_l, nh, 1), (nb_l, nh, LANES))
        hf, hb, _, _, _ = _mlstm(zl, bias_row, state_C[:, l].reshape(nb_l, nh, M_DH, M_DH),
                                 state_n[:, l].reshape(nb_l, nh, M_DH), m0, nb_l, s_l)
        ya = _lat_attn(lat, zl, sink, l, ck, cv, cos, sin)
        x1, h2t, st = _merge(lat, xl, mod[l], g1, g2, hf, hb, zl, ya, *merge_w)
        xl = _peer(lat, h2t, *_route(st), u, vt, x1, mod[l], gfin, final)
    return (xc.reshape(nb_c, s_c, D_MODEL), xl.reshape(nb_l, s_l, D_MODEL),
            jnp.stack(ks, axis=1), jnp.stack(vs, axis=1), jnp.stack(cs, axis=1), jnp.stack(ns, axis=1),
            jnp.stack(ms, axis=1))


def kernel(x_prompt, x_sample, cache_k, cache_v, state_C, state_n, state_m, c, c_ctx, w_ada, b_ada, g_norm1, g_norm2, w_in, b_gates_m, g_mlstm, w_spatial, b_spatial, sink, w_branch, w_merge, b_merge, w_out, w_peer_q, peer_keys, peer_u, peer_v, g_final):
    return _forward(x_prompt, x_sample, cache_k, cache_v, state_C, state_n, state_m, c, c_ctx, w_ada, b_ada,
                    g_norm1, g_norm2, w_in, b_gates_m, g_mlstm, w_spatial, b_spatial, sink, w_branch, w_merge,
                    b_merge, w_out, w_peer_q, peer_keys, peer_u, peer_v, g_final)
```

```python
import functools

import numpy as np
import jax
import jax.numpy as jnp
from jax import lax
from jax.experimental import pallas as pl
from jax.experimental.pallas import tpu as pltpu

D_MODEL = 1024
DEPTH = 2
GRID_W = 64
EPS = 1e-6
NEG_INF = -1e30
MIX_W = D_MODEL // 2
M_HEADS = 4
M_DH = MIX_W // M_HEADS
CHUNK = 128
C_GROUPS = 4
A_HEADS = 8
A_KV = 2
A_GROUP = A_HEADS // A_KV
A_DH = MIX_W // A_HEADS
A_KVW = A_KV * A_DH
ROPE_BASE = 10000.0
ROPE_FREQS = A_DH // 4
N_KEYS = 128
N_EXPERTS = N_KEYS * N_KEYS
PEER_HEADS = 8
PEER_TOPK = 16
PEER_QW = 2 * PEER_HEADS * N_KEYS

LANES = 128
MXU_DTYPE = jnp.bfloat16
GATE_DTYPE = jnp.bfloat16
VMEM_LIMIT = 56 * 1024 * 1024

Z_QM, Z_KM, Z_VM, Z_OM, Z_UC, Z_VC, Z_QA = (i * MIX_W for i in range(7))
Z_KA = 7 * MIX_W
Z_VA = Z_KA + A_KVW
Z_GM = Z_VA + A_KVW
Z_W = Z_GM + LANES

f32 = jnp.float32


def _params(sem):
    return pltpu.CompilerParams(dimension_semantics=sem, vmem_limit_bytes=VMEM_LIMIT)


def _mm(a, b):
    return jnp.dot(a.astype(MXU_DTYPE), b.astype(MXU_DTYPE), preferred_element_type=f32)


def _mm_nt(a, b):
    return lax.dot_general(a.astype(MXU_DTYPE), b.astype(MXU_DTYPE), (((1,), (1,)), ((), ())),
                           preferred_element_type=f32)


def _split3(x):
    hi = x.astype(jnp.bfloat16)
    r1 = x - hi.astype(f32)
    mid = r1.astype(jnp.bfloat16)
    lo = (r1 - mid.astype(f32)).astype(jnp.bfloat16)
    return hi, mid, lo


def _mm_exact_lhs(a01, x):
    a = a01.astype(jnp.bfloat16)
    hi, mid, lo = _split3(x)
    return (jnp.dot(a, hi, preferred_element_type=f32) + jnp.dot(a, mid, preferred_element_type=f32)
            + jnp.dot(a, lo, preferred_element_type=f32))


def _mm_exact_rhs(x, b01):
    b = b01.astype(jnp.bfloat16)
    hi, mid, lo = _split3(x)
    return (jnp.dot(hi, b, preferred_element_type=f32) + jnp.dot(mid, b, preferred_element_type=f32)
            + jnp.dot(lo, b, preferred_element_type=f32))


def _mm3(a, b):
    ah, am, al = _split3(a)
    bh, bm, bl = _split3(b)
    d = functools.partial(jnp.dot, preferred_element_type=f32)
    return (d(ah, bh) + (d(ah, bm) + d(am, bh)) + (d(ah, bl) + d(al, bh) + d(am, bm)))


def _rms(x):
    return x * lax.rsqrt(jnp.mean(x * x, axis=-1, keepdims=True) + EPS)


def _mod_norm(x, g, scale, shift):
    return _rms(x) * g * (1.0 + scale) + shift


def _ada_kernel(c_ref, w_ref, b_ref, o_ref):
    c = c_ref[...]
    o_ref[0] = _mm3(c * jax.nn.sigmoid(c), w_ref[0]) + b_ref[0]


def _ada(cond, w_ada, b_ada):
    rows = cond.shape[0]
    tn = 1536
    return pl.pallas_call(
        _ada_kernel,
        grid=(DEPTH, 6 * D_MODEL // tn),
        in_specs=[pl.BlockSpec((rows, D_MODEL), lambda l, j: (0, 0)),
                  pl.BlockSpec((1, D_MODEL, tn), lambda l, j: (l, 0, j)),
                  pl.BlockSpec((1, 1, tn), lambda l, j: (l, 0, j))],
        out_specs=pl.BlockSpec((1, rows, tn), lambda l, j: (l, 0, j)),
        out_shape=jax.ShapeDtypeStruct((DEPTH, rows, 6 * D_MODEL), f32),
        compiler_params=_params(("parallel", "parallel")),
        name="ada",
    )(cond, w_ada, b_ada.reshape(DEPTH, 1, 6 * D_MODEL))


class _Group:
    def __init__(self, nb, s, cond0, per_sequence):
        self.nb, self.s, self.cond0, self.per_sequence = nb, s, cond0, per_sequence
        self.t = nb * s

    def mod_row(self, tm):
        assert self.t % tm == 0
        if not self.per_sequence:
            return lambda i: self.cond0
        assert self.s % tm == 0
        return lambda i: self.cond0 + i // (self.s // tm)


def _inproj_kernel(x_ref, mod_ref, g1_ref, w_ref, z_ref, *kv_refs):
    h = _mod_norm(x_ref[...], g1_ref[...], mod_ref[0, 1:2, :], mod_ref[0, 0:1, :])
    z = _mm(h, w_ref[...])
    z_ref[...] = z
    if kv_refs:
        kv_refs[0][...] = z[:, Z_KA:Z_KA + A_KVW]
        kv_refs[1][...] = z[:, Z_VA:Z_VA + A_KVW]


def _inproj(grp, x, mod_l, g1, w, emit_kv):
    tm = 256
    row = grp.mod_row(tm)
    kv_spec = [pl.BlockSpec((tm, A_KVW), lambda i: (i, 0))] * 2 if emit_kv else []
    kv_shape = [jax.ShapeDtypeStruct((grp.t, A_KVW), f32)] * 2 if emit_kv else []
    return pl.pallas_call(
        _inproj_kernel,
        grid=(grp.t // tm,),
        in_specs=[pl.BlockSpec((tm, D_MODEL), lambda i: (i, 0)),
                  pl.BlockSpec((1, 6, D_MODEL), lambda i: (row(i), 0, 0)),
                  pl.BlockSpec((1, D_MODEL), lambda i: (0, 0)),
                  pl.BlockSpec((D_MODEL, Z_W), lambda i: (0, 0), pipeline_mode=pl.Buffered(1))],
        out_specs=[pl.BlockSpec((tm, Z_W), lambda i: (i, 0))] + kv_spec,
        out_shape=[jax.ShapeDtypeStruct((grp.t, Z_W), f32)] + kv_shape,
        compiler_params=_params(("parallel",)),
        name="inproj",
    )(x, mod_l, g1, w)


def _log_sigmoid(x):
    return jnp.minimum(x, 0.0) - jnp.log1p(jnp.exp(-jnp.abs(x)))


def _col(a, j):
    lane = lax.broadcasted_iota(jnp.int32, a.shape, 1)
    return jnp.sum(jnp.where(lane == j, a, 0.0), axis=1, keepdims=True)


def _mlstm_kernel(qf_ref, kf_ref, vf_ref, gf_ref, qb_ref, kb_ref, vb_ref, gb_ref, bias_ref,
                  c0_ref, n0_ref, m0_ref, hf_ref, hb_ref, c_out, n_out, m_out, ct_s, n_s, m_s):
    c = pl.program_id(1)
    nseq = qf_ref.shape[0]
    nh = 2 * M_HEADS

    @pl.when(c == 0)
    def _():
        for p in range(nseq):
            for i in range(nh):
                ct_s[p * nh + i] = c0_ref[p, i].T
            n_s[p * nh:(p + 1) * nh, :] = n0_ref[p]
            m_s[p * nh:(p + 1) * nh, :] = m0_ref[p]

    ri = lax.broadcasted_iota(jnp.int32, (CHUNK, CHUNK), 0)
    ci = lax.broadcasted_iota(jnp.int32, (CHUNK, CHUNK), 1)
    lane = lax.broadcasted_iota(jnp.int32, (CHUNK, LANES), 1)
    is_forget = ((lane // M_HEADS) % 2) == 1

    refs = ((qf_ref, kf_ref, vf_ref, gf_ref, hf_ref), (qb_ref, kb_ref, vb_ref, gb_ref, hb_ref))
    keeps = (ri >= ci, ri <= ci)
    krow = lax.broadcasted_iota(jnp.int32, (LANES, M_HEADS * LANES), 0)
    head = lax.broadcasted_iota(jnp.int32, (LANES, M_HEADS * LANES), 1) // LANES
    gates = {}
    for p in range(nseq):
        for d in range(2):
            g = refs[d][3][p] + bias_ref[...]
            lg = jnp.where(is_forget, _log_sigmoid(g), g)
            bc = _mm_exact_lhs(keeps[d], lg)
            li_tiles = _mm_exact_rhs(lg, krow == 2 * d * M_HEADS + head)
            lf_tiles = _mm_exact_rhs(bc, krow == (2 * d + 1) * M_HEADS + head)
            gates[p, d] = (lg.T, bc.T, li_tiles, lf_tiles)

    chains = [(p, d, hh) for p in range(nseq) for d in range(2) for hh in range(M_HEADS)]
    st = {}
    for ch in chains:
        p, d, hh = ch
        lgt, bct, li_tiles, lf_tiles = gates[p, d]
        idx = p * nh + d * M_HEADS + hh
        j_li = 2 * d * M_HEADS + hh
        j_lf = j_li + M_HEADS
        sl = slice(hh * M_DH, (hh + 1) * M_DH)
        q = refs[d][0][p, :, sl]
        k = refs[d][1][p, :, sl] * (M_DH ** -0.5)
        v = refs[d][2][p, :, sl]
        b_t = lf_tiles[:, hh * LANES:(hh + 1) * LANES]
        li_t = li_tiles[:, hh * LANES:(hh + 1) * LANES]
        a_t = b_t + m_s[idx:idx + 1, :]
        dm = jnp.where(keeps[d], b_t - bct[j_lf:j_lf + 1, :] + lgt[j_li:j_li + 1, :], -jnp.inf)
        m_t = jnp.maximum(a_t, jnp.max(dm, axis=1, keepdims=True))
        st[ch] = dict(idx=idx, sl=sl, q=q, k=k, v=v, kt=k.T, b_t=b_t, li_t=li_t, m_t=m_t,
                      w=jnp.exp(dm - m_t), w0=jnp.exp(a_t - m_t))
    for ch in chains:
        c_ = st[ch]
        c_["s"] = c_["w"] * _mm(c_["q"], c_["kt"])
        c_["ct"] = ct_s[c_["idx"]]
        c_["qc"] = _mm(c_["q"], c_["ct"])
        c_["n_row"] = n_s[c_["idx"]:c_["idx"] + 1, :]
        c_["qn"] = _mm_nt(c_["q"], jnp.broadcast_to(c_["n_row"], (CHUNK, M_DH)))
    for ch in chains:
        p, d, hh = ch
        c_ = st[ch]
        num = c_["w0"] * c_["qc"] + _mm(c_["s"], c_["v"])
        den = c_["w0"] * c_["qn"] + jnp.sum(c_["s"], axis=1, keepdims=True)
        refs[d][4][p, :, c_["sl"]] = num / jnp.maximum(jnp.abs(den), jnp.exp(-c_["m_t"]))
    for ch in chains:
        p, d, hh = ch
        c_ = st[ch]
        idx = c_["idx"]
        te = 0 if d == 1 else CHUNK - 1
        m_end = c_["m_t"][te:te + 1, :]
        w0_end = c_["w0"][te:te + 1, :]
        w_end = jnp.exp(c_["b_t"][te:te + 1, :] - c_["b_t"] + c_["li_t"] - m_end)
        ct_s[idx] = w0_end * c_["ct"] + _mm(c_["kt"], c_["v"] * w_end)
        n_s[idx:idx + 1, :] = w0_end * c_["n_row"] + jnp.sum(c_["k"] * w_end, axis=0, keepdims=True)
        m_s[idx:idx + 1, :] = m_end

    @pl.when(c == pl.num_programs(1) - 1)
    def _():
        for p in range(nseq):
            for i in range(nh):
                c_out[p, i] = ct_s[p * nh + i].T
            n_out[p] = n_s[p * nh:(p + 1) * nh, :]
            m_out[p] = m_s[p * nh:(p + 1) * nh, :]


MLSTM_SEQS = 2


def _mlstm(z, bias_row, c0, n0, m0, nb, s):
    nc = s // CHUNK
    nh = 2 * M_HEADS
    ps = MLSTM_SEQS
    assert nb % ps == 0
    z3 = z.reshape(nb, s, Z_W)
    fwd = lambda col: (lambda b, c: (b, c, col))
    bwd = lambda col: (lambda b, c: (b, nc - 1 - c, col))
    wide = lambda im: pl.BlockSpec((ps, CHUNK, MIX_W), im)
    gate = lambda im: pl.BlockSpec((ps, CHUNK, LANES), im)
    state = lambda shape: pl.BlockSpec((ps,) + shape, lambda b, c: (b,) + (0,) * len(shape))
    gcol = Z_GM // LANES
    hf, hb, c_new, n_new, m_new = pl.pallas_call(
        _mlstm_kernel,
        grid=(nb // ps, nc),
        in_specs=[wide(fwd(0)), wide(fwd(1)), wide(fwd(2)), gate(fwd(gcol)),
                  wide(bwd(0)), wide(bwd(1)), wide(bwd(2)), gate(bwd(gcol)),
                  pl.BlockSpec((1, LANES), lambda b, c: (0, 0)),
                  state((nh, M_DH, M_DH)), state((nh, M_DH)), state((nh, LANES))],
        out_specs=[wide(fwd(0)), wide(bwd(0)),
                   state((nh, M_DH, M_DH)), state((nh, M_DH)), state((nh, LANES))],
        out_shape=[jax.ShapeDtypeStruct((nb, s, MIX_W), f32), jax.ShapeDtypeStruct((nb, s, MIX_W), f32),
                   jax.ShapeDtypeStruct((nb, nh, M_DH, M_DH), f32), jax.ShapeDtypeStruct((nb, nh, M_DH), f32),
                   jax.ShapeDtypeStruct((nb, nh, LANES), f32)],
        scratch_shapes=[pltpu.VMEM((ps * nh, M_DH, M_DH), f32), pltpu.VMEM((ps * nh, M_DH), f32),
                        pltpu.VMEM((ps * nh, LANES), f32)],
        compiler_params=_params(("parallel", "arbitrary")),
        name="mlstm",
    )(z3, z3, z3, z3, z3, z3, z3, z3, bias_row, c0, n0, m0)
    return hf.reshape(nb * s, MIX_W), hb.reshape(nb * s, MIX_W), c_new, n_new, m_new


def _half_placements(a):
    lane = lax.broadcasted_iota(jnp.int32, a.shape, 1)
    g0 = jnp.where(lane < A_DH, a, 0.0)
    g1 = jnp.where(lane >= A_DH, a, 0.0)
    return ((g0, pltpu.roll(g0, A_DH, 1)), (pltpu.roll(g1, A_DH, 1), g1))


def _group_attend(q, k_all, v_all, sink_ref, l, keep):
    kz = _half_placements(k_all)
    vz = _half_placements(v_all)
    heads = [(head, head // 2, head % 2, head // A_GROUP) for head in range(A_HEADS)]
    scores = [_mm_nt(q[:, slab * LANES:(slab + 1) * LANES], kz[g][pos]) * (A_DH ** -0.5)
              for _, slab, pos, g in heads]
    if keep is not None:
        scores = [jnp.where(keep, s, NEG_INF) for s in scores]
    sinks = [sink_ref[l, head] for head, _, _, _ in heads]
    tops = [jnp.maximum(jnp.max(s, axis=1, keepdims=True), sk) for s, sk in zip(scores, sinks)]
    probs = [jnp.exp(s - m) for s, m in zip(scores, tops)]
    dens = [jnp.sum(p, axis=1, keepdims=True) + jnp.exp(sk - m) for p, sk, m in zip(probs, sinks, tops)]
    outs = [_mm(p, vz[g][pos]) / den for p, den, (_, _, pos, g) in zip(probs, dens, heads)]
    return jnp.concatenate([outs[2 * slab] + outs[2 * slab + 1] for slab in range(A_HEADS // 2)], axis=1)


def _ctx_attn_kernel(l, sink_ref, q_ref, k_ref, v_ref, o_ref):
    o_ref[...] = _group_attend(q_ref[...], k_ref[...], v_ref[...], sink_ref, l, None)


def _ctx_attn(grp, z, sink, l):
    s = grp.s
    return pl.pallas_call(
        functools.partial(_ctx_attn_kernel, l),
        grid=(grp.nb,),
        in_specs=[pl.BlockSpec(memory_space=pltpu.SMEM),
                  pl.BlockSpec((s, MIX_W), lambda b: (b, Z_QA // MIX_W)),
                  pl.BlockSpec((s, A_KVW), lambda b: (b, Z_KA // A_KVW)),
                  pl.BlockSpec((s, A_KVW), lambda b: (b, Z_VA // A_KVW))],
        out_specs=pl.BlockSpec((s, MIX_W), lambda b: (b, 0)),
        out_shape=jax.ShapeDtypeStruct((grp.t, MIX_W), f32),
        compiler_params=_params(("parallel",)),
        name="ctx_attn",
    )(sink, z, z, z)


def _rope(x, cos, sin):
    lane = lax.broadcasted_iota(jnp.int32, cos.shape, 1)
    first = (lane % (2 * ROPE_FREQS)) < ROPE_FREQS
    out = []
    for j in range(x.shape[1] // LANES):
        xs = x[:, j * LANES:(j + 1) * LANES]
        partner = jnp.where(first, pltpu.roll(xs, LANES - ROPE_FREQS, 1), pltpu.roll(xs, ROPE_FREQS, 1))
        out.append(xs * cos + partner * sin)
    return out[0] if len(out) == 1 else jnp.concatenate(out, axis=1)


def _lat_attn_kernel(l, nblk, sink_ref, q_ref, kp_ref, kc_ref, kn_ref, vp_ref, vc_ref, vn_ref,
                     cq_ref, sq_ref, cp_ref, sp_ref, cn_ref, sn_ref, ck_ref, cv_ref, o_ref):
    i = pl.program_id(1)
    q = _rope(q_ref[...], cq_ref[...], sq_ref[...])
    k_all = jnp.concatenate([ck_ref[0, 0],
                             _rope(kp_ref[...], cp_ref[...], sp_ref[...]),
                             _rope(kc_ref[...], cq_ref[...], sq_ref[...]),
                             _rope(kn_ref[...], cn_ref[...], sn_ref[...])], axis=0)
    v_all = jnp.concatenate([cv_ref[0, 0], vp_ref[...], vc_ref[...], vn_ref[...]], axis=0)
    lc = ck_ref.shape[2]
    nk = lc + 3 * CHUNK
    r = lax.broadcasted_iota(jnp.int32, (CHUNK, nk), 0)
    cc = lax.broadcasted_iota(jnp.int32, (CHUNK, nk), 1) - lc
    lo = jnp.maximum(r, jnp.where(i == 0, CHUNK, 0))
    hi = jnp.minimum(r + 2 * CHUNK, jnp.where(i == nblk - 1, 2 * CHUNK - 1, 3 * CHUNK))
    keep = (cc < 0) | ((cc >= lo) & (cc <= hi))
    o_ref[...] = _group_attend(q, k_all, v_all, sink_ref, l, keep)


def _lat_attn(grp, z, sink, l, cache_k, cache_v, cos, sin):
    s, nb = grp.s, grp.nb
    nblk = s // CHUNK
    lc = cache_k.shape[2]

    def rows(off):
        return lambda b, i: b * nblk + jnp.clip(i + off, 0, nblk - 1)

    def zspec(width, col, off):
        rf = rows(off)
        return pl.BlockSpec((CHUNK, width), lambda b, i: (rf(b, i), col))

    def tab(off):
        return pl.BlockSpec((CHUNK, LANES), lambda b, i: (jnp.clip(i + off, 0, nblk - 1), 0))

    cache = pl.BlockSpec((1, 1, lc, A_KVW), lambda b, i: (b, l, 0, 0))
    kcol, vcol = Z_KA // A_KVW, Z_VA // A_KVW
    return pl.pallas_call(
        functools.partial(_lat_attn_kernel, l, nblk),
        grid=(nb, nblk),
        in_specs=[pl.BlockSpec(memory_space=pltpu.SMEM),
                  zspec(MIX_W, Z_QA // MIX_W, 0),
                  zspec(A_KVW, kcol, -1), zspec(A_KVW, kcol, 0), zspec(A_KVW, kcol, 1),
                  zspec(A_KVW, vcol, -1), zspec(A_KVW, vcol, 0), zspec(A_KVW, vcol, 1),
                  tab(0), tab(0), tab(-1), tab(-1), tab(1), tab(1), cache, cache],
        out_specs=pl.BlockSpec((CHUNK, MIX_W), lambda b, i: (b * nblk + i, 0)),
        out_shape=jax.ShapeDtypeStruct((grp.t, MIX_W), f32),
        compiler_params=_params(("parallel", "parallel")),
        name="lat_attn",
    )(sink, z, z, z, z, z, z, z, cos, sin, cos, sin, cos, sin, cache_k, cache_v)


def _rope_tables(s):
    t = jnp.arange(s)
    freqs = ROPE_BASE ** (-jnp.arange(ROPE_FREQS, dtype=f32) / ROPE_FREQS)
    a_row = (t // GRID_W).astype(f32)[:, None] * freqs[None, :]
    a_col = (t % GRID_W).astype(f32)[:, None] * freqs[None, :]
    cos = jnp.concatenate([jnp.cos(a_row)] * 2 + [jnp.cos(a_col)] * 2, axis=1)
    sin = jnp.concatenate([-jnp.sin(a_row), jnp.sin(a_row), -jnp.sin(a_col), jnp.sin(a_col)], axis=1)
    return jnp.concatenate([cos] * A_KV, axis=1), jnp.concatenate([sin] * A_KV, axis=1)


def _merge_kernel(x_ref, mod_ref, g1_ref, g2_ref, hf_ref, hb_ref, om_ref, uc_ref, vc_ref, ya_ref,
                  gm_ref, ws_ref, bs_ref, wmerge_ref, bmerge_ref, wbr_ref, wout_ref, wq_ref, keys_ref,
                  x1_ref, h2t_ref, st_ref):
    x = x_ref[...]
    tm = x.shape[0]
    mod = mod_ref[0]
    h = _mod_norm(x, g1_ref[...], mod[1:2], mod[0:1])
    hm = hf_ref[...] + hb_ref[...]
    ym = jnp.concatenate(
        [_rms(hm[:, hh * M_DH:(hh + 1) * M_DH]) * gm_ref[:, hh * M_DH:(hh + 1) * M_DH] for hh in range(M_HEADS)],
        axis=1) * jax.nn.sigmoid(om_ref[...])
    vr = _rms(vc_ref[...])
    zc = []
    for n in range(tm // CHUNK):
        rs = slice(n * CHUNK, (n + 1) * CHUNK)
        zc.append(jnp.concatenate(
            [_mm(ws_ref[g], vr[rs, g * LANES:(g + 1) * LANES]) + _col(bs_ref[...], g) for g in range(C_GROUPS)],
            axis=1))
    yc = uc_ref[...] * jnp.concatenate(zc, axis=0)
    mixed = jnp.zeros((tm, D_MODEL), f32)
    for n, y in enumerate((ym, yc, ya_ref[...])):
        gate = jax.nn.sigmoid(_mm(h, wmerge_ref[:, n * D_MODEL:(n + 1) * D_MODEL])
                              + bmerge_ref[:, n * D_MODEL:(n + 1) * D_MODEL])
        mixed = mixed + gate * _mm(y, wbr_ref[n])
    x1 = x + mod[2:3] * _mm(mixed, wout_ref[...])
    x1_ref[...] = x1
    h2 = _mod_norm(x1, g2_ref[...], mod[4:5], mod[3:4])
    h2t_ref[...] = h2.T.astype(h2t_ref.dtype)
    qp = _mm(h2, wq_ref[...])
    for hp in range(2 * PEER_HEADS):
        st_ref[hp] = _mm_nt(keys_ref[hp], qp[:, hp * N_KEYS:(hp + 1) * N_KEYS])


def _merge(grp, x, mod_l, g1, g2, hf, hb, z, ya, gm, ws, bs_t, wmerge, bmerge, wbr, wout, wq, keys):
    tm = 256
    row = grp.mod_row(tm)
    tok = lambda w, col=0: pl.BlockSpec((tm, w), lambda i: (i, col))
    full = lambda a: pl.BlockSpec(a.shape, lambda i: (0,) * a.ndim, pipeline_mode=pl.Buffered(1))
    return pl.pallas_call(
        _merge_kernel,
        grid=(grp.t // tm,),
        in_specs=[tok(D_MODEL), pl.BlockSpec((1, 6, D_MODEL), lambda i: (row(i), 0, 0)), full(g1), full(g2),
                  tok(MIX_W), tok(MIX_W), tok(MIX_W, Z_OM // MIX_W), tok(MIX_W, Z_UC // MIX_W),
                  tok(MIX_W, Z_VC // MIX_W), tok(MIX_W),
                  full(gm), full(ws), full(bs_t), full(wmerge), full(bmerge), full(wbr), full(wout), full(wq),
                  full(keys)],
        out_specs=[tok(D_MODEL), pl.BlockSpec((D_MODEL, tm), lambda i: (0, i)),
                   pl.BlockSpec((2 * PEER_HEADS, N_KEYS, tm), lambda i: (0, 0, i))],
        out_shape=[jax.ShapeDtypeStruct((grp.t, D_MODEL), f32), jax.ShapeDtypeStruct((D_MODEL, grp.t), MXU_DTYPE),
                   jax.ShapeDtypeStruct((2 * PEER_HEADS, N_KEYS, grp.t), f32)],
        compiler_params=_params(("parallel",)),
        name="merge",
    )(x, mod_l, g1, g2, hf, hb, z, z, z, ya, gm, ws, bs_t, wmerge, bmerge, wbr, wout, wq, keys)


_CELLS = [(i, j) for i in range(PEER_TOPK) for j in range(PEER_TOPK) if (i + 1) * (j + 1) <= PEER_TOPK]


def _route_kernel(st_ref, r2_ref, b_ref, jd_ref, a_ref, val_s, rank_s, work_s):
    tb = st_ref.shape[2]
    nhp = 2 * PEER_HEADS
    unranked = jnp.full((N_KEYS, tb), float(PEER_TOPK), f32)

    def start():
        work_s[...] = st_ref[...]
        for hp in range(nhp):
            rank_s[hp] = unranked

    def extract(hp, r, rf, hit_of):
        s = work_s[hp]
        m = jnp.max(s, axis=0, keepdims=True)
        hit = hit_of(s, m)
        val_s[hp % 2, r, pl.ds(hp // 2, 1), :] = m
        rank_s[hp] = jnp.where(hit, rf, rank_s[hp])
        work_s[hp] = jnp.where(hit, -jnp.inf, s)

    start()

    def fast_round(r, _):
        rf = lax.convert_element_type(r, f32)
        for hp in range(nhp):
            extract(hp, r, rf, lambda s, m: s == m)
        return 0

    lax.fori_loop(0, PEER_TOPK, fast_round, 0)
    removed = jnp.zeros((1, tb), f32)
    for hp in range(nhp):
        removed = jnp.maximum(removed, jnp.sum(jnp.where(work_s[hp] == -jnp.inf, 1.0, 0.0), axis=0, keepdims=True))

    @pl.when(jnp.max(removed) > float(PEER_TOPK))
    def _():
        key = lax.broadcasted_iota(jnp.int32, (N_KEYS, tb), 0).astype(f32)

        def lowest_index_hit(s, m):
            return key == jnp.min(jnp.where(s == m, key, float(N_KEYS)), axis=0, keepdims=True)

        start()

        def exact_round(i, _):
            extract(i // PEER_TOPK, i % PEER_TOPK, lax.convert_element_type(i % PEER_TOPK, f32), lowest_index_hit)
            return 0

        lax.fori_loop(0, nhp * PEER_TOPK, exact_round, 0)

    v1 = [val_s[0, i] for i in range(PEER_TOPK)]
    v2 = [val_s[1, i] for i in range(PEER_TOPK)]
    cand = [v1[i] + v2[j] for (i, j) in _CELLS]
    ncell = len(_CELLS)
    before = [jnp.zeros((PEER_HEADS, tb), f32) for _ in range(ncell)]
    for x in range(ncell):
        for y in range(x + 1, ncell):
            x_first = (cand[x] >= cand[y]).astype(f32)
            before[y] = before[y] + x_first
            before[x] = before[x] + (1.0 - x_first)
    ea = [jnp.exp(v1[i] - v1[0]) for i in range(PEER_TOPK)]
    eb = [jnp.exp(v2[j] - v2[0]) for j in range(PEER_TOPK)]
    jcount = [jnp.zeros((PEER_HEADS, tb), f32) for _ in range(PEER_TOPK)]
    zsum = jnp.zeros((PEER_HEADS, tb), f32)
    for x, (i, j) in enumerate(_CELLS):
        sel = (before[x] < PEER_TOPK).astype(f32)
        jcount[i] = jcount[i] + sel
        zsum = zsum + sel * (ea[i] * eb[j])
    inv_z = 1.0 / zsum

    for h in range(PEER_HEADS):
        s1 = st_ref[2 * h]
        s2 = st_ref[2 * h + 1]
        rank1 = rank_s[2 * h]
        jd = jnp.zeros((N_KEYS, tb), f32)
        for i in range(PEER_TOPK):
            jd = jnp.where(rank1 == float(i), jcount[i][h:h + 1, :], jd)
        jd_ref[h] = _pair_words(jd)
        a_ref[h] = _pair_words(jnp.exp(s1 - v1[0][h:h + 1, :]) * inv_z[h:h + 1, :])
        b_ref[h] = jnp.exp(s2 - v2[0][h:h + 1, :]).astype(b_ref.dtype)
        r2_ref[h] = rank_s[2 * h + 1].astype(r2_ref.dtype)


def _route(st):
    t = st.shape[2]
    tb = LANES
    out = [jax.ShapeDtypeStruct((PEER_HEADS, N_KEYS, t), dt)
           for dt in (GATE_DTYPE, GATE_DTYPE, jnp.uint32, jnp.uint32)]
    spec = pl.BlockSpec((PEER_HEADS, N_KEYS, tb), lambda i: (0, 0, i))
    return pl.pallas_call(
        _route_kernel,
        grid=(t // tb,),
        in_specs=[pl.BlockSpec((2 * PEER_HEADS, N_KEYS, tb), lambda i: (0, 0, i))],
        out_specs=[spec] * 4,
        out_shape=out,
        scratch_shapes=[pltpu.VMEM((2, PEER_TOPK, PEER_HEADS, tb), f32),
                        pltpu.VMEM((2 * PEER_HEADS, N_KEYS, tb), f32),
                        pltpu.VMEM((2 * PEER_HEADS, N_KEYS, tb), f32)],
        compiler_params=_params(("parallel",)),
        name="peer_route",
    )(st)


PEER_TT = 512
PEER_E1 = 16
PEER_SUBS = (4, 4, 4, 4)
PEER_ROWS = 16


def _pair_words(x):
    u = lax.bitcast_convert_type(x.astype(jnp.bfloat16).astype(f32), jnp.uint32)
    return u | (u >> 16)


def _bcast_pair_row(ref, hh, e1):
    assert PEER_ROWS == 16
    row = ref[hh, e1:e1 + 1, :]
    return pltpu.bitcast(jnp.broadcast_to(row, (PEER_ROWS // 2, row.shape[1])), jnp.bfloat16)


def _peer_kernel(final, h2t_ref, r2_ref, b_ref, jd_ref, a_ref, u_ref, vt_ref, x1_ref, mod_ref, gf_ref,
                 o_ref, acc_s, a_s, h_s):
    j = pl.program_id(1)

    @pl.when(j == 0)
    def _():
        acc_s[...] = jnp.zeros_like(acc_s)

    assert sum(PEER_SUBS) == PEER_E1
    starts = [sum(PEER_SUBS[:sb]) for sb in range(len(PEER_SUBS))]
    n_sub = len(PEER_SUBS)
    srow = lambda sb: slice(starts[sb] * N_KEYS, (starts[sb] + PEER_SUBS[sb]) * N_KEYS)
    def hdot(sb):
        h_s[sb % 2, :PEER_SUBS[sb] * N_KEYS, :] = jnp.dot(u_ref[srow(sb), :], h2t_ref[...],
                                                         preferred_element_type=f32).astype(h_s.dtype)

    def accumulate(sb):
        acc_s[...] += jnp.dot(vt_ref[:, srow(sb)], a_s[srow(sb), :], preferred_element_type=f32)

    hdot(0)
    for sb in range(n_sub):
        h = h_s.at[sb % 2]
        if sb + 1 < n_sub:
            hdot(sb + 1)
        if sb >= 1:
            accumulate(sb - 1)
        for e in range(PEER_SUBS[sb]):
            e1 = starts[sb] + e
            n_chunk = N_KEYS // PEER_ROWS
            gates = [None] * n_chunk
            for hh in range(PEER_HEADS):
                jd = _bcast_pair_row(jd_ref, hh, e1)
                aa = _bcast_pair_row(a_ref, hh, e1)
                for r in range(n_chunk):
                    rows = slice(r * PEER_ROWS, (r + 1) * PEER_ROWS)
                    term = jnp.where(r2_ref[hh, rows, :] < jd, b_ref[hh, rows, :], jnp.zeros((), GATE_DTYPE)) * aa
                    gates[r] = term if gates[r] is None else gates[r] + term
            for r in range(n_chunk):
                lo = e * N_KEYS + r * PEER_ROWS
                act = gates[r] * jax.nn.gelu(h[lo:lo + PEER_ROWS, :])
                a_s[e1 * N_KEYS + r * PEER_ROWS:e1 * N_KEYS + (r + 1) * PEER_ROWS, :] = act.astype(a_s.dtype)
    accumulate(n_sub - 1)

    @pl.when(j == pl.num_programs(1) - 1)
    def _():
        x2 = x1_ref[...] + mod_ref[0, 5:6, :] * acc_s[...].T
        o_ref[...] = _rms(x2) * gf_ref[...] if final else x2


def _peer(grp, h2t, r2, b, jd, a, u, vt, x1, mod_l, g_final, final):
    tt = PEER_TT
    eb = PEER_E1 * N_KEYS
    row = grp.mod_row(tt)
    gate = pl.BlockSpec((PEER_HEADS, N_KEYS, tt), lambda i, j: (0, 0, i))
    gate1 = pl.BlockSpec((PEER_HEADS, PEER_E1, tt), lambda i, j: (0, j, i))
    return pl.pallas_call(
        functools.partial(_peer_kernel, final),
        grid=(grp.t // tt, N_EXPERTS // eb),
        in_specs=[pl.BlockSpec((D_MODEL, tt), lambda i, j: (0, i)), gate, gate, gate1, gate1,
                  pl.BlockSpec((eb, D_MODEL), lambda i, j: (j, 0)),
                  pl.BlockSpec((D_MODEL, eb), lambda i, j: (0, j)),
                  pl.BlockSpec((tt, D_MODEL), lambda i, j: (i, 0)),
                  pl.BlockSpec((1, 6, D_MODEL), lambda i, j: (row(i), 0, 0)),
                  pl.BlockSpec((1, D_MODEL), lambda i, j: (0, 0))],
        out_specs=pl.BlockSpec((tt, D_MODEL), lambda i, j: (i, 0)),
        out_shape=jax.ShapeDtypeStruct((grp.t, D_MODEL), f32),
        scratch_shapes=[pltpu.VMEM((D_MODEL, tt), f32), pltpu.VMEM((eb, tt), MXU_DTYPE),
                        pltpu.VMEM((2, max(PEER_SUBS) * N_KEYS, tt), GATE_DTYPE)],
        compiler_params=_params(("parallel", "arbitrary")),
        name="peer",
    )(h2t, r2, b, jd, a, u, vt, x1, mod_l, g_final)


def _reorder_w_in(w_in_l):
    ng = 4 * M_HEADS
    a = w_in_l[:, :4 * MIX_W]
    g = w_in_l[:, 4 * MIX_W:4 * MIX_W + ng]
    rest = w_in_l[:, 4 * MIX_W + ng:]
    pad = jnp.zeros((D_MODEL, LANES - ng), w_in_l.dtype)
    return jnp.concatenate([a, rest, g, pad], axis=1).astype(MXU_DTYPE)


def _forward(x_prompt, x_sample, cache_k, cache_v, state_C, state_n, state_m, c, c_ctx,
             w_ada, b_ada, g_norm1, g_norm2, w_in, b_gates_m, g_mlstm, w_spatial, b_spatial, sink,
             w_branch, w_merge, b_merge, w_out, w_peer_q, peer_keys, peer_u, peer_v, g_final):
    nb_c, s_c, _ = x_prompt.shape
    nb_l, s_l, _ = x_sample.shape
    ctx = _Group(nb_c, s_c, 0, False)
    lat = _Group(nb_l, s_l, 1, True)
    lc = cache_k.shape[2]
    nh = 2 * M_HEADS

    cond = jnp.concatenate([c_ctx[None, :], c], axis=0)
    cond = jnp.pad(cond, ((0, (-cond.shape[0]) % 8), (0, 0)))
    mod = _ada(cond, w_ada, b_ada).reshape(DEPTH, cond.shape[0], 6, D_MODEL)
    xc = x_prompt.reshape(ctx.t, D_MODEL)
    xl = x_sample.reshape(lat.t, D_MODEL)
    cos, sin = _rope_tables(s_l)
    ck = cache_k.reshape(nb_l, DEPTH, lc, A_KVW)
    cv = cache_v.reshape(nb_l, DEPTH, lc, A_KVW)
    zero_c = jnp.zeros((nb_c, nh, M_DH, M_DH), f32)
    zero_n = jnp.zeros((nb_c, nh, M_DH), f32)
    zero_m = jnp.zeros((nb_c, nh, LANES), f32)
    gfin = g_final.reshape(1, D_MODEL)

    ks, vs, cs, ns, ms = [], [], [], [], []
    for l in range(DEPTH):
        g1 = g_norm1[l].reshape(1, D_MODEL)
        g2 = g_norm2[l].reshape(1, D_MODEL)
        w_in_l = _reorder_w_in(w_in[l])
        bias_row = jnp.pad(b_gates_m[l].reshape(1, 4 * M_HEADS), ((0, 0), (0, LANES - 4 * M_HEADS)))
        merge_w = (g_mlstm[l].reshape(1, MIX_W), w_spatial[l].astype(MXU_DTYPE),
                   jnp.pad(b_spatial[l].T, ((0, 0), (0, LANES - C_GROUPS))), w_merge[l].astype(MXU_DTYPE),
                   b_merge[l].reshape(1, 3 * D_MODEL), w_branch[l].astype(MXU_DTYPE), w_out[l].astype(MXU_DTYPE),
                   w_peer_q[l].astype(MXU_DTYPE),
                   peer_keys[l].reshape(2 * PEER_HEADS, N_KEYS, N_KEYS).astype(MXU_DTYPE))
        u = peer_u[l].astype(MXU_DTYPE)
        vt = peer_v[l].T.astype(MXU_DTYPE)
        final = l == DEPTH - 1

        zc, k_new, v_new = _inproj(ctx, xc, mod[l], g1, w_in_l, True)
        hf, hb, c_new, n_new, m_new = _mlstm(zc, bias_row, zero_c, zero_n, zero_m, nb_c, s_c)
        ya = _ctx_attn(ctx, zc, sink, l)
        x1, h2t, st = _merge(ctx, xc, mod[l], g1, g2, hf, hb, zc, ya, *merge_w)
        xc = _peer(ctx, h2t, *_route(st), u, vt, x1, mod[l], gfin, final)
        ks.append(k_new.reshape(nb_c, s_c, A_KV, A_DH))
        vs.append(v_new.reshape(nb_c, s_c, A_KV, A_DH))
        cs.append(c_new.reshape(nb_c, 2, M_HEADS, M_DH, M_DH))
        ns.append(n_new.reshape(nb_c, 2, M_HEADS, M_DH))
        ms.append(m_new[:, :, 0].reshape(nb_c, 2, M_HEADS))

        zl = _inproj(lat, xl, mod[l], g1, w_in_l, False)[0]
        m0 = jnp.broadcast_to(state_m[:, l].reshape(nb_l, nh, 1), (nb_l, nh, LANES))
        hf, hb, _, _, _ = _mlstm(zl, bias_row, state_C[:, l].reshape(nb_l, nh, M_DH, M_DH),
                                 state_n[:, l].reshape(nb_l, nh, M_DH), m0, nb_l, s_l)
        ya = _lat_attn(lat, zl, sink, l, ck, cv, cos, sin)
        x1, h2t, st = _merge(lat, xl, mod[l], g1, g2, hf, hb, zl, ya, *merge_w)
        xl = _peer(lat, h2t, *_route(st), u, vt, x1, mod[l], gfin, final)
    return (xc.reshape(nb_c, s_c, D_MODEL), xl.reshape(nb_l, s_l, D_MODEL),
            jnp.stack(ks, axis=1), jnp.stack(vs, axis=1), jnp.stack(cs, axis=1), jnp.stack(ns, axis=1),
            jnp.stack(ms, axis=1))


def kernel(x_prompt, x_sample, cache_k, cache_v, state_C, state_n, state_m, c, c_ctx, w_ada, b_ada, g_norm1, g_norm2, w_in, b_gates_m, g_mlstm, w_spatial, b_spatial, sink, w_branch, w_merge, b_merge, w_out, w_peer_q, peer_keys, peer_u, peer_v, g_final):
    return _forward(x_prompt, x_sample, cache_k, cache_v, state_C, state_n, state_m, c, c_ctx, w_ada, b_ada,
                    g_norm1, g_norm2, w_in, b_gates_m, g_mlstm, w_spatial, b_spatial, sink, w_branch, w_merge,
                    b_merge, w_out, w_peer_q, peer_keys, peer_u, peer_v, g_final)
```

```python
import functools

import numpy as np
import jax
import jax.numpy as jnp
from jax import lax
from jax.experimental import pallas as pl
from jax.experimental.pallas import tpu as pltpu

D_MODEL = 1024
DEPTH = 2
GRID_W = 64
EPS = 1e-6
NEG_INF = -1e30
MIX_W = D_MODEL // 2
M_HEADS = 4
M_DH = MIX_W // M_HEADS
CHUNK = 128
C_GROUPS = 4
A_HEADS = 8
A_KV = 2
A_GROUP = A_HEADS // A_KV
A_DH = MIX_W // A_HEADS
A_KVW = A_KV * A_DH
ROPE_BASE = 10000.0
ROPE_FREQS = A_DH // 4
N_KEYS = 128
N_EXPERTS = N_KEYS * N_KEYS
PEER_HEADS = 8
PEER_TOPK = 16
PEER_QW = 2 * PEER_HEADS * N_KEYS

LANES = 128
MXU_DTYPE = jnp.bfloat16
GATE_DTYPE = jnp.bfloat16
VMEM_LIMIT = 56 * 1024 * 1024

Z_QM, Z_KM, Z_VM, Z_OM, Z_UC, Z_VC, Z_QA = (i * MIX_W for i in range(7))
Z_KA = 7 * MIX_W
Z_VA = Z_KA + A_KVW
Z_GM = Z_VA + A_KVW
Z_W = Z_GM + LANES

f32 = jnp.float32


def _params(sem):
    return pltpu.CompilerParams(dimension_semantics=sem, vmem_limit_bytes=VMEM_LIMIT)


def _mm(a, b):
    return jnp.dot(a.astype(MXU_DTYPE), b.astype(MXU_DTYPE), preferred_element_type=f32)


def _mm_nt(a, b):
    return lax.dot_general(a.astype(MXU_DTYPE), b.astype(MXU_DTYPE), (((1,), (1,)), ((), ())),
                           preferred_element_type=f32)


def _split3(x):
    hi = x.astype(jnp.bfloat16)
    r1 = x - hi.astype(f32)
    mid = r1.astype(jnp.bfloat16)
    lo = (r1 - mid.astype(f32)).astype(jnp.bfloat16)
    return hi, mid, lo


def _mm_exact_lhs(a01, x):
    a = a01.astype(jnp.bfloat16)
    hi, mid, lo = _split3(x)
    return (jnp.dot(a, hi, preferred_element_type=f32) + jnp.dot(a, mid, preferred_element_type=f32)
            + jnp.dot(a, lo, preferred_element_type=f32))


def _mm_exact_rhs(x, b01):
    b = b01.astype(jnp.bfloat16)
    hi, mid, lo = _split3(x)
    return (jnp.dot(hi, b, preferred_element_type=f32) + jnp.dot(mid, b, preferred_element_type=f32)
            + jnp.dot(lo, b, preferred_element_type=f32))


def _mm3(a, b):
    ah, am, al = _split3(a)
    bh, bm, bl = _split3(b)
    d = functools.partial(jnp.dot, preferred_element_type=f32)
    return (d(ah, bh) + (d(ah, bm) + d(am, bh)) + (d(ah, bl) + d(al, bh) + d(am, bm)))


def _rms(x):
    return x * lax.rsqrt(jnp.mean(x * x, axis=-1, keepdims=True) + EPS)


def _mod_norm(x, g, scale, shift):
    return _rms(x) * g * (1.0 + scale) + shift


def _ada_kernel(c_ref, w_ref, b_ref, o_ref):
    c = c_ref[...]
    o_ref[0] = _mm3(c * jax.nn.sigmoid(c), w_ref[0]) + b_ref[0]


def _ada(cond, w_ada, b_ada):
    rows = cond.shape[0]
    tn = 1536
    return pl.pallas_call(
        _ada_kernel,
        grid=(DEPTH, 6 * D_MODEL // tn),
        in_specs=[pl.BlockSpec((rows, D_MODEL), lambda l, j: (0, 0)),
                  pl.BlockSpec((1, D_MODEL, tn), lambda l, j: (l, 0, j)),
                  pl.BlockSpec((1, 1, tn), lambda l, j: (l, 0, j))],
        out_specs=pl.BlockSpec((1, rows, tn), lambda l, j: (l, 0, j)),
        out_shape=jax.ShapeDtypeStruct((DEPTH, rows, 6 * D_MODEL), f32),
        compiler_params=_params(("parallel", "parallel")),
        name="ada",
    )(cond, w_ada, b_ada.reshape(DEPTH, 1, 6 * D_MODEL))


class _Group:
    def __init__(self, nb, s, cond0, per_sequence):
        self.nb, self.s, self.cond0, self.per_sequence = nb, s, cond0, per_sequence
        self.t = nb * s

    def mod_row(self, tm):
        assert self.t % tm == 0
        if not self.per_sequence:
            return lambda i: self.cond0
        assert self.s % tm == 0
        return lambda i: self.cond0 + i // (self.s // tm)


def _inproj_kernel(x_ref, mod_ref, g1_ref, w_ref, z_ref, *kv_refs):
    h = _mod_norm(x_ref[...], g1_ref[...], mod_ref[0, 1:2, :], mod_ref[0, 0:1, :])
    z = _mm(h, w_ref[...])
    z_ref[...] = z
    if kv_refs:
        kv_refs[0][...] = z[:, Z_KA:Z_KA + A_KVW]
        kv_refs[1][...] = z[:, Z_VA:Z_VA + A_KVW]


def _inproj(grp, x, mod_l, g1, w, emit_kv):
    tm = 256
    row = grp.mod_row(tm)
    kv_spec = [pl.BlockSpec((tm, A_KVW), lambda i: (i, 0))] * 2 if emit_kv else []
    kv_shape = [jax.ShapeDtypeStruct((grp.t, A_KVW), f32)] * 2 if emit_kv else []
    return pl.pallas_call(
        _inproj_kernel,
        grid=(grp.t // tm,),
        in_specs=[pl.BlockSpec((tm, D_MODEL), lambda i: (i, 0)),
                  pl.BlockSpec((1, 6, D_MODEL), lambda i: (row(i), 0, 0)),
                  pl.BlockSpec((1, D_MODEL), lambda i: (0, 0)),
                  pl.BlockSpec((D_MODEL, Z_W), lambda i: (0, 0), pipeline_mode=pl.Buffered(1))],
        out_specs=[pl.BlockSpec((tm, Z_W), lambda i: (i, 0))] + kv_spec,
        out_shape=[jax.ShapeDtypeStruct((grp.t, Z_W), f32)] + kv_shape,
        compiler_params=_params(("parallel",)),
        name="inproj",
    )(x, mod_l, g1, w)


def _log_sigmoid(x):
    return jnp.minimum(x, 0.0) - jnp.log1p(jnp.exp(-jnp.abs(x)))


def _col(a, j):
    lane = lax.broadcasted_iota(jnp.int32, a.shape, 1)
    return jnp.sum(jnp.where(lane == j, a, 0.0), axis=1, keepdims=True)


def _mlstm_kernel(qf_ref, kf_ref, vf_ref, gf_ref, qb_ref, kb_ref, vb_ref, gb_ref, bias_ref,
                  c0_ref, n0_ref, m0_ref, hf_ref, hb_ref, c_out, n_out, m_out, ct_s, n_s, m_s):
    c = pl.program_id(1)
    nseq = qf_ref.shape[0]
    nh = 2 * M_HEADS

    @pl.when(c == 0)
    def _():
        for p in range(nseq):
            for i in range(nh):
                ct_s[p * nh + i] = c0_ref[p, i].T
            n_s[p * nh:(p + 1) * nh, :] = n0_ref[p]
            m_s[p * nh:(p + 1) * nh, :] = m0_ref[p]

    ri = lax.broadcasted_iota(jnp.int32, (CHUNK, CHUNK), 0)
    ci = lax.broadcasted_iota(jnp.int32, (CHUNK, CHUNK), 1)
    lane = lax.broadcasted_iota(jnp.int32, (CHUNK, LANES), 1)
    is_forget = ((lane // M_HEADS) % 2) == 1

    refs = ((qf_ref, kf_ref, vf_ref, gf_ref, hf_ref), (qb_ref, kb_ref, vb_ref, gb_ref, hb_ref))
    keeps = (ri >= ci, ri <= ci)
    krow = lax.broadcasted_iota(jnp.int32, (LANES, M_HEADS * LANES), 0)
    head = lax.broadcasted_iota(jnp.int32, (LANES, M_HEADS * LANES), 1) // LANES
    gates = {}
    for p in range(nseq):
        for d in range(2):
            g = refs[d][3][p] + bias_ref[...]
            lg = jnp.where(is_forget, _log_sigmoid(g), g)
            bc = _mm_exact_lhs(keeps[d], lg)
            li_tiles = _mm_exact_rhs(lg, krow == 2 * d * M_HEADS + head)
            lf_tiles = _mm_exact_rhs(bc, krow == (2 * d + 1) * M_HEADS + head)
            gates[p, d] = (lg.T, bc.T, li_tiles, lf_tiles)

    chains = [(p, d, hh) for p in range(nseq) for d in range(2) for hh in range(M_HEADS)]
    st = {}
    for ch in chains:
        p, d, hh = ch
        lgt, bct, li_tiles, lf_tiles = gates[p, d]
        idx = p * nh + d * M_HEADS + hh
        j_li = 2 * d * M_HEADS + hh
        j_lf = j_li + M_HEADS
        sl = slice(hh * M_DH, (hh + 1) * M_DH)
        q = refs[d][0][p, :, sl]
        k = refs[d][1][p, :, sl] * (M_DH ** -0.5)
        v = refs[d][2][p, :, sl]
        b_t = lf_tiles[:, hh * LANES:(hh + 1) * LANES]
        li_t = li_tiles[:, hh * LANES:(hh + 1) * LANES]
        a_t = b_t + m_s[idx:idx + 1, :]
        dm = jnp.where(keeps[d], b_t - bct[j_lf:j_lf + 1, :] + lgt[j_li:j_li + 1, :], -jnp.inf)
        m_t = jnp.maximum(a_t, jnp.max(dm, axis=1, keepdims=True))
        st[ch] = dict(idx=idx, sl=sl, q=q, k=k, v=v, kt=k.T, b_t=b_t, li_t=li_t, m_t=m_t,
                      w=jnp.exp(dm - m_t), w0=jnp.exp(a_t - m_t))
    for ch in chains:
        c_ = st[ch]
        c_["s"] = c_["w"] * _mm(c_["q"], c_["kt"])
        c_["ct"] = ct_s[c_["idx"]]
        c_["qc"] = _mm(c_["q"], c_["ct"])
        c_["n_row"] = n_s[c_["idx"]:c_["idx"] + 1, :]
        c_["qn"] = _mm_nt(c_["q"], jnp.broadcast_to(c_["n_row"], (CHUNK, M_DH)))
    for ch in chains:
        p, d, hh = ch
        c_ = st[ch]
        num = c_["w0"] * c_["qc"] + _mm(c_["s"], c_["v"])
        den = c_["w0"] * c_["qn"] + jnp.sum(c_["s"], axis=1, keepdims=True)
        refs[d][4][p, :, c_["sl"]] = num / jnp.maximum(jnp.abs(den), jnp.exp(-c_["m_t"]))
    for ch in chains:
        p, d, hh = ch
        c_ = st[ch]
        idx = c_["idx"]
        te = 0 if d == 1 else CHUNK - 1
        m_end = c_["m_t"][te:te + 1, :]
        w0_end = c_["w0"][te:te + 1, :]
        w_end = jnp.exp(c_["b_t"][te:te + 1, :] - c_["b_t"] + c_["li_t"] - m_end)
        ct_s[idx] = w0_end * c_["ct"] + _mm(c_["kt"], c_["v"] * w_end)
        n_s[idx:idx + 1, :] = w0_end * c_["n_row"] + jnp.sum(c_["k"] * w_end, axis=0, keepdims=True)
        m_s[idx:idx + 1, :] = m_end

    @pl.when(c == pl.num_programs(1) - 1)
    def _():
        for p in range(nseq):
            for i in range(nh):
                c_out[p, i] = ct_s[p * nh + i].T
            n_out[p] = n_s[p * nh:(p + 1) * nh, :]
            m_out[p] = m_s[p * nh:(p + 1) * nh, :]


MLSTM_SEQS = 2


def _mlstm(z, bias_row, c0, n0, m0, nb, s):
    nc = s // CHUNK
    nh = 2 * M_HEADS
    ps = MLSTM_SEQS
    assert nb % ps == 0
    z3 = z.reshape(nb, s, Z_W)
    fwd = lambda col: (lambda b, c: (b, c, col))
    bwd = lambda col: (lambda b, c: (b, nc - 1 - c, col))
    wide = lambda im: pl.BlockSpec((ps, CHUNK, MIX_W), im)
    gate = lambda im: pl.BlockSpec((ps, CHUNK, LANES), im)
    state = lambda shape: pl.BlockSpec((ps,) + shape, lambda b, c: (b,) + (0,) * len(shape))
    gcol = Z_GM // LANES
    hf, hb, c_new, n_new, m_new = pl.pallas_call(
        _mlstm_kernel,
        grid=(nb // ps, nc),
        in_specs=[wide(fwd(0)), wide(fwd(1)), wide(fwd(2)), gate(fwd(gcol)),
                  wide(bwd(0)), wide(bwd(1)), wide(bwd(2)), gate(bwd(gcol)),
                  pl.BlockSpec((1, LANES), lambda b, c: (0, 0)),
                  state((nh, M_DH, M_DH)), state((nh, M_DH)), state((nh, LANES))],
        out_specs=[wide(fwd(0)), wide(bwd(0)),
                   state((nh, M_DH, M_DH)), state((nh, M_DH)), state((nh, LANES))],
        out_shape=[jax.ShapeDtypeStruct((nb, s, MIX_W), f32), jax.ShapeDtypeStruct((nb, s, MIX_W), f32),
                   jax.ShapeDtypeStruct((nb, nh, M_DH, M_DH), f32), jax.ShapeDtypeStruct((nb, nh, M_DH), f32),
                   jax.ShapeDtypeStruct((nb, nh, LANES), f32)],
        scratch_shapes=[pltpu.VMEM((ps * nh, M_DH, M_DH), f32), pltpu.VMEM((ps * nh, M_DH), f32),
                        pltpu.VMEM((ps * nh, LANES), f32)],
        compiler_params=_params(("parallel", "arbitrary")),
        name="mlstm",
    )(z3, z3, z3, z3, z3, z3, z3, z3, bias_row, c0, n0, m0)
    return hf.reshape(nb * s, MIX_W), hb.reshape(nb * s, MIX_W), c_new, n_new, m_new


def _half_placements(a):
    lane = lax.broadcasted_iota(jnp.int32, a.shape, 1)
    g0 = jnp.where(lane < A_DH, a, 0.0)
    g1 = jnp.where(lane >= A_DH, a, 0.0)
    return ((g0, pltpu.roll(g0, A_DH, 1)), (pltpu.roll(g1, A_DH, 1), g1))


def _group_attend(q, k_all, v_all, sink_ref, l, keep):
    kz = _half_placements(k_all)
    vz = _half_placements(v_all)
    heads = [(head, head // 2, head % 2, head // A_GROUP) for head in range(A_HEADS)]
    scores = [_mm_nt(q[:, slab * LANES:(slab + 1) * LANES], kz[g][pos]) * (A_DH ** -0.5)
              for _, slab, pos, g in heads]
    if keep is not None:
        scores = [jnp.where(keep, s, NEG_INF) for s in scores]
    sinks = [sink_ref[l, head] for head, _, _, _ in heads]
    tops = [jnp.maximum(jnp.max(s, axis=1, keepdims=True), sk) for s, sk in zip(scores, sinks)]
    probs = [jnp.exp(s - m) for s, m in zip(scores, tops)]
    dens = [jnp.sum(p, axis=1, keepdims=True) + jnp.exp(sk - m) for p, sk, m in zip(probs, sinks, tops)]
    outs = [_mm(p, vz[g][pos]) / den for p, den, (_, _, pos, g) in zip(probs, dens, heads)]
    return jnp.concatenate([outs[2 * slab] + outs[2 * slab + 1] for slab in range(A_HEADS // 2)], axis=1)


def _ctx_attn_kernel(l, sink_ref, q_ref, k_ref, v_ref, o_ref):
    o_ref[...] = _group_attend(q_ref[...], k_ref[...], v_ref[...], sink_ref, l, None)


def _ctx_attn(grp, z, sink, l):
    s = grp.s
    return pl.pallas_call(
        functools.partial(_ctx_attn_kernel, l),
        grid=(grp.nb,),
        in_specs=[pl.BlockSpec(memory_space=pltpu.SMEM),
                  pl.BlockSpec((s, MIX_W), lambda b: (b, Z_QA // MIX_W)),
                  pl.BlockSpec((s, A_KVW), lambda b: (b, Z_KA // A_KVW)),
                  pl.BlockSpec((s, A_KVW), lambda b: (b, Z_VA // A_KVW))],
        out_specs=pl.BlockSpec((s, MIX_W), lambda b: (b, 0)),
        out_shape=jax.ShapeDtypeStruct((grp.t, MIX_W), f32),
        compiler_params=_params(("parallel",)),
        name="ctx_attn",
    )(sink, z, z, z)


def _rope(x, cos, sin):
    lane = lax.broadcasted_iota(jnp.int32, cos.shape, 1)
    first = (lane % (2 * ROPE_FREQS)) < ROPE_FREQS
    out = []
    for j in range(x.shape[1] // LANES):
        xs = x[:, j * LANES:(j + 1) * LANES]
        partner = jnp.where(first, pltpu.roll(xs, LANES - ROPE_FREQS, 1), pltpu.roll(xs, ROPE_FREQS, 1))
        out.append(xs * cos + partner * sin)
    return out[0] if len(out) == 1 else jnp.concatenate(out, axis=1)


def _lat_attn_kernel(l, nblk, sink_ref, q_ref, kp_ref, kc_ref, kn_ref, vp_ref, vc_ref, vn_ref,
                     cq_ref, sq_ref, cp_ref, sp_ref, cn_ref, sn_ref, ck_ref, cv_ref, o_ref):
    i = pl.program_id(1)
    q = _rope(q_ref[...], cq_ref[...], sq_ref[...])
    k_all = jnp.concatenate([ck_ref[0, 0],
                             _rope(kp_ref[...], cp_ref[...], sp_ref[...]),
                             _rope(kc_ref[...], cq_ref[...], sq_ref[...]),
                             _rope(kn_ref[...], cn_ref[...], sn_ref[...])], axis=0)
    v_all = jnp.concatenate([cv_ref[0, 0], vp_ref[...], vc_ref[...], vn_ref[...]], axis=0)
    lc = ck_ref.shape[2]
    nk = lc + 3 * CHUNK
    r = lax.broadcasted_iota(jnp.int32, (CHUNK, nk), 0)
    cc = lax.broadcasted_iota(jnp.int32, (CHUNK, nk), 1) - lc
    lo = jnp.maximum(r, jnp.where(i == 0, CHUNK, 0))
    hi = jnp.minimum(r + 2 * CHUNK, jnp.where(i == nblk - 1, 2 * CHUNK - 1, 3 * CHUNK))
    keep = (cc < 0) | ((cc >= lo) & (cc <= hi))
    o_ref[...] = _group_attend(q, k_all, v_all, sink_ref, l, keep)


def _lat_attn(grp, z, sink, l, cache_k, cache_v, cos, sin):
    s, nb = grp.s, grp.nb
    nblk = s // CHUNK
    lc = cache_k.shape[2]

    def rows(off):
        return lambda b, i: b * nblk + jnp.clip(i + off, 0, nblk - 1)

    def zspec(width, col, off):
        rf = rows(off)
        return pl.BlockSpec((CHUNK, width), lambda b, i: (rf(b, i), col))

    def tab(off):
        return pl.BlockSpec((CHUNK, LANES), lambda b, i: (jnp.clip(i + off, 0, nblk - 1), 0))

    cache = pl.BlockSpec((1, 1, lc, A_KVW), lambda b, i: (b, l, 0, 0))
    kcol, vcol = Z_KA // A_KVW, Z_VA // A_KVW
    return pl.pallas_call(
        functools.partial(_lat_attn_kernel, l, nblk),
        grid=(nb, nblk),
        in_specs=[pl.BlockSpec(memory_space=pltpu.SMEM),
                  zspec(MIX_W, Z_QA // MIX_W, 0),
                  zspec(A_KVW, kcol, -1), zspec(A_KVW, kcol, 0), zspec(A_KVW, kcol, 1),
                  zspec(A_KVW, vcol, -1), zspec(A_KVW, vcol, 0), zspec(A_KVW, vcol, 1),
                  tab(0), tab(0), tab(-1), tab(-1), tab(1), tab(1), cache, cache],
        out_specs=pl.BlockSpec((CHUNK, MIX_W), lambda b, i: (b * nblk + i, 0)),
        out_shape=jax.ShapeDtypeStruct((grp.t, MIX_W), f32),
        compiler_params=_params(("parallel", "parallel")),
        name="lat_attn",
    )(sink, z, z, z, z, z, z, z, cos, sin, cos, sin, cos, sin, cache_k, cache_v)


def _rope_tables(s):
    t = jnp.arange(s)
    freqs = ROPE_BASE ** (-jnp.arange(ROPE_FREQS, dtype=f32) / ROPE_FREQS)
    a_row = (t // GRID_W).astype(f32)[:, None] * freqs[None, :]
    a_col = (t % GRID_W).astype(f32)[:, None] * freqs[None, :]
    cos = jnp.concatenate([jnp.cos(a_row)] * 2 + [jnp.cos(a_col)] * 2, axis=1)
    sin = jnp.concatenate([-jnp.sin(a_row), jnp.sin(a_row), -jnp.sin(a_col), jnp.sin(a_col)], axis=1)
    return jnp.concatenate([cos] * A_KV, axis=1), jnp.concatenate([sin] * A_KV, axis=1)


def _merge_kernel(x_ref, mod_ref, g1_ref, g2_ref, hf_ref, hb_ref, om_ref, uc_ref, vc_ref, ya_ref,
                  gm_ref, ws_ref, bs_ref, wmerge_ref, bmerge_ref, wbr_ref, wout_ref, wq_ref, keys_ref,
                  x1_ref, h2t_ref, st_ref):
    x = x_ref[...]
    tm = x.shape[0]
    mod = mod_ref[0]
    h = _mod_norm(x, g1_ref[...], mod[1:2], mod[0:1])
    hm = hf_ref[...] + hb_ref[...]
    ym = jnp.concatenate(
        [_rms(hm[:, hh * M_DH:(hh + 1) * M_DH]) * gm_ref[:, hh * M_DH:(hh + 1) * M_DH] for hh in range(M_HEADS)],
        axis=1) * jax.nn.sigmoid(om_ref[...])
    vr = _rms(vc_ref[...])
    zc = []
    for n in range(tm // CHUNK):
        rs = slice(n * CHUNK, (n + 1) * CHUNK)
        zc.append(jnp.concatenate(
            [_mm(ws_ref[g], vr[rs, g * LANES:(g + 1) * LANES]) + _col(bs_ref[...], g) for g in range(C_GROUPS)],
            axis=1))
    yc = uc_ref[...] * jnp.concatenate(zc, axis=0)
    mixed = jnp.zeros((tm, D_MODEL), f32)
    for n, y in enumerate((ym, yc, ya_ref[...])):
        gate = jax.nn.sigmoid(_mm(h, wmerge_ref[:, n * D_MODEL:(n + 1) * D_MODEL])
                              + bmerge_ref[:, n * D_MODEL:(n + 1) * D_MODEL])
        mixed = mixed + gate * _mm(y, wbr_ref[n])
    x1 = x + mod[2:3] * _mm(mixed, wout_ref[...])
    x1_ref[...] = x1
    h2 = _mod_norm(x1, g2_ref[...], mod[4:5], mod[3:4])
    h2t_ref[...] = h2.T.astype(h2t_ref.dtype)
    qp = _mm(h2, wq_ref[...])
    for hp in range(2 * PEER_HEADS):
        st_ref[hp] = _mm_nt(keys_ref[hp], qp[:, hp * N_KEYS:(hp + 1) * N_KEYS])


def _merge(grp, x, mod_l, g1, g2, hf, hb, z, ya, gm, ws, bs_t, wmerge, bmerge, wbr, wout, wq, keys):
    tm = 256
    row = grp.mod_row(tm)
    tok = lambda w, col=0: pl.BlockSpec((tm, w), lambda i: (i, col))
    full = lambda a: pl.BlockSpec(a.shape, lambda i: (0,) * a.ndim, pipeline_mode=pl.Buffered(1))
    return pl.pallas_call(
        _merge_kernel,
        grid=(grp.t // tm,),
        in_specs=[tok(D_MODEL), pl.BlockSpec((1, 6, D_MODEL), lambda i: (row(i), 0, 0)), full(g1), full(g2),
                  tok(MIX_W), tok(MIX_W), tok(MIX_W, Z_OM // MIX_W), tok(MIX_W, Z_UC // MIX_W),
                  tok(MIX_W, Z_VC // MIX_W), tok(MIX_W),
                  full(gm), full(ws), full(bs_t), full(wmerge), full(bmerge), full(wbr), full(wout), full(wq),
                  full(keys)],
        out_specs=[tok(D_MODEL), pl.BlockSpec((D_MODEL, tm), lambda i: (0, i)),
                   pl.BlockSpec((2 * PEER_HEADS, N_KEYS, tm), lambda i: (0, 0, i))],
        out_shape=[jax.ShapeDtypeStruct((grp.t, D_MODEL), f32), jax.ShapeDtypeStruct((D_MODEL, grp.t), MXU_DTYPE),
                   jax.ShapeDtypeStruct((2 * PEER_HEADS, N_KEYS, grp.t), f32)],
        compiler_params=_params(("parallel",)),
        name="merge",
    )(x, mod_l, g1, g2, hf, hb, z, z, z, ya, gm, ws, bs_t, wmerge, bmerge, wbr, wout, wq, keys)


_CELLS = [(i, j) for i in range(PEER_TOPK) for j in range(PEER_TOPK) if (i + 1) * (j + 1) <= PEER_TOPK]


def _route_kernel(st_ref, r2_ref, b_ref, jd_ref, a_ref, val_s, rank_s, work_s):
    tb = st_ref.shape[2]
    nhp = 2 * PEER_HEADS
    unranked = jnp.full((N_KEYS, tb), float(PEER_TOPK), f32)

    def start():
        work_s[...] = st_ref[...]
        for hp in range(nhp):
            rank_s[hp] = unranked

    def extract(hp, r, rf, hit_of):
        s = work_s[hp]
        m = jnp.max(s, axis=0, keepdims=True)
        hit = hit_of(s, m)
        val_s[hp % 2, r, pl.ds(hp // 2, 1), :] = m
        rank_s[hp] = jnp.where(hit, rf, rank_s[hp])
        work_s[hp] = jnp.where(hit, -jnp.inf, s)

    start()

    def fast_round(r, _):
        rf = lax.convert_element_type(r, f32)
        for hp in range(nhp):
            extract(hp, r, rf, lambda s, m: s == m)
        return 0

    lax.fori_loop(0, PEER_TOPK, fast_round, 0)
    removed = jnp.zeros((1, tb), f32)
    for hp in range(nhp):
        removed = jnp.maximum(removed, jnp.sum(jnp.where(work_s[hp] == -jnp.inf, 1.0, 0.0), axis=0, keepdims=True))

    @pl.when(jnp.max(removed) > float(PEER_TOPK))
    def _():
        key = lax.broadcasted_iota(jnp.int32, (N_KEYS, tb), 0).astype(f32)

        def lowest_index_hit(s, m):
            return key == jnp.min(jnp.where(s == m, key, float(N_KEYS)), axis=0, keepdims=True)

        start()

        def exact_round(i, _):
            extract(i // PEER_TOPK, i % PEER_TOPK, lax.convert_element_type(i % PEER_TOPK, f32), lowest_index_hit)
            return 0

        lax.fori_loop(0, nhp * PEER_TOPK, exact_round, 0)

    v1 = [val_s[0, i] for i in range(PEER_TOPK)]
    v2 = [val_s[1, i] for i in range(PEER_TOPK)]
    cand = [v1[i] + v2[j] for (i, j) in _CELLS]
    ncell = len(_CELLS)
    before = [jnp.zeros((PEER_HEADS, tb), f32) for _ in range(ncell)]
    for x in range(ncell):
        for y in range(x + 1, ncell):
            x_first = (cand[x] >= cand[y]).astype(f32)
            before[y] = before[y] + x_first
            before[x] = before[x] + (1.0 - x_first)
    ea = [jnp.exp(v1[i] - v1[0]) for i in range(PEER_TOPK)]
    eb = [jnp.exp(v2[j] - v2[0]) for j in range(PEER_TOPK)]
    jcount = [jnp.zeros((PEER_HEADS, tb), f32) for _ in range(PEER_TOPK)]
    zsum = jnp.zeros((PEER_HEADS, tb), f32)
    for x, (i, j) in enumerate(_CELLS):
        sel = (before[x] < PEER_TOPK).astype(f32)
        jcount[i] = jcount[i] + sel
        zsum = zsum + sel * (ea[i] * eb[j])
    inv_z = 1.0 / zsum

    for h in range(PEER_HEADS):
        s1 = st_ref[2 * h]
        s2 = st_ref[2 * h + 1]
        rank1 = rank_s[2 * h]
        jd = jnp.zeros((N_KEYS, tb), f32)
        for i in range(PEER_TOPK):
            jd = jnp.where(rank1 == float(i), jcount[i][h:h + 1, :], jd)
        jd_ref[h] = _pair_words(jd)
        a_ref[h] = _pair_words(jnp.exp(s1 - v1[0][h:h + 1, :]) * inv_z[h:h + 1, :])
        b_ref[h] = jnp.exp(s2 - v2[0][h:h + 1, :]).astype(b_ref.dtype)
        r2_ref[h] = rank_s[2 * h + 1].astype(r2_ref.dtype)


def _route(st):
    t = st.shape[2]
    tb = LANES
    out = [jax.ShapeDtypeStruct((PEER_HEADS, N_KEYS, t), dt)
           for dt in (GATE_DTYPE, GATE_DTYPE, jnp.uint32, jnp.uint32)]
    spec = pl.BlockSpec((PEER_HEADS, N_KEYS, tb), lambda i: (0, 0, i))
    return pl.pallas_call(
        _route_kernel,
        grid=(t // tb,),
        in_specs=[pl.BlockSpec((2 * PEER_HEADS, N_KEYS, tb), lambda i: (0, 0, i))],
        out_specs=[spec] * 4,
        out_shape=out,
        scratch_shapes=[pltpu.VMEM((2, PEER_TOPK, PEER_HEADS, tb), f32),
                        pltpu.VMEM((2 * PEER_HEADS, N_KEYS, tb), f32),
                        pltpu.VMEM((2 * PEER_HEADS, N_KEYS, tb), f32)],
        compiler_params=_params(("parallel",)),
        name="peer_route",
    )(st)


PEER_TT = 512
PEER_TW = 512
PEER_E1 = 16
PEER_SUBS = (4, 4, 4, 4)
PEER_ROWS = 16


def _pair_words(x):
    u = lax.bitcast_convert_type(x.astype(jnp.bfloat16).astype(f32), jnp.uint32)
    return u | (u >> 16)


def _bcast_pair_row(row):
    assert PEER_ROWS == 16
    return pltpu.bitcast(jnp.broadcast_to(row, (PEER_ROWS // 2, row.shape[1])), jnp.bfloat16)


def _peer_kernel(final, h2t_ref, r2_ref, b_ref, jd_ref, a_ref, u_ref, vt_ref, x1_ref, mod_ref, gf_ref,
                 o_ref, acc_s, a_s, h_s):
    j = pl.program_id(1)

    @pl.when(j == 0)
    def _():
        acc_s[...] = jnp.zeros_like(acc_s)

    assert sum(PEER_SUBS) == PEER_E1
    starts = [sum(PEER_SUBS[:sb]) for sb in range(len(PEER_SUBS))]
    n_sub = len(PEER_SUBS)
    srow = lambda sb: slice(starts[sb] * N_KEYS, (starts[sb] + PEER_SUBS[sb]) * N_KEYS)
    units = [(sb, pl.ds(w * PEER_TW, PEER_TW)) for w in range(PEER_TT // PEER_TW) for sb in range(n_sub)]

    def hdot(i):
        sb, cols = units[i]
        h_s[i % 2, :PEER_SUBS[sb] * N_KEYS, :] = jnp.dot(u_ref[srow(sb), :], h2t_ref[:, cols],
                                                        preferred_element_type=f32).astype(h_s.dtype)

    def accumulate(i):
        sb, cols = units[i]
        acc_s[:, cols] += jnp.dot(vt_ref[:, srow(sb)], a_s[srow(sb), cols], preferred_element_type=f32)

    hdot(0)
    for i, (sb, cols) in enumerate(units):
        h = h_s.at[i % 2]
        if i + 1 < len(units):
            hdot(i + 1)
        if i >= 1:
            accumulate(i - 1)
        for e in range(PEER_SUBS[sb]):
            e1 = starts[sb] + e
            n_chunk = N_KEYS // PEER_ROWS
            gates = [None] * n_chunk
            for hh in range(PEER_HEADS):
                jd = _bcast_pair_row(jd_ref[hh, e1:e1 + 1, cols])
                aa = _bcast_pair_row(a_ref[hh, e1:e1 + 1, cols])
                for r in range(n_chunk):
                    rows = slice(r * PEER_ROWS, (r + 1) * PEER_ROWS)
                    term = jnp.where(r2_ref[hh, rows, cols] < jd, b_ref[hh, rows, cols],
                                     jnp.zeros((), GATE_DTYPE)) * aa
                    gates[r] = term if gates[r] is None else gates[r] + term
            for r in range(n_chunk):
                lo = e * N_KEYS + r * PEER_ROWS
                act = gates[r] * jax.nn.gelu(h[lo:lo + PEER_ROWS, :])
                a_s[pl.ds(e1 * N_KEYS + r * PEER_ROWS, PEER_ROWS), cols] = act.astype(a_s.dtype)
    accumulate(len(units) - 1)

    @pl.when(j == pl.num_programs(1) - 1)
    def _():
        x2 = x1_ref[...] + mod_ref[0, 5:6, :] * acc_s[...].T
        o_ref[...] = _rms(x2) * gf_ref[...] if final else x2


def _tables_kernel(u_ref, v_ref, ub_ref, vtb_ref):
    ub_ref[0] = u_ref[0].astype(ub_ref.dtype)
    vtb_ref[0] = v_ref[0].T.astype(vtb_ref.dtype)


def _prep_tables(peer_u, peer_v):
    rows = 512
    src = pl.BlockSpec((1, rows, D_MODEL), lambda l, i: (l, i, 0))
    return pl.pallas_call(
        _tables_kernel,
        grid=(DEPTH, N_EXPERTS // rows),
        in_specs=[src, src],
        out_specs=[src, pl.BlockSpec((1, D_MODEL, rows), lambda l, i: (l, 0, i))],
        out_shape=[jax.ShapeDtypeStruct((DEPTH, N_EXPERTS, D_MODEL), MXU_DTYPE),
                   jax.ShapeDtypeStruct((DEPTH, D_MODEL, N_EXPERTS), MXU_DTYPE)],
        compiler_params=_params(("parallel", "parallel")),
        name="peer_tables",
    )(peer_u, peer_v)


def _peer(grp, h2t, r2, b, jd, a, u, vt, l, x1, mod_l, g_final, final):
    tt = PEER_TT
    eb = PEER_E1 * N_KEYS
    row = grp.mod_row(tt)
    gate = pl.BlockSpec((PEER_HEADS, N_KEYS, tt), lambda i, j: (0, 0, i))
    gate1 = pl.BlockSpec((PEER_HEADS, PEER_E1, tt), lambda i, j: (0, j, i))
    return pl.pallas_call(
        functools.partial(_peer_kernel, final),
        grid=(grp.t // tt, N_EXPERTS // eb),
        in_specs=[pl.BlockSpec((D_MODEL, tt), lambda i, j: (0, i)), gate, gate, gate1, gate1,
                  pl.BlockSpec((None, eb, D_MODEL), lambda i, j: (l, j, 0)),
                  pl.BlockSpec((None, D_MODEL, eb), lambda i, j: (l, 0, j)),
                  pl.BlockSpec((tt, D_MODEL), lambda i, j: (i, 0)),
                  pl.BlockSpec((1, 6, D_MODEL), lambda i, j: (row(i), 0, 0)),
                  pl.BlockSpec((1, D_MODEL), lambda i, j: (0, 0))],
        out_specs=pl.BlockSpec((tt, D_MODEL), lambda i, j: (i, 0)),
        out_shape=jax.ShapeDtypeStruct((grp.t, D_MODEL), f32),
        scratch_shapes=[pltpu.VMEM((D_MODEL, tt), f32), pltpu.VMEM((eb, tt), MXU_DTYPE),
                        pltpu.VMEM((2, max(PEER_SUBS) * N_KEYS, PEER_TW), GATE_DTYPE)],
        compiler_params=_params(("parallel", "arbitrary")),
        name="peer",
    )(h2t, r2, b, jd, a, u, vt, x1, mod_l, g_final)


def _reorder_w_in(w_in_l):
    ng = 4 * M_HEADS
    a = w_in_l[:, :4 * MIX_W]
    g = w_in_l[:, 4 * MIX_W:4 * MIX_W + ng]
    rest = w_in_l[:, 4 * MIX_W + ng:]
    pad = jnp.zeros((D_MODEL, LANES - ng), w_in_l.dtype)
    return jnp.concatenate([a, rest, g, pad], axis=1).astype(MXU_DTYPE)


def _forward(x_prompt, x_sample, cache_k, cache_v, state_C, state_n, state_m, c, c_ctx,
             w_ada, b_ada, g_norm1, g_norm2, w_in, b_gates_m, g_mlstm, w_spatial, b_spatial, sink,
             w_branch, w_merge, b_merge, w_out, w_peer_q, peer_keys, peer_u, peer_v, g_final):
    nb_c, s_c, _ = x_prompt.shape
    nb_l, s_l, _ = x_sample.shape
    ctx = _Group(nb_c, s_c, 0, False)
    lat = _Group(nb_l, s_l, 1, True)
    lc = cache_k.shape[2]
    nh = 2 * M_HEADS

    cond = jnp.concatenate([c_ctx[None, :], c], axis=0)
    cond = jnp.pad(cond, ((0, (-cond.shape[0]) % 8), (0, 0)))
    mod = _ada(cond, w_ada, b_ada).reshape(DEPTH, cond.shape[0], 6, D_MODEL)
    xc = x_prompt.reshape(ctx.t, D_MODEL)
    xl = x_sample.reshape(lat.t, D_MODEL)
    cos, sin = _rope_tables(s_l)
    ck = cache_k.reshape(nb_l, DEPTH, lc, A_KVW)
    cv = cache_v.reshape(nb_l, DEPTH, lc, A_KVW)
    zero_c = jnp.zeros((nb_c, nh, M_DH, M_DH), f32)
    zero_n = jnp.zeros((nb_c, nh, M_DH), f32)
    zero_m = jnp.zeros((nb_c, nh, LANES), f32)
    gfin = g_final.reshape(1, D_MODEL)
    u, vt = _prep_tables(peer_u, peer_v)

    ks, vs, cs, ns, ms = [], [], [], [], []
    for l in range(DEPTH):
        g1 = g_norm1[l].reshape(1, D_MODEL)
        g2 = g_norm2[l].reshape(1, D_MODEL)
        w_in_l = _reorder_w_in(w_in[l])
        bias_row = jnp.pad(b_gates_m[l].reshape(1, 4 * M_HEADS), ((0, 0), (0, LANES - 4 * M_HEADS)))
        merge_w = (g_mlstm[l].reshape(1, MIX_W), w_spatial[l].astype(MXU_DTYPE),
                   jnp.pad(b_spatial[l].T, ((0, 0), (0, LANES - C_GROUPS))), w_merge[l].astype(MXU_DTYPE),
                   b_merge[l].reshape(1, 3 * D_MODEL), w_branch[l].astype(MXU_DTYPE), w_out[l].astype(MXU_DTYPE),
                   w_peer_q[l].astype(MXU_DTYPE),
                   peer_keys[l].reshape(2 * PEER_HEADS, N_KEYS, N_KEYS).astype(MXU_DTYPE))
        final = l == DEPTH - 1

        zc, k_new, v_new = _inproj(ctx, xc, mod[l], g1, w_in_l, True)
        hf, hb, c_new, n_new, m_new = _mlstm(zc, bias_row, zero_c, zero_n, zero_m, nb_c, s_c)
        ya = _ctx_attn(ctx, zc, sink, l)
        x1, h2t, st = _merge(ctx, xc, mod[l], g1, g2, hf, hb, zc, ya, *merge_w)
        xc = _peer(ctx, h2t, *_route(st), u, vt, l, x1, mod[l], gfin, final)
        ks.append(k_new.reshape(nb_c, s_c, A_KV, A_DH))
        vs.append(v_new.reshape(nb_c, s_c, A_KV, A_DH))
        cs.append(c_new.reshape(nb_c, 2, M_HEADS, M_DH, M_DH))
        ns.append(n_new.reshape(nb_c, 2, M_HEADS, M_DH))
        ms.append(m_new[:, :, 0].reshape(nb_c, 2, M_HEADS))

        zl = _inproj(lat, xl, mod[l], g1, w_in_l, False)[0]
        m0 = jnp.broadcast_to(state_m[:, l].reshape(nb_l, nh, 1), (nb_l, nh, LANES))
        hf, hb, _, _, _ = _mlstm(zl, bias_row, state_C[:, l].reshape(nb_l, nh, M_DH, M_DH),
                                 state_n[:, l].reshape(nb_l, nh, M_DH), m0, nb_l, s_l)
        ya = _lat_attn(lat, zl, sink, l, ck, cv, cos, sin)
        x1, h2t, st = _merge(lat, xl, mod[l], g1, g2, hf, hb, zl, ya, *merge_w)
        xl = _peer(lat, h2t, *_route(st), u, vt, l, x1, mod[l], gfin, final)
    return (xc.reshape(nb_c, s_c, D_MODEL), xl.reshape(nb_l, s_l, D_MODEL),
            jnp.stack(ks, axis=1), jnp.stack(vs, axis=1), jnp.stack(cs, axis=1), jnp.stack(ns, axis=1),
            jnp.stack(ms, axis=1))


def kernel(x_prompt, x_sample, cache_k, cache_v, state_C, state_n, state_m, c, c_ctx, w_ada, b_ada, g_norm1, g_norm2, w_in, b_gates_m, g_mlstm, w_spatial, b_spatial, sink, w_branch, w_merge, b_merge, w_out, w_peer_q, peer_keys, peer_u, peer_v, g_final):
    return _forward(x_prompt, x_sample, cache_k, cache_v, state_C, state_n, state_m, c, c_ctx, w_ada, b_ada,
                    g_norm1, g_norm2, w_in, b_gates_m, g_mlstm, w_spatial, b_spatial, sink, w_branch, w_merge,
                    b_merge, w_out, w_peer_q, peer_keys, peer_u, peer_v, g_final)
```

```python
import functools

import numpy as np
import jax
import jax.numpy as jnp
from jax import lax
from jax.experimental import pallas as pl
from jax.experimental.pallas import tpu as pltpu

D_MODEL = 1024
DEPTH = 2
GRID_W = 64
EPS = 1e-6
NEG_INF = -1e30
MIX_W = D_MODEL // 2
M_HEADS = 4
M_DH = MIX_W // M_HEADS
CHUNK = 128
C_GROUPS = 4
A_HEADS = 8
A_KV = 2
A_GROUP = A_HEADS // A_KV
A_DH = MIX_W // A_HEADS
A_KVW = A_KV * A_DH
ROPE_BASE = 10000.0
ROPE_FREQS = A_DH // 4
N_KEYS = 128
N_EXPERTS = N_KEYS * N_KEYS
PEER_HEADS = 8
PEER_TOPK = 16
PEER_QW = 2 * PEER_HEADS * N_KEYS

LANES = 128
MXU_DTYPE = jnp.bfloat16
GATE_DTYPE = jnp.bfloat16
VMEM_LIMIT = 56 * 1024 * 1024

Z_QM, Z_KM, Z_VM, Z_OM, Z_UC, Z_VC, Z_QA = (i * MIX_W for i in range(7))
Z_KA = 7 * MIX_W
Z_VA = Z_KA + A_KVW
Z_GM = Z_VA + A_KVW
Z_W = Z_GM + LANES

f32 = jnp.float32


def _params(sem):
    return pltpu.CompilerParams(dimension_semantics=sem, vmem_limit_bytes=VMEM_LIMIT)


def _mm(a, b):
    return jnp.dot(a.astype(MXU_DTYPE), b.astype(MXU_DTYPE), preferred_element_type=f32)


def _mm_nt(a, b):
    return lax.dot_general(a.astype(MXU_DTYPE), b.astype(MXU_DTYPE), (((1,), (1,)), ((), ())),
                           preferred_element_type=f32)


def _split3(x):
    hi = x.astype(jnp.bfloat16)
    r1 = x - hi.astype(f32)
    mid = r1.astype(jnp.bfloat16)
    lo = (r1 - mid.astype(f32)).astype(jnp.bfloat16)
    return hi, mid, lo


def _mm_exact_lhs(a01, x):
    a = a01.astype(jnp.bfloat16)
    hi, mid, lo = _split3(x)
    return (jnp.dot(a, hi, preferred_element_type=f32) + jnp.dot(a, mid, preferred_element_type=f32)
            + jnp.dot(a, lo, preferred_element_type=f32))


def _mm_exact_rhs(x, b01):
    b = b01.astype(jnp.bfloat16)
    hi, mid, lo = _split3(x)
    return (jnp.dot(hi, b, preferred_element_type=f32) + jnp.dot(mid, b, preferred_element_type=f32)
            + jnp.dot(lo, b, preferred_element_type=f32))


def _mm3(a, b):
    ah, am, al = _split3(a)
    bh, bm, bl = _split3(b)
    d = functools.partial(jnp.dot, preferred_element_type=f32)
    return (d(ah, bh) + (d(ah, bm) + d(am, bh)) + (d(ah, bl) + d(al, bh) + d(am, bm)))


def _rms(x):
    return x * lax.rsqrt(jnp.mean(x * x, axis=-1, keepdims=True) + EPS)


def _mod_norm(x, g, scale, shift):
    return _rms(x) * g * (1.0 + scale) + shift


def _ada_kernel(c_ref, w_ref, b_ref, o_ref):
    c = c_ref[...]
    o_ref[0] = _mm3(c * jax.nn.sigmoid(c), w_ref[0]) + b_ref[0]


def _ada(cond, w_ada, b_ada):
    rows = cond.shape[0]
    tn = 1536
    return pl.pallas_call(
        _ada_kernel,
        grid=(DEPTH, 6 * D_MODEL // tn),
        in_specs=[pl.BlockSpec((rows, D_MODEL), lambda l, j: (0, 0)),
                  pl.BlockSpec((1, D_MODEL, tn), lambda l, j: (l, 0, j)),
                  pl.BlockSpec((1, 1, tn), lambda l, j: (l, 0, j))],
        out_specs=pl.BlockSpec((1, rows, tn), lambda l, j: (l, 0, j)),
        out_shape=jax.ShapeDtypeStruct((DEPTH, rows, 6 * D_MODEL), f32),
        compiler_params=_params(("parallel", "parallel")),
        name="ada",
    )(cond, w_ada, b_ada.reshape(DEPTH, 1, 6 * D_MODEL))


class _Group:
    def __init__(self, nb, s, cond0, per_sequence):
        self.nb, self.s, self.cond0, self.per_sequence = nb, s, cond0, per_sequence
        self.t = nb * s

    def mod_row(self, tm):
        assert self.t % tm == 0
        if not self.per_sequence:
            return lambda i: self.cond0
        assert self.s % tm == 0
        return lambda i: self.cond0 + i // (self.s // tm)


def _inproj_kernel(x_ref, mod_ref, g1_ref, w_ref, z_ref, *kv_refs):
    h = _mod_norm(x_ref[...], g1_ref[...], mod_ref[0, 1:2, :], mod_ref[0, 0:1, :])
    z = _mm(h, w_ref[...])
    z_ref[...] = z
    if kv_refs:
        kv_refs[0][...] = z[:, Z_KA:Z_KA + A_KVW]
        kv_refs[1][...] = z[:, Z_VA:Z_VA + A_KVW]


def _inproj(grp, x, mod_l, g1, w, emit_kv):
    tm = 256
    row = grp.mod_row(tm)
    kv_spec = [pl.BlockSpec((tm, A_KVW), lambda i: (i, 0))] * 2 if emit_kv else []
    kv_shape = [jax.ShapeDtypeStruct((grp.t, A_KVW), f32)] * 2 if emit_kv else []
    return pl.pallas_call(
        _inproj_kernel,
        grid=(grp.t // tm,),
        in_specs=[pl.BlockSpec((tm, D_MODEL), lambda i: (i, 0)),
                  pl.BlockSpec((1, 6, D_MODEL), lambda i: (row(i), 0, 0)),
                  pl.BlockSpec((1, D_MODEL), lambda i: (0, 0)),
                  pl.BlockSpec((D_MODEL, Z_W), lambda i: (0, 0), pipeline_mode=pl.Buffered(1))],
        out_specs=[pl.BlockSpec((tm, Z_W), lambda i: (i, 0))] + kv_spec,
        out_shape=[jax.ShapeDtypeStruct((grp.t, Z_W), f32)] + kv_shape,
        compiler_params=_params(("parallel",)),
        name="inproj",
    )(x, mod_l, g1, w)


def _log_sigmoid(x):
    return jnp.minimum(x, 0.0) - jnp.log1p(jnp.exp(-jnp.abs(x)))


def _col(a, j):
    lane = lax.broadcasted_iota(jnp.int32, a.shape, 1)
    return jnp.sum(jnp.where(lane == j, a, 0.0), axis=1, keepdims=True)


def _mlstm_kernel(qf_ref, kf_ref, vf_ref, gf_ref, qb_ref, kb_ref, vb_ref, gb_ref, bias_ref,
                  c0_ref, n0_ref, m0_ref, hf_ref, hb_ref, c_out, n_out, m_out, ct_s, n_s, m_s):
    c = pl.program_id(1)
    nseq = qf_ref.shape[0]
    nh = 2 * M_HEADS

    @pl.when(c == 0)
    def _():
        for p in range(nseq):
            for i in range(nh):
                ct_s[p * nh + i] = c0_ref[p, i].T
            n_s[p * nh:(p + 1) * nh, :] = n0_ref[p]
            m_s[p * nh:(p + 1) * nh, :] = m0_ref[p]

    ri = lax.broadcasted_iota(jnp.int32, (CHUNK, CHUNK), 0)
    ci = lax.broadcasted_iota(jnp.int32, (CHUNK, CHUNK), 1)
    lane = lax.broadcasted_iota(jnp.int32, (CHUNK, LANES), 1)
    is_forget = ((lane // M_HEADS) % 2) == 1

    refs = ((qf_ref, kf_ref, vf_ref, gf_ref, hf_ref), (qb_ref, kb_ref, vb_ref, gb_ref, hb_ref))
    keeps = (ri >= ci, ri <= ci)
    krow = lax.broadcasted_iota(jnp.int32, (LANES, M_HEADS * LANES), 0)
    head = lax.broadcasted_iota(jnp.int32, (LANES, M_HEADS * LANES), 1) // LANES
    gates = {}
    for p in range(nseq):
        for d in range(2):
            g = refs[d][3][p] + bias_ref[...]
            lg = jnp.where(is_forget, _log_sigmoid(g), g)
            bc = _mm_exact_lhs(keeps[d], lg)
            li_tiles = _mm_exact_rhs(lg, krow == 2 * d * M_HEADS + head)
            lf_tiles = _mm_exact_rhs(bc, krow == (2 * d + 1) * M_HEADS + head)
            gates[p, d] = (lg.T, bc.T, li_tiles, lf_tiles)

    chains = [(p, d, hh) for p in range(nseq) for d in range(2) for hh in range(M_HEADS)]
    st = {}
    for ch in chains:
        p, d, hh = ch
        lgt, bct, li_tiles, lf_tiles = gates[p, d]
        idx = p * nh + d * M_HEADS + hh
        j_li = 2 * d * M_HEADS + hh
        j_lf = j_li + M_HEADS
        sl = slice(hh * M_DH, (hh + 1) * M_DH)
        q = refs[d][0][p, :, sl]
        k = refs[d][1][p, :, sl] * (M_DH ** -0.5)
        v = refs[d][2][p, :, sl]
        b_t = lf_tiles[:, hh * LANES:(hh + 1) * LANES]
        li_t = li_tiles[:, hh * LANES:(hh + 1) * LANES]
        a_t = b_t + m_s[idx:idx + 1, :]
        dm = jnp.where(keeps[d], b_t - bct[j_lf:j_lf + 1, :] + lgt[j_li:j_li + 1, :], -jnp.inf)
        m_t = jnp.maximum(a_t, jnp.max(dm, axis=1, keepdims=True))
        st[ch] = dict(idx=idx, sl=sl, q=q, k=k, v=v, kt=k.T, b_t=b_t, li_t=li_t, m_t=m_t,
                      w=jnp.exp(dm - m_t), w0=jnp.exp(a_t - m_t))
    for ch in chains:
        c_ = st[ch]
        c_["s"] = c_["w"] * _mm(c_["q"], c_["kt"])
        c_["ct"] = ct_s[c_["idx"]]
        c_["qc"] = _mm(c_["q"], c_["ct"])
        c_["n_row"] = n_s[c_["idx"]:c_["idx"] + 1, :]
        c_["qn"] = _mm_nt(c_["q"], jnp.broadcast_to(c_["n_row"], (CHUNK, M_DH)))
    for ch in chains:
        p, d, hh = ch
        c_ = st[ch]
        num = c_["w0"] * c_["qc"] + _mm(c_["s"], c_["v"])
        den = c_["w0"] * c_["qn"] + jnp.sum(c_["s"], axis=1, keepdims=True)
        refs[d][4][p, :, c_["sl"]] = num / jnp.maximum(jnp.abs(den), jnp.exp(-c_["m_t"]))
    for ch in chains:
        p, d, hh = ch
        c_ = st[ch]
        idx = c_["idx"]
        te = 0 if d == 1 else CHUNK - 1
        m_end = c_["m_t"][te:te + 1, :]
        w0_end = c_["w0"][te:te + 1, :]
        w_end = jnp.exp(c_["b_t"][te:te + 1, :] - c_["b_t"] + c_["li_t"] - m_end)
        ct_s[idx] = w0_end * c_["ct"] + _mm(c_["kt"], c_["v"] * w_end)
        n_s[idx:idx + 1, :] = w0_end * c_["n_row"] + jnp.sum(c_["k"] * w_end, axis=0, keepdims=True)
        m_s[idx:idx + 1, :] = m_end

    @pl.when(c == pl.num_programs(1) - 1)
    def _():
        for p in range(nseq):
            for i in range(nh):
                c_out[p, i] = ct_s[p * nh + i].T
            n_out[p] = n_s[p * nh:(p + 1) * nh, :]
            m_out[p] = m_s[p * nh:(p + 1) * nh, :]


MLSTM_SEQS = 2


def _mlstm(z, bias_row, c0, n0, m0, nb, s):
    nc = s // CHUNK
    nh = 2 * M_HEADS
    ps = MLSTM_SEQS
    assert nb % ps == 0
    z3 = z.reshape(nb, s, Z_W)
    fwd = lambda col: (lambda b, c: (b, c, col))
    bwd = lambda col: (lambda b, c: (b, nc - 1 - c, col))
    wide = lambda im: pl.BlockSpec((ps, CHUNK, MIX_W), im)
    gate = lambda im: pl.BlockSpec((ps, CHUNK, LANES), im)
    state = lambda shape: pl.BlockSpec((ps,) + shape, lambda b, c: (b,) + (0,) * len(shape))
    gcol = Z_GM // LANES
    hf, hb, c_new, n_new, m_new = pl.pallas_call(
        _mlstm_kernel,
        grid=(nb // ps, nc),
        in_specs=[wide(fwd(0)), wide(fwd(1)), wide(fwd(2)), gate(fwd(gcol)),
                  wide(bwd(0)), wide(bwd(1)), wide(bwd(2)), gate(bwd(gcol)),
                  pl.BlockSpec((1, LANES), lambda b, c: (0, 0)),
                  state((nh, M_DH, M_DH)), state((nh, M_DH)), state((nh, LANES))],
        out_specs=[wide(fwd(0)), wide(bwd(0)),
                   state((nh, M_DH, M_DH)), state((nh, M_DH)), state((nh, LANES))],
        out_shape=[jax.ShapeDtypeStruct((nb, s, MIX_W), f32), jax.ShapeDtypeStruct((nb, s, MIX_W), f32),
                   jax.ShapeDtypeStruct((nb, nh, M_DH, M_DH), f32), jax.ShapeDtypeStruct((nb, nh, M_DH), f32),
                   jax.ShapeDtypeStruct((nb, nh, LANES), f32)],
        scratch_shapes=[pltpu.VMEM((ps * nh, M_DH, M_DH), f32), pltpu.VMEM((ps * nh, M_DH), f32),
                        pltpu.VMEM((ps * nh, LANES), f32)],
        compiler_params=_params(("parallel", "arbitrary")),
        name="mlstm",
    )(z3, z3, z3, z3, z3, z3, z3, z3, bias_row, c0, n0, m0)
    return hf.reshape(nb * s, MIX_W), hb.reshape(nb * s, MIX_W), c_new, n_new, m_new


def _half_placements(a):
    lane = lax.broadcasted_iota(jnp.int32, a.shape, 1)
    g0 = jnp.where(lane < A_DH, a, 0.0)
    g1 = jnp.where(lane >= A_DH, a, 0.0)
    return ((g0, pltpu.roll(g0, A_DH, 1)), (pltpu.roll(g1, A_DH, 1), g1))


def _group_attend(q, k_all, v_all, sink_ref, l, keep):
    kz = _half_placements(k_all)
    vz = _half_placements(v_all)
    heads = [(head, head // 2, head % 2, head // A_GROUP) for head in range(A_HEADS)]
    scores = [_mm_nt(q[:, slab * LANES:(slab + 1) * LANES], kz[g][pos]) * (A_DH ** -0.5)
              for _, slab, pos, g in heads]
    if keep is not None:
        scores = [jnp.where(keep, s, NEG_INF) for s in scores]
    sinks = [sink_ref[l, head] for head, _, _, _ in heads]
    tops = [jnp.maximum(jnp.max(s, axis=1, keepdims=True), sk) for s, sk in zip(scores, sinks)]
    probs = [jnp.exp(s - m) for s, m in zip(scores, tops)]
    dens = [jnp.sum(p, axis=1, keepdims=True) + jnp.exp(sk - m) for p, sk, m in zip(probs, sinks, tops)]
    outs = [_mm(p, vz[g][pos]) / den for p, den, (_, _, pos, g) in zip(probs, dens, heads)]
    return jnp.concatenate([outs[2 * slab] + outs[2 * slab + 1] for slab in range(A_HEADS // 2)], axis=1)


def _ctx_attn_kernel(l, sink_ref, q_ref, k_ref, v_ref, o_ref):
    o_ref[...] = _group_attend(q_ref[...], k_ref[...], v_ref[...], sink_ref, l, None)


def _ctx_attn(grp, z, sink, l):
    s = grp.s
    return pl.pallas_call(
        functools.partial(_ctx_attn_kernel, l),
        grid=(grp.nb,),
        in_specs=[pl.BlockSpec(memory_space=pltpu.SMEM),
                  pl.BlockSpec((s, MIX_W), lambda b: (b, Z_QA // MIX_W)),
                  pl.BlockSpec((s, A_KVW), lambda b: (b, Z_KA // A_KVW)),
                  pl.BlockSpec((s, A_KVW), lambda b: (b, Z_VA // A_KVW))],
        out_specs=pl.BlockSpec((s, MIX_W), lambda b: (b, 0)),
        out_shape=jax.ShapeDtypeStruct((grp.t, MIX_W), f32),
        compiler_params=_params(("parallel",)),
        name="ctx_attn",
    )(sink, z, z, z)


def _rope(x, cos, sin):
    lane = lax.broadcasted_iota(jnp.int32, cos.shape, 1)
    first = (lane % (2 * ROPE_FREQS)) < ROPE_FREQS
    out = []
    for j in range(x.shape[1] // LANES):
        xs = x[:, j * LANES:(j + 1) * LANES]
        partner = jnp.where(first, pltpu.roll(xs, LANES - ROPE_FREQS, 1), pltpu.roll(xs, ROPE_FREQS, 1))
        out.append(xs * cos + partner * sin)
    return out[0] if len(out) == 1 else jnp.concatenate(out, axis=1)


def _lat_attn_kernel(l, nblk, sink_ref, q_ref, kp_ref, kc_ref, kn_ref, vp_ref, vc_ref, vn_ref,
                     cq_ref, sq_ref, cp_ref, sp_ref, cn_ref, sn_ref, ck_ref, cv_ref, o_ref):
    i = pl.program_id(1)
    q = _rope(q_ref[...], cq_ref[...], sq_ref[...])
    k_all = jnp.concatenate([ck_ref[0, 0],
                             _rope(kp_ref[...], cp_ref[...], sp_ref[...]),
                             _rope(kc_ref[...], cq_ref[...], sq_ref[...]),
                             _rope(kn_ref[...], cn_ref[...], sn_ref[...])], axis=0)
    v_all = jnp.concatenate([cv_ref[0, 0], vp_ref[...], vc_ref[...], vn_ref[...]], axis=0)
    lc = ck_ref.shape[2]
    nk = lc + 3 * CHUNK
    r = lax.broadcasted_iota(jnp.int32, (CHUNK, nk), 0)
    cc = lax.broadcasted_iota(jnp.int32, (CHUNK, nk), 1) - lc
    lo = jnp.maximum(r, jnp.where(i == 0, CHUNK, 0))
    hi = jnp.minimum(r + 2 * CHUNK, jnp.where(i == nblk - 1, 2 * CHUNK - 1, 3 * CHUNK))
    keep = (cc < 0) | ((cc >= lo) & (cc <= hi))
    o_ref[...] = _group_attend(q, k_all, v_all, sink_ref, l, keep)


def _lat_attn(grp, z, sink, l, cache_k, cache_v, cos, sin):
    s, nb = grp.s, grp.nb
    nblk = s // CHUNK
    lc = cache_k.shape[2]

    def rows(off):
        return lambda b, i: b * nblk + jnp.clip(i + off, 0, nblk - 1)

    def zspec(width, col, off):
        rf = rows(off)
        return pl.BlockSpec((CHUNK, width), lambda b, i: (rf(b, i), col))

    def tab(off):
        return pl.BlockSpec((CHUNK, LANES), lambda b, i: (jnp.clip(i + off, 0, nblk - 1), 0))

    cache = pl.BlockSpec((1, 1, lc, A_KVW), lambda b, i: (b, l, 0, 0))
    kcol, vcol = Z_KA // A_KVW, Z_VA // A_KVW
    return pl.pallas_call(
        functools.partial(_lat_attn_kernel, l, nblk),
        grid=(nb, nblk),
        in_specs=[pl.BlockSpec(memory_space=pltpu.SMEM),
                  zspec(MIX_W, Z_QA // MIX_W, 0),
                  zspec(A_KVW, kcol, -1), zspec(A_KVW, kcol, 0), zspec(A_KVW, kcol, 1),
                  zspec(A_KVW, vcol, -1), zspec(A_KVW, vcol, 0), zspec(A_KVW, vcol, 1),
                  tab(0), tab(0), tab(-1), tab(-1), tab(1), tab(1), cache, cache],
        out_specs=pl.BlockSpec((CHUNK, MIX_W), lambda b, i: (b * nblk + i, 0)),
        out_shape=jax.ShapeDtypeStruct((grp.t, MIX_W), f32),
        compiler_params=_params(("parallel", "parallel")),
        name="lat_attn",
    )(sink, z, z, z, z, z, z, z, cos, sin, cos, sin, cos, sin, cache_k, cache_v)


def _rope_tables(s):
    t = jnp.arange(s)
    freqs = ROPE_BASE ** (-jnp.arange(ROPE_FREQS, dtype=f32) / ROPE_FREQS)
    a_row = (t // GRID_W).astype(f32)[:, None] * freqs[None, :]
    a_col = (t % GRID_W).astype(f32)[:, None] * freqs[None, :]
    cos = jnp.concatenate([jnp.cos(a_row)] * 2 + [jnp.cos(a_col)] * 2, axis=1)
    sin = jnp.concatenate([-jnp.sin(a_row), jnp.sin(a_row), -jnp.sin(a_col), jnp.sin(a_col)], axis=1)
    return jnp.concatenate([cos] * A_KV, axis=1), jnp.concatenate([sin] * A_KV, axis=1)


def _merge_kernel(x_ref, mod_ref, g1_ref, g2_ref, hf_ref, hb_ref, om_ref, uc_ref, vc_ref, ya_ref,
                  gm_ref, ws_ref, bs_ref, wmerge_ref, bmerge_ref, wbr_ref, wout_ref, wq_ref, keys_ref,
                  x1_ref, h2t_ref, st_ref):
    x = x_ref[...]
    tm = x.shape[0]
    mod = mod_ref[0]
    h = _mod_norm(x, g1_ref[...], mod[1:2], mod[0:1])
    hm = hf_ref[...] + hb_ref[...]
    ym = jnp.concatenate(
        [_rms(hm[:, hh * M_DH:(hh + 1) * M_DH]) * gm_ref[:, hh * M_DH:(hh + 1) * M_DH] for hh in range(M_HEADS)],
        axis=1) * jax.nn.sigmoid(om_ref[...])
    vr = _rms(vc_ref[...])
    zc = []
    for n in range(tm // CHUNK):
        rs = slice(n * CHUNK, (n + 1) * CHUNK)
        zc.append(jnp.concatenate(
            [_mm(ws_ref[g], vr[rs, g * LANES:(g + 1) * LANES]) + _col(bs_ref[...], g) for g in range(C_GROUPS)],
            axis=1))
    yc = uc_ref[...] * jnp.concatenate(zc, axis=0)
    mixed = jnp.zeros((tm, D_MODEL), f32)
    for n, y in enumerate((ym, yc, ya_ref[...])):
        gate = jax.nn.sigmoid(_mm(h, wmerge_ref[:, n * D_MODEL:(n + 1) * D_MODEL])
                              + bmerge_ref[:, n * D_MODEL:(n + 1) * D_MODEL])
        mixed = mixed + gate * _mm(y, wbr_ref[n])
    x1 = x + mod[2:3] * _mm(mixed, wout_ref[...])
    x1_ref[...] = x1
    h2 = _mod_norm(x1, g2_ref[...], mod[4:5], mod[3:4])
    h2t_ref[...] = h2.T.astype(h2t_ref.dtype)
    qp = _mm(h2, wq_ref[...])
    for hp in range(2 * PEER_HEADS):
        st_ref[hp] = _mm_nt(keys_ref[hp], qp[:, hp * N_KEYS:(hp + 1) * N_KEYS])


def _merge(grp, x, mod_l, g1, g2, hf, hb, z, ya, gm, ws, bs_t, wmerge, bmerge, wbr, wout, wq, keys):
    tm = 256
    row = grp.mod_row(tm)
    tok = lambda w, col=0: pl.BlockSpec((tm, w), lambda i: (i, col))
    full = lambda a: pl.BlockSpec(a.shape, lambda i: (0,) * a.ndim, pipeline_mode=pl.Buffered(1))
    return pl.pallas_call(
        _merge_kernel,
        grid=(grp.t // tm,),
        in_specs=[tok(D_MODEL), pl.BlockSpec((1, 6, D_MODEL), lambda i: (row(i), 0, 0)), full(g1), full(g2),
                  tok(MIX_W), tok(MIX_W), tok(MIX_W, Z_OM // MIX_W), tok(MIX_W, Z_UC // MIX_W),
                  tok(MIX_W, Z_VC // MIX_W), tok(MIX_W),
                  full(gm), full(ws), full(bs_t), full(wmerge), full(bmerge), full(wbr), full(wout), full(wq),
                  full(keys)],
        out_specs=[tok(D_MODEL), pl.BlockSpec((D_MODEL, tm), lambda i: (0, i)),
                   pl.BlockSpec((2 * PEER_HEADS, N_KEYS, tm), lambda i: (0, 0, i))],
        out_shape=[jax.ShapeDtypeStruct((grp.t, D_MODEL), f32), jax.ShapeDtypeStruct((D_MODEL, grp.t), MXU_DTYPE),
                   jax.ShapeDtypeStruct((2 * PEER_HEADS, N_KEYS, grp.t), f32)],
        compiler_params=_params(("parallel",)),
        name="merge",
    )(x, mod_l, g1, g2, hf, hb, z, z, z, ya, gm, ws, bs_t, wmerge, bmerge, wbr, wout, wq, keys)


_CELLS = [(i, j) for i in range(PEER_TOPK) for j in range(PEER_TOPK) if (i + 1) * (j + 1) <= PEER_TOPK]


def _route_kernel(st_ref, r2_ref, b_ref, jd_ref, a_ref, val_s, rank_s, work_s):
    tb = st_ref.shape[2]
    nhp = 2 * PEER_HEADS
    unranked = jnp.full((N_KEYS, tb), float(PEER_TOPK), f32)

    def extract(hp, r, hit_of, rank_value=None):
        s = work_s[hp]
        m = jnp.max(s, axis=0, keepdims=True)
        hit = hit_of(s, m)
        val_s[hp % 2, r, pl.ds(hp // 2, 1), :] = m
        if rank_value is not None:
            rank_s[hp] = jnp.where(hit, rank_value, rank_s[hp])
        work_s[hp] = jnp.where(hit, -jnp.inf, s)

    def finish(ranked):
        v1 = [val_s[0, i] for i in range(PEER_TOPK)]
        v2 = [val_s[1, i] for i in range(PEER_TOPK)]
        cand = [v1[i] + v2[j] for (i, j) in _CELLS]
        ncell = len(_CELLS)
        before = [jnp.zeros((PEER_HEADS, tb), f32) for _ in range(ncell)]
        for x in range(ncell):
            for y in range(x + 1, ncell):
                x_first = (cand[x] >= cand[y]).astype(f32)
                before[y] = before[y] + x_first
                before[x] = before[x] + (1.0 - x_first)
        ea = [jnp.exp(v1[i] - v1[0]) for i in range(PEER_TOPK)]
        eb = [jnp.exp(v2[j] - v2[0]) for j in range(PEER_TOPK)]
        jcount = [jnp.zeros((PEER_HEADS, tb), f32) for _ in range(PEER_TOPK)]
        zsum = jnp.zeros((PEER_HEADS, tb), f32)
        for x, (i, j) in enumerate(_CELLS):
            sel = (before[x] < PEER_TOPK).astype(f32)
            jcount[i] = jcount[i] + sel
            zsum = zsum + sel * (ea[i] * eb[j])
        inv_z = 1.0 / zsum

        for h in range(PEER_HEADS):
            s1 = st_ref[2 * h]
            s2 = st_ref[2 * h + 1]
            jd = jnp.zeros((N_KEYS, tb), f32)
            if ranked:
                rank1 = rank_s[2 * h]
                for i in range(PEER_TOPK):
                    jd = jnp.where(rank1 == float(i), jcount[i][h:h + 1, :], jd)
                r2 = rank_s[2 * h + 1]
            else:
                r2 = jnp.zeros((N_KEYS, tb), f32)
                for i in range(PEER_TOPK):
                    jd = jnp.where(s1 == v1[i][h:h + 1, :], jcount[i][h:h + 1, :], jd)
                    r2 = r2 + jnp.where(v2[i][h:h + 1, :] > s2, 1.0, 0.0)
            jd_ref[h] = _pair_words(jd)
            a_ref[h] = _pair_words(jnp.exp(s1 - v1[0][h:h + 1, :]) * inv_z[h:h + 1, :])
            b_ref[h] = jnp.exp(s2 - v2[0][h:h + 1, :]).astype(b_ref.dtype)
            r2_ref[h] = r2.astype(r2_ref.dtype)

    work_s[...] = st_ref[...]

    def fast_round(r, _):
        for hp in range(nhp):
            extract(hp, r, lambda s, m: s == m)
        return 0

    lax.fori_loop(0, PEER_TOPK, fast_round, 0)
    removed = jnp.zeros((1, tb), f32)
    for hp in range(nhp):
        removed = jnp.maximum(removed, jnp.sum(jnp.where(work_s[hp] == -jnp.inf, 1.0, 0.0), axis=0, keepdims=True))
    tied = jnp.max(removed) > float(PEER_TOPK)

    @pl.when(jnp.logical_not(tied))
    def _():
        finish(ranked=False)

    @pl.when(tied)
    def _():
        key = lax.broadcasted_iota(jnp.int32, (N_KEYS, tb), 0).astype(f32)

        def lowest_index_hit(s, m):
            return key == jnp.min(jnp.where(s == m, key, float(N_KEYS)), axis=0, keepdims=True)

        work_s[...] = st_ref[...]
        for hp in range(nhp):
            rank_s[hp] = unranked

        def exact_round(i, _):
            r = i % PEER_TOPK
            extract(i // PEER_TOPK, r, lowest_index_hit, lax.convert_element_type(r, f32))
            return 0

        lax.fori_loop(0, nhp * PEER_TOPK, exact_round, 0)
        finish(ranked=True)


def _route(st):
    t = st.shape[2]
    tb = LANES
    out = [jax.ShapeDtypeStruct((PEER_HEADS, N_KEYS, t), dt)
           for dt in (GATE_DTYPE, GATE_DTYPE, jnp.uint32, jnp.uint32)]
    spec = pl.BlockSpec((PEER_HEADS, N_KEYS, tb), lambda i: (0, 0, i))
    return pl.pallas_call(
        _route_kernel,
        grid=(t // tb,),
        in_specs=[pl.BlockSpec((2 * PEER_HEADS, N_KEYS, tb), lambda i: (0, 0, i))],
        out_specs=[spec] * 4,
        out_shape=out,
        scratch_shapes=[pltpu.VMEM((2, PEER_TOPK, PEER_HEADS, tb), f32),
                        pltpu.VMEM((2 * PEER_HEADS, N_KEYS, tb), f32),
                        pltpu.VMEM((2 * PEER_HEADS, N_KEYS, tb), f32)],
        compiler_params=_params(("parallel",)),
        name="peer_route",
    )(st)


PEER_TT = 512
PEER_TW = 512
PEER_E1 = 16
PEER_SUBS = (4, 4, 4, 4)
PEER_ROWS = 16


def _pair_words(x):
    u = lax.bitcast_convert_type(x.astype(jnp.bfloat16).astype(f32), jnp.uint32)
    return u | (u >> 16)


def _bcast_pair_row(row):
    assert PEER_ROWS == 16
    return pltpu.bitcast(jnp.broadcast_to(row, (PEER_ROWS // 2, row.shape[1])), jnp.bfloat16)


def _peer_kernel(final, h2t_ref, r2_ref, b_ref, jd_ref, a_ref, u_ref, vt_ref, x1_ref, mod_ref, gf_ref,
                 o_ref, acc_s, a_s, h_s):
    j = pl.program_id(1)

    @pl.when(j == 0)
    def _():
        acc_s[...] = jnp.zeros_like(acc_s)

    assert sum(PEER_SUBS) == PEER_E1
    starts = [sum(PEER_SUBS[:sb]) for sb in range(len(PEER_SUBS))]
    n_sub = len(PEER_SUBS)
    srow = lambda sb: slice(starts[sb] * N_KEYS, (starts[sb] + PEER_SUBS[sb]) * N_KEYS)
    units = [(sb, pl.ds(w * PEER_TW, PEER_TW)) for w in range(PEER_TT // PEER_TW) for sb in range(n_sub)]

    def hdot(i):
        sb, cols = units[i]
        h_s[i % 2, :PEER_SUBS[sb] * N_KEYS, :] = jnp.dot(u_ref[srow(sb), :], h2t_ref[:, cols],
                                                        preferred_element_type=f32).astype(h_s.dtype)

    def accumulate(i):
        sb, cols = units[i]
        acc_s[:, cols] += jnp.dot(vt_ref[:, srow(sb)], a_s[srow(sb), cols], preferred_element_type=f32)

    hdot(0)
    for i, (sb, cols) in enumerate(units):
        h = h_s.at[i % 2]
        if i + 1 < len(units):
            hdot(i + 1)
        if i >= 1:
            accumulate(i - 1)
        for e in range(PEER_SUBS[sb]):
            e1 = starts[sb] + e
            n_chunk = N_KEYS // PEER_ROWS
            gates = [None] * n_chunk
            for hh in range(PEER_HEADS):
                jd = _bcast_pair_row(jd_ref[hh, e1:e1 + 1, cols])
                aa = _bcast_pair_row(a_ref[hh, e1:e1 + 1, cols])
                for r in range(n_chunk):
                    rows = slice(r * PEER_ROWS, (r + 1) * PEER_ROWS)
                    term = jnp.where(r2_ref[hh, rows, cols] < jd, b_ref[hh, rows, cols],
                                     jnp.zeros((), GATE_DTYPE)) * aa
                    gates[r] = term if gates[r] is None else gates[r] + term
            for r in range(n_chunk):
                lo = e * N_KEYS + r * PEER_ROWS
                act = gates[r] * jax.nn.gelu(h[lo:lo + PEER_ROWS, :])
                a_s[pl.ds(e1 * N_KEYS + r * PEER_ROWS, PEER_ROWS), cols] = act.astype(a_s.dtype)
    accumulate(len(units) - 1)

    @pl.when(j == pl.num_programs(1) - 1)
    def _():
        x2 = x1_ref[...] + mod_ref[0, 5:6, :] * acc_s[...].T
        o_ref[...] = _rms(x2) * gf_ref[...] if final else x2


def _tables_kernel(u_ref, v_ref, ub_ref, vtb_ref):
    ub_ref[0] = u_ref[0].astype(ub_ref.dtype)
    vtb_ref[0] = v_ref[0].T.astype(vtb_ref.dtype)


def _prep_tables(peer_u, peer_v):
    rows = 512
    src = pl.BlockSpec((1, rows, D_MODEL), lambda l, i: (l, i, 0))
    return pl.pallas_call(
        _tables_kernel,
        grid=(DEPTH, N_EXPERTS // rows),
        in_specs=[src, src],
        out_specs=[src, pl.BlockSpec((1, D_MODEL, rows), lambda l, i: (l, 0, i))],
        out_shape=[jax.ShapeDtypeStruct((DEPTH, N_EXPERTS, D_MODEL), MXU_DTYPE),
                   jax.ShapeDtypeStruct((DEPTH, D_MODEL, N_EXPERTS), MXU_DTYPE)],
        compiler_params=_params(("parallel", "parallel")),
        name="peer_tables",
    )(peer_u, peer_v)


def _peer(grp, h2t, r2, b, jd, a, u, vt, l, x1, mod_l, g_final, final):
    tt = PEER_TT
    eb = PEER_E1 * N_KEYS
    row = grp.mod_row(tt)
    gate = pl.BlockSpec((PEER_HEADS, N_KEYS, tt), lambda i, j: (0, 0, i))
    gate1 = pl.BlockSpec((PEER_HEADS, PEER_E1, tt), lambda i, j: (0, j, i))
    return pl.pallas_call(
        functools.partial(_peer_kernel, final),
        grid=(grp.t // tt, N_EXPERTS // eb),
        in_specs=[pl.BlockSpec((D_MODEL, tt), lambda i, j: (0, i)), gate, gate, gate1, gate1,
                  pl.BlockSpec((None, eb, D_MODEL), lambda i, j: (l, j, 0)),
                  pl.BlockSpec((None, D_MODEL, eb), lambda i, j: (l, 0, j)),
                  pl.BlockSpec((tt, D_MODEL), lambda i, j: (i, 0)),
                  pl.BlockSpec((1, 6, D_MODEL), lambda i, j: (row(i), 0, 0)),
                  pl.BlockSpec((1, D_MODEL), lambda i, j: (0, 0))],
        out_specs=pl.BlockSpec((tt, D_MODEL), lambda i, j: (i, 0)),
        out_shape=jax.ShapeDtypeStruct((grp.t, D_MODEL), f32),
        scratch_shapes=[pltpu.VMEM((D_MODEL, tt), f32), pltpu.VMEM((eb, tt), MXU_DTYPE),
                        pltpu.VMEM((2, max(PEER_SUBS) * N_KEYS, PEER_TW), GATE_DTYPE)],
        compiler_params=_params(("parallel", "arbitrary")),
        name="peer",
    )(h2t, r2, b, jd, a, u, vt, x1, mod_l, g_final)


def _reorder_w_in(w_in_l):
    ng = 4 * M_HEADS
    a = w_in_l[:, :4 * MIX_W]
    g = w_in_l[:, 4 * MIX_W:4 * MIX_W + ng]
    rest = w_in_l[:, 4 * MIX_W + ng:]
    pad = jnp.zeros((D_MODEL, LANES - ng), w_in_l.dtype)
    return jnp.concatenate([a, rest, g, pad], axis=1).astype(MXU_DTYPE)


def _forward(x_prompt, x_sample, cache_k, cache_v, state_C, state_n, state_m, c, c_ctx,
             w_ada, b_ada, g_norm1, g_norm2, w_in, b_gates_m, g_mlstm, w_spatial, b_spatial, sink,
             w_branch, w_merge, b_merge, w_out, w_peer_q, peer_keys, peer_u, peer_v, g_final):
    nb_c, s_c, _ = x_prompt.shape
    nb_l, s_l, _ = x_sample.shape
    ctx = _Group(nb_c, s_c, 0, False)
    lat = _Group(nb_l, s_l, 1, True)
    lc = cache_k.shape[2]
    nh = 2 * M_HEADS

    cond = jnp.concatenate([c_ctx[None, :], c], axis=0)
    cond = jnp.pad(cond, ((0, (-cond.shape[0]) % 8), (0, 0)))
    mod = _ada(cond, w_ada, b_ada).reshape(DEPTH, cond.shape[0], 6, D_MODEL)
    xc = x_prompt.reshape(ctx.t, D_MODEL)
    xl = x_sample.reshape(lat.t, D_MODEL)
    cos, sin = _rope_tables(s_l)
    ck = cache_k.reshape(nb_l, DEPTH, lc, A_KVW)
    cv = cache_v.reshape(nb_l, DEPTH, lc, A_KVW)
    zero_c = jnp.zeros((nb_c, nh, M_DH, M_DH), f32)
    zero_n = jnp.zeros((nb_c, nh, M_DH), f32)
    zero_m = jnp.zeros((nb_c, nh, LANES), f32)
    gfin = g_final.reshape(1, D_MODEL)
    u, vt = _prep_tables(peer_u, peer_v)

    ks, vs, cs, ns, ms = [], [], [], [], []
    for l in range(DEPTH):
        g1 = g_norm1[l].reshape(1, D_MODEL)
        g2 = g_norm2[l].reshape(1, D_MODEL)
        w_in_l = _reorder_w_in(w_in[l])
        bias_row = jnp.pad(b_gates_m[l].reshape(1, 4 * M_HEADS), ((0, 0), (0, LANES - 4 * M_HEADS)))
        merge_w = (g_mlstm[l].reshape(1, MIX_W), w_spatial[l].astype(MXU_DTYPE),
                   jnp.pad(b_spatial[l].T, ((0, 0), (0, LANES - C_GROUPS))), w_merge[l].astype(MXU_DTYPE),
                   b_merge[l].reshape(1, 3 * D_MODEL), w_branch[l].astype(MXU_DTYPE), w_out[l].astype(MXU_DTYPE),
                   w_peer_q[l].astype(MXU_DTYPE),
                   peer_keys[l].reshape(2 * PEER_HEADS, N_KEYS, N_KEYS).astype(MXU_DTYPE))
        final = l == DEPTH - 1

        zc, k_new, v_new = _inproj(ctx, xc, mod[l], g1, w_in_l, True)
        hf, hb, c_new, n_new, m_new = _mlstm(zc, bias_row, zero_c, zero_n, zero_m, nb_c, s_c)
        ya = _ctx_attn(ctx, zc, sink, l)
        x1, h2t, st = _merge(ctx, xc, mod[l], g1, g2, hf, hb, zc, ya, *merge_w)
        xc = _peer(ctx, h2t, *_route(st), u, vt, l, x1, mod[l], gfin, final)
        ks.append(k_new.reshape(nb_c, s_c, A_KV, A_DH))
        vs.append(v_new.reshape(nb_c, s_c, A_KV, A_DH))
        cs.append(c_new.reshape(nb_c, 2, M_HEADS, M_DH, M_DH))
        ns.append(n_new.reshape(nb_c, 2, M_HEADS, M_DH))
        ms.append(m_new[:, :, 0].reshape(nb_c, 2, M_HEADS))

        zl = _inproj(lat, xl, mod[l], g1, w_in_l, False)[0]
        m0 = jnp.broadcast_to(state_m[:, l].reshape(nb_l, nh, 1), (nb_l, nh, LANES))
        hf, hb, _, _, _ = _mlstm(zl, bias_row, state_C[:, l].reshape(nb_l, nh, M_DH, M_DH),
                                 state_n[:, l].reshape(nb_l, nh, M_DH), m0, nb_l, s_l)
        ya = _lat_attn(lat, zl, sink, l, ck, cv, cos, sin)
        x1, h2t, st = _merge(lat, xl, mod[l], g1, g2, hf, hb, zl, ya, *merge_w)
        xl = _peer(lat, h2t, *_route(st), u, vt, l, x1, mod[l], gfin, final)
    return (xc.reshape(nb_c, s_c, D_MODEL), xl.reshape(nb_l, s_l, D_MODEL),
            jnp.stack(ks, axis=1), jnp.stack(vs, axis=1), jnp.stack(cs, axis=1), jnp.stack(ns, axis=1),
            jnp.stack(ms, axis=1))


def kernel(x_prompt, x_sample, cache_k, cache_v, state_C, state_n, state_m, c, c_ctx, w_ada, b_ada, g_norm1, g_norm2, w_in, b_gates_m, g_mlstm, w_spatial, b_spatial, sink, w_branch, w_merge, b_merge, w_out, w_peer_q, peer_keys, peer_u, peer_v, g_final):
    return _forward(x_prompt, x_sample, cache_k, cache_v, state_C, state_n, state_m, c, c_ctx, w_ada, b_ada,
                    g_norm1, g_norm2, w_in, b_gates_m, g_mlstm, w_spatial, b_spatial, sink, w_branch, w_merge,
                    b_merge, w_out, w_peer_q, peer_keys, peer_u, peer_v, g_final)
```

```python
import functools

import numpy as np
import jax
import jax.numpy as jnp
from jax import lax
from jax.experimental import pallas as pl
from jax.experimental.pallas import tpu as pltpu

D_MODEL = 1024
DEPTH = 2
GRID_W = 64
EPS = 1e-6
NEG_INF = -1e30
MIX_W = D_MODEL // 2
M_HEADS = 4
M_DH = MIX_W // M_HEADS
CHUNK = 128
C_GROUPS = 4
A_HEADS = 8
A_KV = 2
A_GROUP = A_HEADS // A_KV
A_DH = MIX_W // A_HEADS
A_KVW = A_KV * A_DH
ROPE_BASE = 10000.0
ROPE_FREQS = A_DH // 4
N_KEYS = 128
N_EXPERTS = N_KEYS * N_KEYS
PEER_HEADS = 8
PEER_TOPK = 16
PEER_QW = 2 * PEER_HEADS * N_KEYS

LANES = 128
MXU_DTYPE = jnp.bfloat16
GATE_DTYPE = jnp.bfloat16
VMEM_LIMIT = 56 * 1024 * 1024

Z_QM, Z_KM, Z_VM, Z_OM, Z_UC, Z_VC, Z_QA = (i * MIX_W for i in range(7))
Z_KA = 7 * MIX_W
Z_VA = Z_KA + A_KVW
Z_GM = Z_VA + A_KVW
Z_W = Z_GM + LANES

f32 = jnp.float32


def _params(sem):
    return pltpu.CompilerParams(dimension_semantics=sem, vmem_limit_bytes=VMEM_LIMIT)


def _mm(a, b):
    return jnp.dot(a.astype(MXU_DTYPE), b.astype(MXU_DTYPE), preferred_element_type=f32)


def _mm_nt(a, b):
    return lax.dot_general(a.astype(MXU_DTYPE), b.astype(MXU_DTYPE), (((1,), (1,)), ((), ())),
                           preferred_element_type=f32)


def _split3(x):
    hi = x.astype(jnp.bfloat16)
    r1 = x - hi.astype(f32)
    mid = r1.astype(jnp.bfloat16)
    lo = (r1 - mid.astype(f32)).astype(jnp.bfloat16)
    return hi, mid, lo


def _mm_exact_lhs(a01, x):
    a = a01.astype(jnp.bfloat16)
    hi, mid, lo = _split3(x)
    return (jnp.dot(a, hi, preferred_element_type=f32) + jnp.dot(a, mid, preferred_element_type=f32)
            + jnp.dot(a, lo, preferred_element_type=f32))


def _mm_exact_rhs(x, b01):
    b = b01.astype(jnp.bfloat16)
    hi, mid, lo = _split3(x)
    return (jnp.dot(hi, b, preferred_element_type=f32) + jnp.dot(mid, b, preferred_element_type=f32)
            + jnp.dot(lo, b, preferred_element_type=f32))


def _mm3(a, b):
    ah, am, al = _split3(a)
    bh, bm, bl = _split3(b)
    d = functools.partial(jnp.dot, preferred_element_type=f32)
    return (d(ah, bh) + (d(ah, bm) + d(am, bh)) + (d(ah, bl) + d(al, bh) + d(am, bm)))


def _rms(x):
    return x * lax.rsqrt(jnp.mean(x * x, axis=-1, keepdims=True) + EPS)


def _mod_norm(x, g, scale, shift):
    return _rms(x) * g * (1.0 + scale) + shift


def _ada_kernel(c_ref, w_ref, b_ref, o_ref):
    c = c_ref[...]
    o_ref[0] = _mm3(c * jax.nn.sigmoid(c), w_ref[0]) + b_ref[0]


def _ada(cond, w_ada, b_ada):
    rows = cond.shape[0]
    tn = 1536
    return pl.pallas_call(
        _ada_kernel,
        grid=(DEPTH, 6 * D_MODEL // tn),
        in_specs=[pl.BlockSpec((rows, D_MODEL), lambda l, j: (0, 0)),
                  pl.BlockSpec((1, D_MODEL, tn), lambda l, j: (l, 0, j)),
                  pl.BlockSpec((1, 1, tn), lambda l, j: (l, 0, j))],
        out_specs=pl.BlockSpec((1, rows, tn), lambda l, j: (l, 0, j)),
        out_shape=jax.ShapeDtypeStruct((DEPTH, rows, 6 * D_MODEL), f32),
        compiler_params=_params(("parallel", "parallel")),
        name="ada",
    )(cond, w_ada, b_ada.reshape(DEPTH, 1, 6 * D_MODEL))


class _Group:
    def __init__(self, nb, s, cond0, per_sequence):
        self.nb, self.s, self.cond0, self.per_sequence = nb, s, cond0, per_sequence
        self.t = nb * s

    def mod_row(self, tm):
        assert self.t % tm == 0
        if not self.per_sequence:
            return lambda i: self.cond0
        assert self.s % tm == 0
        return lambda i: self.cond0 + i // (self.s // tm)


def _inproj_kernel(x_ref, mod_ref, g1_ref, w_ref, z_ref, *kv_refs):
    h = _mod_norm(x_ref[...], g1_ref[...], mod_ref[0, 1:2, :], mod_ref[0, 0:1, :])
    z = _mm(h, w_ref[...])
    z_ref[...] = z
    if kv_refs:
        kv_refs[0][...] = z[:, Z_KA:Z_KA + A_KVW]
        kv_refs[1][...] = z[:, Z_VA:Z_VA + A_KVW]


def _inproj(grp, x, mod_l, g1, w, emit_kv):
    tm = 256
    row = grp.mod_row(tm)
    kv_spec = [pl.BlockSpec((tm, A_KVW), lambda i: (i, 0))] * 2 if emit_kv else []
    kv_shape = [jax.ShapeDtypeStruct((grp.t, A_KVW), f32)] * 2 if emit_kv else []
    return pl.pallas_call(
        _inproj_kernel,
        grid=(grp.t // tm,),
        in_specs=[pl.BlockSpec((tm, D_MODEL), lambda i: (i, 0)),
                  pl.BlockSpec((1, 6, D_MODEL), lambda i: (row(i), 0, 0)),
                  pl.BlockSpec((1, D_MODEL), lambda i: (0, 0)),
                  pl.BlockSpec((D_MODEL, Z_W), lambda i: (0, 0), pipeline_mode=pl.Buffered(1))],
        out_specs=[pl.BlockSpec((tm, Z_W), lambda i: (i, 0))] + kv_spec,
        out_shape=[jax.ShapeDtypeStruct((grp.t, Z_W), f32)] + kv_shape,
        compiler_params=_params(("parallel",)),
        name="inproj",
    )(x, mod_l, g1, w)


def _log_sigmoid(x):
    return jnp.minimum(x, 0.0) - jnp.log1p(jnp.exp(-jnp.abs(x)))


def _col(a, j):
    lane = lax.broadcasted_iota(jnp.int32, a.shape, 1)
    return jnp.sum(jnp.where(lane == j, a, 0.0), axis=1, keepdims=True)


def _mlstm_kernel(qf_ref, kf_ref, vf_ref, gf_ref, qb_ref, kb_ref, vb_ref, gb_ref, bias_ref,
                  c0_ref, n0_ref, m0_ref, hf_ref, hb_ref, c_out, n_out, m_out, ct_s, n_s, m_s):
    c = pl.program_id(1)
    nseq = qf_ref.shape[0]
    nh = 2 * M_HEADS

    @pl.when(c == 0)
    def _():
        for p in range(nseq):
            for i in range(nh):
                ct_s[p * nh + i] = c0_ref[p, i].T
            n_s[p * nh:(p + 1) * nh, :] = n0_ref[p]
            m_s[p * nh:(p + 1) * nh, :] = m0_ref[p]

    ri = lax.broadcasted_iota(jnp.int32, (CHUNK, CHUNK), 0)
    ci = lax.broadcasted_iota(jnp.int32, (CHUNK, CHUNK), 1)
    lane = lax.broadcasted_iota(jnp.int32, (CHUNK, LANES), 1)
    is_forget = ((lane // M_HEADS) % 2) == 1

    refs = ((qf_ref, kf_ref, vf_ref, gf_ref, hf_ref), (qb_ref, kb_ref, vb_ref, gb_ref, hb_ref))
    keeps = (ri >= ci, ri <= ci)
    krow = lax.broadcasted_iota(jnp.int32, (LANES, M_HEADS * LANES), 0)
    head = lax.broadcasted_iota(jnp.int32, (LANES, M_HEADS * LANES), 1) // LANES
    gates = {}
    for p in range(nseq):
        for d in range(2):
            g = refs[d][3][p] + bias_ref[...]
            lg = jnp.where(is_forget, _log_sigmoid(g), g)
            bc = _mm_exact_lhs(keeps[d], lg)
            li_tiles = _mm_exact_rhs(lg, krow == 2 * d * M_HEADS + head)
            lf_tiles = _mm_exact_rhs(bc, krow == (2 * d + 1) * M_HEADS + head)
            gates[p, d] = (lg.T, bc.T, li_tiles, lf_tiles)

    chains = [(p, d, hh) for p in range(nseq) for d in range(2) for hh in range(M_HEADS)]
    st = {}
    for ch in chains:
        p, d, hh = ch
        lgt, bct, li_tiles, lf_tiles = gates[p, d]
        idx = p * nh + d * M_HEADS + hh
        j_li = 2 * d * M_HEADS + hh
        j_lf = j_li + M_HEADS
        sl = slice(hh * M_DH, (hh + 1) * M_DH)
        q = refs[d][0][p, :, sl]
        k = refs[d][1][p, :, sl] * (M_DH ** -0.5)
        v = refs[d][2][p, :, sl]
        b_t = lf_tiles[:, hh * LANES:(hh + 1) * LANES]
        li_t = li_tiles[:, hh * LANES:(hh + 1) * LANES]
        a_t = b_t + m_s[idx:idx + 1, :]
        dm = jnp.where(keeps[d], b_t - bct[j_lf:j_lf + 1, :] + lgt[j_li:j_li + 1, :], -jnp.inf)
        m_t = jnp.maximum(a_t, jnp.max(dm, axis=1, keepdims=True))
        st[ch] = dict(idx=idx, sl=sl, q=q, k=k, v=v, kt=k.T, b_t=b_t, li_t=li_t, m_t=m_t,
                      w=jnp.exp(dm - m_t), w0=jnp.exp(a_t - m_t))
    for ch in chains:
        c_ = st[ch]
        c_["s"] = c_["w"] * _mm(c_["q"], c_["kt"])
        c_["ct"] = ct_s[c_["idx"]]
        c_["qc"] = _mm(c_["q"], c_["ct"])
        c_["n_row"] = n_s[c_["idx"]:c_["idx"] + 1, :]
        c_["qn"] = _mm_nt(c_["q"], jnp.broadcast_to(c_["n_row"], (CHUNK, M_DH)))
    for ch in chains:
        p, d, hh = ch
        c_ = st[ch]
        num = c_["w0"] * c_["qc"] + _mm(c_["s"], c_["v"])
        den = c_["w0"] * c_["qn"] + jnp.sum(c_["s"], axis=1, keepdims=True)
        refs[d][4][p, :, c_["sl"]] = num / jnp.maximum(jnp.abs(den), jnp.exp(-c_["m_t"]))
    for ch in chains:
        p, d, hh = ch
        c_ = st[ch]
        idx = c_["idx"]
        te = 0 if d == 1 else CHUNK - 1
        m_end = c_["m_t"][te:te + 1, :]
        w0_end = c_["w0"][te:te + 1, :]
        w_end = jnp.exp(c_["b_t"][te:te + 1, :] - c_["b_t"] + c_["li_t"] - m_end)
        ct_s[idx] = w0_end * c_["ct"] + _mm(c_["kt"], c_["v"] * w_end)
        n_s[idx:idx + 1, :] = w0_end * c_["n_row"] + jnp.sum(c_["k"] * w_end, axis=0, keepdims=True)
        m_s[idx:idx + 1, :] = m_end

    @pl.when(c == pl.num_programs(1) - 1)
    def _():
        for p in range(nseq):
            for i in range(nh):
                c_out[p, i] = ct_s[p * nh + i].T
            n_out[p] = n_s[p * nh:(p + 1) * nh, :]
            m_out[p] = m_s[p * nh:(p + 1) * nh, :]


MLSTM_SEQS = 2


def _mlstm(z, bias_row, c0, n0, m0, nb, s):
    nc = s // CHUNK
    nh = 2 * M_HEADS
    ps = MLSTM_SEQS
    assert nb % ps == 0
    z3 = z.reshape(nb, s, Z_W)
    fwd = lambda col: (lambda b, c: (b, c, col))
    bwd = lambda col: (lambda b, c: (b, nc - 1 - c, col))
    wide = lambda im: pl.BlockSpec((ps, CHUNK, MIX_W), im)
    gate = lambda im: pl.BlockSpec((ps, CHUNK, LANES), im)
    state = lambda shape: pl.BlockSpec((ps,) + shape, lambda b, c: (b,) + (0,) * len(shape))
    gcol = Z_GM // LANES
    hf, hb, c_new, n_new, m_new = pl.pallas_call(
        _mlstm_kernel,
        grid=(nb // ps, nc),
        in_specs=[wide(fwd(0)), wide(fwd(1)), wide(fwd(2)), gate(fwd(gcol)),
                  wide(bwd(0)), wide(bwd(1)), wide(bwd(2)), gate(bwd(gcol)),
                  pl.BlockSpec((1, LANES), lambda b, c: (0, 0)),
                  state((nh, M_DH, M_DH)), state((nh, M_DH)), state((nh, LANES))],
        out_specs=[wide(fwd(0)), wide(bwd(0)),
                   state((nh, M_DH, M_DH)), state((nh, M_DH)), state((nh, LANES))],
        out_shape=[jax.ShapeDtypeStruct((nb, s, MIX_W), f32), jax.ShapeDtypeStruct((nb, s, MIX_W), f32),
                   jax.ShapeDtypeStruct((nb, nh, M_DH, M_DH), f32), jax.ShapeDtypeStruct((nb, nh, M_DH), f32),
                   jax.ShapeDtypeStruct((nb, nh, LANES), f32)],
        scratch_shapes=[pltpu.VMEM((ps * nh, M_DH, M_DH), f32), pltpu.VMEM((ps * nh, M_DH), f32),
                        pltpu.VMEM((ps * nh, LANES), f32)],
        compiler_params=_params(("parallel", "arbitrary")),
        name="mlstm",
    )(z3, z3, z3, z3, z3, z3, z3, z3, bias_row, c0, n0, m0)
    return hf.reshape(nb * s, MIX_W), hb.reshape(nb * s, MIX_W), c_new, n_new, m_new


def _half_placements(a):
    lane = lax.broadcasted_iota(jnp.int32, a.shape, 1)
    g0 = jnp.where(lane < A_DH, a, 0.0)
    g1 = jnp.where(lane >= A_DH, a, 0.0)
    return ((g0, pltpu.roll(g0, A_DH, 1)), (pltpu.roll(g1, A_DH, 1), g1))


def _group_attend(q, k_all, v_all, sink_ref, l, keep):
    kz = _half_placements(k_all)
    vz = _half_placements(v_all)
    heads = [(head, head // 2, head % 2, head // A_GROUP) for head in range(A_HEADS)]
    scores = [_mm_nt(q[:, slab * LANES:(slab + 1) * LANES], kz[g][pos]) * (A_DH ** -0.5)
              for _, slab, pos, g in heads]
    if keep is not None:
        scores = [jnp.where(keep, s, NEG_INF) for s in scores]
    sinks = [sink_ref[l, head] for head, _, _, _ in heads]
    tops = [jnp.maximum(jnp.max(s, axis=1, keepdims=True), sk) for s, sk in zip(scores, sinks)]
    probs = [jnp.exp(s - m) for s, m in zip(scores, tops)]
    dens = [jnp.sum(p, axis=1, keepdims=True) + jnp.exp(sk - m) for p, sk, m in zip(probs, sinks, tops)]
    outs = [_mm(p, vz[g][pos]) / den for p, den, (_, _, pos, g) in zip(probs, dens, heads)]
    return jnp.concatenate([outs[2 * slab] + outs[2 * slab + 1] for slab in range(A_HEADS // 2)], axis=1)


def _ctx_attn_kernel(l, sink_ref, q_ref, k_ref, v_ref, o_ref):
    o_ref[...] = _group_attend(q_ref[...], k_ref[...], v_ref[...], sink_ref, l, None)


def _ctx_attn(grp, z, sink, l):
    s = grp.s
    return pl.pallas_call(
        functools.partial(_ctx_attn_kernel, l),
        grid=(grp.nb,),
        in_specs=[pl.BlockSpec(memory_space=pltpu.SMEM),
                  pl.BlockSpec((s, MIX_W), lambda b: (b, Z_QA // MIX_W)),
                  pl.BlockSpec((s, A_KVW), lambda b: (b, Z_KA // A_KVW)),
                  pl.BlockSpec((s, A_KVW), lambda b: (b, Z_VA // A_KVW))],
        out_specs=pl.BlockSpec((s, MIX_W), lambda b: (b, 0)),
        out_shape=jax.ShapeDtypeStruct((grp.t, MIX_W), f32),
        compiler_params=_params(("parallel",)),
        name="ctx_attn",
    )(sink, z, z, z)


def _rope(x, cos, sin):
    lane = lax.broadcasted_iota(jnp.int32, cos.shape, 1)
    first = (lane % (2 * ROPE_FREQS)) < ROPE_FREQS
    out = []
    for j in range(x.shape[1] // LANES):
        xs = x[:, j * LANES:(j + 1) * LANES]
        partner = jnp.where(first, pltpu.roll(xs, LANES - ROPE_FREQS, 1), pltpu.roll(xs, ROPE_FREQS, 1))
        out.append(xs * cos + partner * sin)
    return out[0] if len(out) == 1 else jnp.concatenate(out, axis=1)


def _lat_attn_kernel(l, nblk, sink_ref, q_ref, kp_ref, kc_ref, kn_ref, vp_ref, vc_ref, vn_ref,
                     cq_ref, sq_ref, cp_ref, sp_ref, cn_ref, sn_ref, ck_ref, cv_ref, o_ref):
    i = pl.program_id(1)
    q = _rope(q_ref[...], cq_ref[...], sq_ref[...])
    k_all = jnp.concatenate([ck_ref[0, 0],
                             _rope(kp_ref[...], cp_ref[...], sp_ref[...]),
                             _rope(kc_ref[...], cq_ref[...], sq_ref[...]),
                             _rope(kn_ref[...], cn_ref[...], sn_ref[...])], axis=0)
    v_all = jnp.concatenate([cv_ref[0, 0], vp_ref[...], vc_ref[...], vn_ref[...]], axis=0)
    lc = ck_ref.shape[2]
    nk = lc + 3 * CHUNK
    r = lax.broadcasted_iota(jnp.int32, (CHUNK, nk), 0)
    cc = lax.broadcasted_iota(jnp.int32, (CHUNK, nk), 1) - lc
    lo = jnp.maximum(r, jnp.where(i == 0, CHUNK, 0))
    hi = jnp.minimum(r + 2 * CHUNK, jnp.where(i == nblk - 1, 2 * CHUNK - 1, 3 * CHUNK))
    keep = (cc < 0) | ((cc >= lo) & (cc <= hi))
    o_ref[...] = _group_attend(q, k_all, v_all, sink_ref, l, keep)


def _lat_attn(grp, z, sink, l, cache_k, cache_v, cos, sin):
    s, nb = grp.s, grp.nb
    nblk = s // CHUNK
    lc = cache_k.shape[2]

    def rows(off):
        return lambda b, i: b * nblk + jnp.clip(i + off, 0, nblk - 1)

    def zspec(width, col, off):
        rf = rows(off)
        return pl.BlockSpec((CHUNK, width), lambda b, i: (rf(b, i), col))

    def tab(off):
        return pl.BlockSpec((CHUNK, LANES), lambda b, i: (jnp.clip(i + off, 0, nblk - 1), 0))

    cache = pl.BlockSpec((1, 1, lc, A_KVW), lambda b, i: (b, l, 0, 0))
    kcol, vcol = Z_KA // A_KVW, Z_VA // A_KVW
    return pl.pallas_call(
        functools.partial(_lat_attn_kernel, l, nblk),
        grid=(nb, nblk),
        in_specs=[pl.BlockSpec(memory_space=pltpu.SMEM),
                  zspec(MIX_W, Z_QA // MIX_W, 0),
                  zspec(A_KVW, kcol, -1), zspec(A_KVW, kcol, 0), zspec(A_KVW, kcol, 1),
                  zspec(A_KVW, vcol, -1), zspec(A_KVW, vcol, 0), zspec(A_KVW, vcol, 1),
                  tab(0), tab(0), tab(-1), tab(-1), tab(1), tab(1), cache, cache],
        out_specs=pl.BlockSpec((CHUNK, MIX_W), lambda b, i: (b * nblk + i, 0)),
        out_shape=jax.ShapeDtypeStruct((grp.t, MIX_W), f32),
        compiler_params=_params(("parallel", "parallel")),
        name="lat_attn",
    )(sink, z, z, z, z, z, z, z, cos, sin, cos, sin, cos, sin, cache_k, cache_v)


def _rope_tables(s):
    t = jnp.arange(s)
    freqs = ROPE_BASE ** (-jnp.arange(ROPE_FREQS, dtype=f32) / ROPE_FREQS)
    a_row = (t // GRID_W).astype(f32)[:, None] * freqs[None, :]
    a_col = (t % GRID_W).astype(f32)[:, None] * freqs[None, :]
    cos = jnp.concatenate([jnp.cos(a_row)] * 2 + [jnp.cos(a_col)] * 2, axis=1)
    sin = jnp.concatenate([-jnp.sin(a_row), jnp.sin(a_row), -jnp.sin(a_col), jnp.sin(a_col)], axis=1)
    return jnp.concatenate([cos] * A_KV, axis=1), jnp.concatenate([sin] * A_KV, axis=1)


def _merge_kernel(x_ref, mod_ref, g1_ref, g2_ref, hf_ref, hb_ref, om_ref, uc_ref, vc_ref, ya_ref,
                  gm_ref, ws_ref, bs_ref, wmerge_ref, bmerge_ref, wbr_ref, wout_ref, wq_ref, keys_ref,
                  x1_ref, h2t_ref, st_ref):
    x = x_ref[...]
    tm = x.shape[0]
    mod = mod_ref[0]
    h = _mod_norm(x, g1_ref[...], mod[1:2], mod[0:1])
    hm = hf_ref[...] + hb_ref[...]
    ym = jnp.concatenate(
        [_rms(hm[:, hh * M_DH:(hh + 1) * M_DH]) * gm_ref[:, hh * M_DH:(hh + 1) * M_DH] for hh in range(M_HEADS)],
        axis=1) * jax.nn.sigmoid(om_ref[...])
    vr = _rms(vc_ref[...])
    zc = []
    for n in range(tm // CHUNK):
        rs = slice(n * CHUNK, (n + 1) * CHUNK)
        zc.append(jnp.concatenate(
            [_mm(ws_ref[g], vr[rs, g * LANES:(g + 1) * LANES]) + _col(bs_ref[...], g) for g in range(C_GROUPS)],
            axis=1))
    yc = uc_ref[...] * jnp.concatenate(zc, axis=0)
    mixed = jnp.zeros((tm, D_MODEL), f32)
    for n, y in enumerate((ym, yc, ya_ref[...])):
        gate = jax.nn.sigmoid(_mm(h, wmerge_ref[:, n * D_MODEL:(n + 1) * D_MODEL])
                              + bmerge_ref[:, n * D_MODEL:(n + 1) * D_MODEL])
        mixed = mixed + gate * _mm(y, wbr_ref[n])
    x1 = x + mod[2:3] * _mm(mixed, wout_ref[...])
    x1_ref[...] = x1
    h2 = _mod_norm(x1, g2_ref[...], mod[4:5], mod[3:4])
    h2t_ref[...] = h2.T.astype(h2t_ref.dtype)
    qp = _mm(h2, wq_ref[...])
    for hp in range(2 * PEER_HEADS):
        st_ref[hp] = _mm_nt(keys_ref[hp], qp[:, hp * N_KEYS:(hp + 1) * N_KEYS])


def _merge(grp, x, mod_l, g1, g2, hf, hb, z, ya, gm, ws, bs_t, wmerge, bmerge, wbr, wout, wq, keys):
    tm = 256
    row = grp.mod_row(tm)
    tok = lambda w, col=0: pl.BlockSpec((tm, w), lambda i: (i, col))
    full = lambda a: pl.BlockSpec(a.shape, lambda i: (0,) * a.ndim, pipeline_mode=pl.Buffered(1))
    return pl.pallas_call(
        _merge_kernel,
        grid=(grp.t // tm,),
        in_specs=[tok(D_MODEL), pl.BlockSpec((1, 6, D_MODEL), lambda i: (row(i), 0, 0)), full(g1), full(g2),
                  tok(MIX_W), tok(MIX_W), tok(MIX_W, Z_OM // MIX_W), tok(MIX_W, Z_UC // MIX_W),
                  tok(MIX_W, Z_VC // MIX_W), tok(MIX_W),
                  full(gm), full(ws), full(bs_t), full(wmerge), full(bmerge), full(wbr), full(wout), full(wq),
                  full(keys)],
        out_specs=[tok(D_MODEL), pl.BlockSpec((D_MODEL, tm), lambda i: (0, i)),
                   pl.BlockSpec((2 * PEER_HEADS, N_KEYS, tm), lambda i: (0, 0, i))],
        out_shape=[jax.ShapeDtypeStruct((grp.t, D_MODEL), f32), jax.ShapeDtypeStruct((D_MODEL, grp.t), MXU_DTYPE),
                   jax.ShapeDtypeStruct((2 * PEER_HEADS, N_KEYS, grp.t), f32)],
        compiler_params=_params(("parallel",)),
        name="merge",
    )(x, mod_l, g1, g2, hf, hb, z, z, z, ya, gm, ws, bs_t, wmerge, bmerge, wbr, wout, wq, keys)


_CELLS = [(i, j) for i in range(PEER_TOPK) for j in range(PEER_TOPK) if (i + 1) * (j + 1) <= PEER_TOPK]


def _route_kernel(st_ref, r2_ref, b_ref, jd_ref, a_ref, val_s, rank_s, work_s):
    tb = st_ref.shape[2]
    nhp = 2 * PEER_HEADS
    unranked = jnp.full((N_KEYS, tb), float(PEER_TOPK), f32)

    def extract(hp, r, hit_of, rank_value=None):
        s = work_s[hp]
        m = jnp.max(s, axis=0, keepdims=True)
        hit = hit_of(s, m)
        val_s[hp % 2, r, pl.ds(hp // 2, 1), :] = m
        if rank_value is not None:
            rank_s[hp] = jnp.where(hit, rank_value, rank_s[hp])
        work_s[hp] = jnp.where(hit, -jnp.inf, s)

    def finish(ranked):
        v1 = [val_s[0, i] for i in range(PEER_TOPK)]
        v2 = [val_s[1, i] for i in range(PEER_TOPK)]
        cand = [v1[i] + v2[j] for (i, j) in _CELLS]
        ncell = len(_CELLS)
        wins = [jnp.zeros((PEER_HEADS, tb), f32) for _ in range(ncell)]
        losses = [jnp.zeros((PEER_HEADS, tb), f32) for _ in range(ncell)]
        for x in range(ncell):
            for y in range(x + 1, ncell):
                x_first = jnp.where(cand[x] >= cand[y], 1.0, 0.0)
                wins[x] = wins[x] + x_first
                losses[y] = losses[y] + x_first
        before = [losses[x] + (float(ncell - 1 - x) - wins[x]) for x in range(ncell)]
        ea = [jnp.exp(v1[i] - v1[0]) for i in range(PEER_TOPK)]
        eb = [jnp.exp(v2[j] - v2[0]) for j in range(PEER_TOPK)]
        jcount = [jnp.zeros((PEER_HEADS, tb), f32) for _ in range(PEER_TOPK)]
        zsum = jnp.zeros((PEER_HEADS, tb), f32)
        for x, (i, j) in enumerate(_CELLS):
            sel = (before[x] < PEER_TOPK).astype(f32)
            jcount[i] = jcount[i] + sel
            zsum = zsum + sel * (ea[i] * eb[j])
        inv_z = 1.0 / zsum

        for h in range(PEER_HEADS):
            s1 = st_ref[2 * h]
            s2 = st_ref[2 * h + 1]
            jd = jnp.zeros((N_KEYS, tb), f32)
            if ranked:
                rank1 = rank_s[2 * h]
                for i in range(PEER_TOPK):
                    jd = jnp.where(rank1 == float(i), jcount[i][h:h + 1, :], jd)
                r2 = rank_s[2 * h + 1]
            else:
                r2 = unranked
                for i in range(PEER_TOPK):
                    jd = jnp.where(s1 == v1[i][h:h + 1, :], jcount[i][h:h + 1, :], jd)
                    r2 = jnp.where(s2 == v2[i][h:h + 1, :], float(i), r2)
            jd_ref[h] = _pair_words(jd)
            a_ref[h] = _pair_words(jnp.exp(s1 - v1[0][h:h + 1, :]) * inv_z[h:h + 1, :])
            b_ref[h] = jnp.exp(s2 - v2[0][h:h + 1, :]).astype(b_ref.dtype)
            r2_ref[h] = r2.astype(r2_ref.dtype)

    work_s[...] = st_ref[...]

    def fast_round(r, _):
        for hp in range(nhp):
            extract(hp, r, lambda s, m: s == m)
        return 0

    lax.fori_loop(0, PEER_TOPK, fast_round, 0)
    removed = jnp.zeros((1, tb), f32)
    for hp in range(nhp):
        removed = jnp.maximum(removed, jnp.sum(jnp.where(work_s[hp] == -jnp.inf, 1.0, 0.0), axis=0, keepdims=True))
    tied = jnp.max(removed) > float(PEER_TOPK)

    @pl.when(jnp.logical_not(tied))
    def _():
        finish(ranked=False)

    @pl.when(tied)
    def _():
        key = lax.broadcasted_iota(jnp.int32, (N_KEYS, tb), 0).astype(f32)

        def lowest_index_hit(s, m):
            return key == jnp.min(jnp.where(s == m, key, float(N_KEYS)), axis=0, keepdims=True)

        work_s[...] = st_ref[...]
        for hp in range(nhp):
            rank_s[hp] = unranked

        def exact_round(i, _):
            r = i % PEER_TOPK
            extract(i // PEER_TOPK, r, lowest_index_hit, lax.convert_element_type(r, f32))
            return 0

        lax.fori_loop(0, nhp * PEER_TOPK, exact_round, 0)
        finish(ranked=True)


def _route(st):
    t = st.shape[2]
    tb = LANES
    out = [jax.ShapeDtypeStruct((PEER_HEADS, N_KEYS, t), dt)
           for dt in (GATE_DTYPE, GATE_DTYPE, jnp.uint32, jnp.uint32)]
    spec = pl.BlockSpec((PEER_HEADS, N_KEYS, tb), lambda i: (0, 0, i))
    return pl.pallas_call(
        _route_kernel,
        grid=(t // tb,),
        in_specs=[pl.BlockSpec((2 * PEER_HEADS, N_KEYS, tb), lambda i: (0, 0, i))],
        out_specs=[spec] * 4,
        out_shape=out,
        scratch_shapes=[pltpu.VMEM((2, PEER_TOPK, PEER_HEADS, tb), f32),
                        pltpu.VMEM((2 * PEER_HEADS, N_KEYS, tb), f32),
                        pltpu.VMEM((2 * PEER_HEADS, N_KEYS, tb), f32)],
        compiler_params=_params(("parallel",)),
        name="peer_route",
    )(st)


PEER_TT = 512
PEER_TW = 512
PEER_E1 = 16
PEER_SUBS = (4, 4, 4, 4)
PEER_ROWS = 16


def _pair_words(x):
    u = lax.bitcast_convert_type(x.astype(jnp.bfloat16).astype(f32), jnp.uint32)
    return u | (u >> 16)


def _bcast_pair_row(row):
    assert PEER_ROWS == 16
    return pltpu.bitcast(jnp.broadcast_to(row, (PEER_ROWS // 2, row.shape[1])), jnp.bfloat16)


def _peer_kernel(final, h2t_ref, r2_ref, b_ref, jd_ref, a_ref, u_ref, vt_ref, x1_ref, mod_ref, gf_ref,
                 o_ref, acc_s, a_s, h_s):
    j = pl.program_id(1)

    @pl.when(j == 0)
    def _():
        acc_s[...] = jnp.zeros_like(acc_s)

    assert sum(PEER_SUBS) == PEER_E1
    starts = [sum(PEER_SUBS[:sb]) for sb in range(len(PEER_SUBS))]
    n_sub = len(PEER_SUBS)
    srow = lambda sb: slice(starts[sb] * N_KEYS, (starts[sb] + PEER_SUBS[sb]) * N_KEYS)
    units = [(sb, pl.ds(w * PEER_TW, PEER_TW)) for w in range(PEER_TT // PEER_TW) for sb in range(n_sub)]

    def hdot(i):
        sb, cols = units[i]
        h_s[i % 2, :PEER_SUBS[sb] * N_KEYS, :] = jnp.dot(u_ref[srow(sb), :], h2t_ref[:, cols],
                                                        preferred_element_type=f32).astype(h_s.dtype)

    def accumulate(i):
        sb, cols = units[i]
        acc_s[:, cols] += jnp.dot(vt_ref[:, srow(sb)], a_s[srow(sb), cols], preferred_element_type=f32)

    hdot(0)
    for i, (sb, cols) in enumerate(units):
        h = h_s.at[i % 2]
        if i + 1 < len(units):
            hdot(i + 1)
        if i >= 1:
            accumulate(i - 1)
        for e in range(PEER_SUBS[sb]):
            e1 = starts[sb] + e
            n_chunk = N_KEYS // PEER_ROWS
            gates = [None] * n_chunk
            for hh in range(PEER_HEADS):
                jd = _bcast_pair_row(jd_ref[hh, e1:e1 + 1, cols])
                aa = _bcast_pair_row(a_ref[hh, e1:e1 + 1, cols])
                for r in range(n_chunk):
                    rows = slice(r * PEER_ROWS, (r + 1) * PEER_ROWS)
                    term = jnp.where(r2_ref[hh, rows, cols] < jd, b_ref[hh, rows, cols],
                                     jnp.zeros((), GATE_DTYPE)) * aa
                    gates[r] = term if gates[r] is None else gates[r] + term
            for r in range(n_chunk):
                lo = e * N_KEYS + r * PEER_ROWS
                act = gates[r] * jax.nn.gelu(h[lo:lo + PEER_ROWS, :])
                a_s[pl.ds(e1 * N_KEYS + r * PEER_ROWS, PEER_ROWS), cols] = act.astype(a_s.dtype)
    accumulate(len(units) - 1)

    @pl.when(j == pl.num_programs(1) - 1)
    def _():
        x2 = x1_ref[...] + mod_ref[0, 5:6, :] * acc_s[...].T
        o_ref[...] = _rms(x2) * gf_ref[...] if final else x2


def _tables_kernel(u_ref, v_ref, ub_ref, vtb_ref):
    ub_ref[0] = u_ref[0].astype(ub_ref.dtype)
    vtb_ref[0] = v_ref[0].T.astype(vtb_ref.dtype)


def _prep_tables(peer_u, peer_v):
    rows = 512
    src = pl.BlockSpec((1, rows, D_MODEL), lambda l, i: (l, i, 0))
    return pl.pallas_call(
        _tables_kernel,
        grid=(DEPTH, N_EXPERTS // rows),
        in_specs=[src, src],
        out_specs=[src, pl.BlockSpec((1, D_MODEL, rows), lambda l, i: (l, 0, i))],
        out_shape=[jax.ShapeDtypeStruct((DEPTH, N_EXPERTS, D_MODEL), MXU_DTYPE),
                   jax.ShapeDtypeStruct((DEPTH, D_MODEL, N_EXPERTS), MXU_DTYPE)],
        compiler_params=_params(("parallel", "parallel")),
        name="peer_tables",
    )(peer_u, peer_v)


def _peer(grp, h2t, r2, b, jd, a, u, vt, l, x1, mod_l, g_final, final):
    tt = PEER_TT
    eb = PEER_E1 * N_KEYS
    row = grp.mod_row(tt)
    gate = pl.BlockSpec((PEER_HEADS, N_KEYS, tt), lambda i, j: (0, 0, i))
    gate1 = pl.BlockSpec((PEER_HEADS, PEER_E1, tt), lambda i, j: (0, j, i))
    return pl.pallas_call(
        functools.partial(_peer_kernel, final),
        grid=(grp.t // tt, N_EXPERTS // eb),
        in_specs=[pl.BlockSpec((D_MODEL, tt), lambda i, j: (0, i)), gate, gate, gate1, gate1,
                  pl.BlockSpec((None, eb, D_MODEL), lambda i, j: (l, j, 0)),
                  pl.BlockSpec((None, D_MODEL, eb), lambda i, j: (l, 0, j)),
                  pl.BlockSpec((tt, D_MODEL), lambda i, j: (i, 0)),
                  pl.BlockSpec((1, 6, D_MODEL), lambda i, j: (row(i), 0, 0)),
                  pl.BlockSpec((1, D_MODEL), lambda i, j: (0, 0))],
        out_specs=pl.BlockSpec((tt, D_MODEL), lambda i, j: (i, 0)),
        out_shape=jax.ShapeDtypeStruct((grp.t, D_MODEL), f32),
        scratch_shapes=[pltpu.VMEM((D_MODEL, tt), f32), pltpu.VMEM((eb, tt), MXU_DTYPE),
                        pltpu.VMEM((2, max(PEER_SUBS) * N_KEYS, PEER_TW), GATE_DTYPE)],
        compiler_params=_params(("parallel", "arbitrary")),
        name="peer",
    )(h2t, r2, b, jd, a, u, vt, x1, mod_l, g_final)


def _reorder_w_in(w_in_l):
    ng = 4 * M_HEADS
    a = w_in_l[:, :4 * MIX_W]
    g = w_in_l[:, 4 * MIX_W:4 * MIX_W + ng]
    rest = w_in_l[:, 4 * MIX_W + ng:]
    pad = jnp.zeros((D_MODEL, LANES - ng), w_in_l.dtype)
    return jnp.concatenate([a, rest, g, pad], axis=1).astype(MXU_DTYPE)


def _forward(x_prompt, x_sample, cache_k, cache_v, state_C, state_n, state_m, c, c_ctx,
             w_ada, b_ada, g_norm1, g_norm2, w_in, b_gates_m, g_mlstm, w_spatial, b_spatial, sink,
             w_branch, w_merge, b_merge, w_out, w_peer_q, peer_keys, peer_u, peer_v, g_final):
    nb_c, s_c, _ = x_prompt.shape
    nb_l, s_l, _ = x_sample.shape
    ctx = _Group(nb_c, s_c, 0, False)
    lat = _Group(nb_l, s_l, 1, True)
    lc = cache_k.shape[2]
    nh = 2 * M_HEADS

    cond = jnp.concatenate([c_ctx[None, :], c], axis=0)
    cond = jnp.pad(cond, ((0, (-cond.shape[0]) % 8), (0, 0)))
    mod = _ada(cond, w_ada, b_ada).reshape(DEPTH, cond.shape[0], 6, D_MODEL)
    xc = x_prompt.reshape(ctx.t, D_MODEL)
    xl = x_sample.reshape(lat.t, D_MODEL)
    cos, sin = _rope_tables(s_l)
    ck = cache_k.reshape(nb_l, DEPTH, lc, A_KVW)
    cv = cache_v.reshape(nb_l, DEPTH, lc, A_KVW)
    zero_c = jnp.zeros((nb_c, nh, M_DH, M_DH), f32)
    zero_n = jnp.zeros((nb_c, nh, M_DH), f32)
    zero_m = jnp.zeros((nb_c, nh, LANES), f32)
    gfin = g_final.reshape(1, D_MODEL)
    u, vt = _prep_tables(peer_u, peer_v)

    ks, vs, cs, ns, ms = [], [], [], [], []
    for l in range(DEPTH):
        g1 = g_norm1[l].reshape(1, D_MODEL)
        g2 = g_norm2[l].reshape(1, D_MODEL)
        w_in_l = _reorder_w_in(w_in[l])
        bias_row = jnp.pad(b_gates_m[l].reshape(1, 4 * M_HEADS), ((0, 0), (0, LANES - 4 * M_HEADS)))
        merge_w = (g_mlstm[l].reshape(1, MIX_W), w_spatial[l].astype(MXU_DTYPE),
                   jnp.pad(b_spatial[l].T, ((0, 0), (0, LANES - C_GROUPS))), w_merge[l].astype(MXU_DTYPE),
                   b_merge[l].reshape(1, 3 * D_MODEL), w_branch[l].astype(MXU_DTYPE), w_out[l].astype(MXU_DTYPE),
                   w_peer_q[l].astype(MXU_DTYPE),
                   peer_keys[l].reshape(2 * PEER_HEADS, N_KEYS, N_KEYS).astype(MXU_DTYPE))
        final = l == DEPTH - 1

        zc, k_new, v_new = _inproj(ctx, xc, mod[l], g1, w_in_l, True)
        hf, hb, c_new, n_new, m_new = _mlstm(zc, bias_row, zero_c, zero_n, zero_m, nb_c, s_c)
        ya = _ctx_attn(ctx, zc, sink, l)
        x1, h2t, st = _merge(ctx, xc, mod[l], g1, g2, hf, hb, zc, ya, *merge_w)
        xc = _peer(ctx, h2t, *_route(st), u, vt, l, x1, mod[l], gfin, final)
        ks.append(k_new.reshape(nb_c, s_c, A_KV, A_DH))
        vs.append(v_new.reshape(nb_c, s_c, A_KV, A_DH))
        cs.append(c_new.reshape(nb_c, 2, M_HEADS, M_DH, M_DH))
        ns.append(n_new.reshape(nb_c, 2, M_HEADS, M_DH))
        ms.append(m_new[:, :, 0].reshape(nb_c, 2, M_HEADS))

        zl = _inproj(lat, xl, mod[l], g1, w_in_l, False)[0]
        m0 = jnp.broadcast_to(state_m[:, l].reshape(nb_l, nh, 1), (nb_l, nh, LANES))
        hf, hb, _, _, _ = _mlstm(zl, bias_row, state_C[:, l].reshape(nb_l, nh, M_DH, M_DH),
                                 state_n[:, l].reshape(nb_l, nh, M_DH), m0, nb_l, s_l)
        ya = _lat_attn(lat, zl, sink, l, ck, cv, cos, sin)
        x1, h2t, st = _merge(lat, xl, mod[l], g1, g2, hf, hb, zl, ya, *merge_w)
        xl = _peer(lat, h2t, *_route(st), u, vt, l, x1, mod[l], gfin, final)
    return (xc.reshape(nb_c, s_c, D_MODEL), xl.reshape(nb_l, s_l, D_MODEL),
            jnp.stack(ks, axis=1), jnp.stack(vs, axis=1), jnp.stack(cs, axis=1), jnp.stack(ns, axis=1),
            jnp.stack(ms, axis=1))


def kernel(x_prompt, x_sample, cache_k, cache_v, state_C, state_n, state_m, c, c_ctx, w_ada, b_ada, g_norm1, g_norm2, w_in, b_gates_m, g_mlstm, w_spatial, b_spatial, sink, w_branch, w_merge, b_merge, w_out, w_peer_q, peer_keys, peer_u, peer_v, g_final):
    return _forward(x_prompt, x_sample, cache_k, cache_v, state_C, state_n, state_m, c, c_ctx, w_ada, b_ada,
                    g_norm1, g_norm2, w_in, b_gates_m, g_mlstm, w_spatial, b_spatial, sink, w_branch, w_merge,
                    b_merge, w_out, w_peer_q, peer_keys, peer_u, peer_v, g_final)
```

```python
import functools
import math

import numpy as np
import jax
import jax.numpy as jnp
from jax import lax
from jax.experimental import pallas as pl
from jax.experimental.pallas import tpu as pltpu

D_MODEL = 1024
DEPTH = 2
GRID_W = 64
EPS = 1e-6
NEG_INF = -1e30
MIX_W = D_MODEL // 2
M_HEADS = 4
M_DH = MIX_W // M_HEADS
CHUNK = 128
C_GROUPS = 4
A_HEADS = 8
A_KV = 2
A_GROUP = A_HEADS // A_KV
A_DH = MIX_W // A_HEADS
A_KVW = A_KV * A_DH
ROPE_BASE = 10000.0
ROPE_FREQS = A_DH // 4
N_KEYS = 128
N_EXPERTS = N_KEYS * N_KEYS
PEER_HEADS = 8
PEER_TOPK = 16
PEER_QW = 2 * PEER_HEADS * N_KEYS

LANES = 128
MXU_DTYPE = jnp.bfloat16
GATE_DTYPE = jnp.bfloat16
VMEM_LIMIT = 56 * 1024 * 1024

Z_QM, Z_KM, Z_VM, Z_OM, Z_UC, Z_VC, Z_QA = (i * MIX_W for i in range(7))
Z_KA = 7 * MIX_W
Z_VA = Z_KA + A_KVW
Z_GM = Z_VA + A_KVW
Z_W = Z_GM + LANES

f32 = jnp.float32


def _params(sem):
    return pltpu.CompilerParams(dimension_semantics=sem, vmem_limit_bytes=VMEM_LIMIT)


def _mm(a, b):
    return jnp.dot(a.astype(MXU_DTYPE), b.astype(MXU_DTYPE), preferred_element_type=f32)


def _mm_nt(a, b):
    return lax.dot_general(a.astype(MXU_DTYPE), b.astype(MXU_DTYPE), (((1,), (1,)), ((), ())),
                           preferred_element_type=f32)


def _split3(x):
    hi = x.astype(jnp.bfloat16)
    r1 = x - hi.astype(f32)
    mid = r1.astype(jnp.bfloat16)
    lo = (r1 - mid.astype(f32)).astype(jnp.bfloat16)
    return hi, mid, lo


def _mm_exact_lhs(a01, x):
    a = a01.astype(jnp.bfloat16)
    hi, mid, lo = _split3(x)
    return (jnp.dot(a, hi, preferred_element_type=f32) + jnp.dot(a, mid, preferred_element_type=f32)
            + jnp.dot(a, lo, preferred_element_type=f32))


def _mm_exact_rhs(x, b01):
    b = b01.astype(jnp.bfloat16)
    hi, mid, lo = _split3(x)
    return (jnp.dot(hi, b, preferred_element_type=f32) + jnp.dot(mid, b, preferred_element_type=f32)
            + jnp.dot(lo, b, preferred_element_type=f32))


def _mm3(a, b):
    ah, am, al = _split3(a)
    bh, bm, bl = _split3(b)
    d = functools.partial(jnp.dot, preferred_element_type=f32)
    return (d(ah, bh) + (d(ah, bm) + d(am, bh)) + (d(ah, bl) + d(al, bh) + d(am, bm)))


def _rms(x):
    return x * lax.rsqrt(jnp.mean(x * x, axis=-1, keepdims=True) + EPS)


def _mod_norm(x, g, scale, shift):
    return _rms(x) * g * (1.0 + scale) + shift


def _ada_kernel(c_ref, w_ref, b_ref, o_ref):
    c = c_ref[...]
    o_ref[0] = _mm3(c * jax.nn.sigmoid(c), w_ref[0]) + b_ref[0]


def _ada(cond, w_ada, b_ada):
    rows = cond.shape[0]
    tn = 1536
    return pl.pallas_call(
        _ada_kernel,
        grid=(DEPTH, 6 * D_MODEL // tn),
        in_specs=[pl.BlockSpec((rows, D_MODEL), lambda l, j: (0, 0)),
                  pl.BlockSpec((1, D_MODEL, tn), lambda l, j: (l, 0, j)),
                  pl.BlockSpec((1, 1, tn), lambda l, j: (l, 0, j))],
        out_specs=pl.BlockSpec((1, rows, tn), lambda l, j: (l, 0, j)),
        out_shape=jax.ShapeDtypeStruct((DEPTH, rows, 6 * D_MODEL), f32),
        compiler_params=_params(("parallel", "parallel")),
        name="ada",
    )(cond, w_ada, b_ada.reshape(DEPTH, 1, 6 * D_MODEL))


class _Group:
    def __init__(self, nb, s, cond0, per_sequence):
        self.nb, self.s, self.cond0, self.per_sequence = nb, s, cond0, per_sequence
        self.t = nb * s

    def mod_row(self, tm):
        assert self.t % tm == 0
        if not self.per_sequence:
            return lambda i: self.cond0
        assert self.s % tm == 0
        return lambda i: self.cond0 + i // (self.s // tm)


def _inproj_kernel(x_ref, mod_ref, g1_ref, w_ref, z_ref, *kv_refs):
    h = _mod_norm(x_ref[...], g1_ref[...], mod_ref[0, 1:2, :], mod_ref[0, 0:1, :])
    z = _mm(h, w_ref[...])
    z_ref[...] = z
    if kv_refs:
        kv_refs[0][...] = z[:, Z_KA:Z_KA + A_KVW]
        kv_refs[1][...] = z[:, Z_VA:Z_VA + A_KVW]


def _inproj(grp, x, mod_l, g1, w, emit_kv):
    tm = 256
    row = grp.mod_row(tm)
    kv_spec = [pl.BlockSpec((tm, A_KVW), lambda i: (i, 0))] * 2 if emit_kv else []
    kv_shape = [jax.ShapeDtypeStruct((grp.t, A_KVW), f32)] * 2 if emit_kv else []
    return pl.pallas_call(
        _inproj_kernel,
        grid=(grp.t // tm,),
        in_specs=[pl.BlockSpec((tm, D_MODEL), lambda i: (i, 0)),
                  pl.BlockSpec((1, 6, D_MODEL), lambda i: (row(i), 0, 0)),
                  pl.BlockSpec((1, D_MODEL), lambda i: (0, 0)),
                  pl.BlockSpec((D_MODEL, Z_W), lambda i: (0, 0), pipeline_mode=pl.Buffered(1))],
        out_specs=[pl.BlockSpec((tm, Z_W), lambda i: (i, 0))] + kv_spec,
        out_shape=[jax.ShapeDtypeStruct((grp.t, Z_W), f32)] + kv_shape,
        compiler_params=_params(("parallel",)),
        name="inproj",
    )(x, mod_l, g1, w)


def _log_sigmoid(x):
    return jnp.minimum(x, 0.0) - jnp.log1p(jnp.exp(-jnp.abs(x)))


def _col(a, j):
    lane = lax.broadcasted_iota(jnp.int32, a.shape, 1)
    return jnp.sum(jnp.where(lane == j, a, 0.0), axis=1, keepdims=True)


def _mlstm_kernel(qf_ref, kf_ref, vf_ref, gf_ref, qb_ref, kb_ref, vb_ref, gb_ref, bias_ref,
                  c0_ref, n0_ref, m0_ref, hf_ref, hb_ref, c_out, n_out, m_out, ct_s, n_s, m_s):
    c = pl.program_id(1)
    nseq = qf_ref.shape[0]
    nh = 2 * M_HEADS

    @pl.when(c == 0)
    def _():
        for p in range(nseq):
            for i in range(nh):
                ct_s[p * nh + i] = c0_ref[p, i].T
            n_s[p * nh:(p + 1) * nh, :] = n0_ref[p]
            m_s[p * nh:(p + 1) * nh, :] = m0_ref[p]

    ri = lax.broadcasted_iota(jnp.int32, (CHUNK, CHUNK), 0)
    ci = lax.broadcasted_iota(jnp.int32, (CHUNK, CHUNK), 1)
    lane = lax.broadcasted_iota(jnp.int32, (CHUNK, LANES), 1)
    is_forget = ((lane // M_HEADS) % 2) == 1

    refs = ((qf_ref, kf_ref, vf_ref, gf_ref, hf_ref), (qb_ref, kb_ref, vb_ref, gb_ref, hb_ref))
    keeps = (ri >= ci, ri <= ci)
    krow = lax.broadcasted_iota(jnp.int32, (LANES, M_HEADS * LANES), 0)
    head = lax.broadcasted_iota(jnp.int32, (LANES, M_HEADS * LANES), 1) // LANES
    gates = {}
    for p in range(nseq):
        for d in range(2):
            g = refs[d][3][p] + bias_ref[...]
            lg = jnp.where(is_forget, _log_sigmoid(g), g)
            bc = _mm_exact_lhs(keeps[d], lg)
            li_tiles = _mm_exact_rhs(lg, krow == 2 * d * M_HEADS + head)
            lf_tiles = _mm_exact_rhs(bc, krow == (2 * d + 1) * M_HEADS + head)
            gates[p, d] = (lg.T, bc.T, li_tiles, lf_tiles)

    chains = [(p, d, hh) for p in range(nseq) for d in range(2) for hh in range(M_HEADS)]
    st = {}
    for ch in chains:
        p, d, hh = ch
        lgt, bct, li_tiles, lf_tiles = gates[p, d]
        idx = p * nh + d * M_HEADS + hh
        j_li = 2 * d * M_HEADS + hh
        j_lf = j_li + M_HEADS
        sl = slice(hh * M_DH, (hh + 1) * M_DH)
        q = refs[d][0][p, :, sl]
        k = refs[d][1][p, :, sl] * (M_DH ** -0.5)
        v = refs[d][2][p, :, sl]
        b_t = lf_tiles[:, hh * LANES:(hh + 1) * LANES]
        li_t = li_tiles[:, hh * LANES:(hh + 1) * LANES]
        a_t = b_t + m_s[idx:idx + 1, :]
        dm = jnp.where(keeps[d], b_t - bct[j_lf:j_lf + 1, :] + lgt[j_li:j_li + 1, :], -jnp.inf)
        m_t = jnp.maximum(a_t, jnp.max(dm, axis=1, keepdims=True))
        st[ch] = dict(idx=idx, sl=sl, q=q, k=k, v=v, kt=k.T, b_t=b_t, li_t=li_t, m_t=m_t,
                      w=jnp.exp(dm - m_t), w0=jnp.exp(a_t - m_t))
    for ch in chains:
        c_ = st[ch]
        c_["s"] = c_["w"] * _mm(c_["q"], c_["kt"])
        c_["ct"] = ct_s[c_["idx"]]
        c_["qc"] = _mm(c_["q"], c_["ct"])
        c_["n_row"] = n_s[c_["idx"]:c_["idx"] + 1, :]
        c_["qn"] = _mm_nt(c_["q"], jnp.broadcast_to(c_["n_row"], (CHUNK, M_DH)))
    for ch in chains:
        p, d, hh = ch
        c_ = st[ch]
        num = c_["w0"] * c_["qc"] + _mm(c_["s"], c_["v"])
        den = c_["w0"] * c_["qn"] + jnp.sum(c_["s"], axis=1, keepdims=True)
        refs[d][4][p, :, c_["sl"]] = num / jnp.maximum(jnp.abs(den), jnp.exp(-c_["m_t"]))
    for ch in chains:
        p, d, hh = ch
        c_ = st[ch]
        idx = c_["idx"]
        te = 0 if d == 1 else CHUNK - 1
        m_end = c_["m_t"][te:te + 1, :]
        w0_end = c_["w0"][te:te + 1, :]
        w_end = jnp.exp(c_["b_t"][te:te + 1, :] - c_["b_t"] + c_["li_t"] - m_end)
        ct_s[idx] = w0_end * c_["ct"] + _mm(c_["kt"], c_["v"] * w_end)
        n_s[idx:idx + 1, :] = w0_end * c_["n_row"] + jnp.sum(c_["k"] * w_end, axis=0, keepdims=True)
        m_s[idx:idx + 1, :] = m_end

    @pl.when(c == pl.num_programs(1) - 1)
    def _():
        for p in range(nseq):
            for i in range(nh):
                c_out[p, i] = ct_s[p * nh + i].T
            n_out[p] = n_s[p * nh:(p + 1) * nh, :]
            m_out[p] = m_s[p * nh:(p + 1) * nh, :]


MLSTM_SEQS = 4


def _mlstm(z, bias_row, c0, n0, m0, nb, s):
    nc = s // CHUNK
    nh = 2 * M_HEADS
    ps = math.gcd(nb, MLSTM_SEQS)
    z3 = z.reshape(nb, s, Z_W)
    fwd = lambda col: (lambda b, c: (b, c, col))
    bwd = lambda col: (lambda b, c: (b, nc - 1 - c, col))
    wide = lambda im: pl.BlockSpec((ps, CHUNK, MIX_W), im)
    gate = lambda im: pl.BlockSpec((ps, CHUNK, LANES), im)
    state = lambda shape: pl.BlockSpec((ps,) + shape, lambda b, c: (b,) + (0,) * len(shape))
    gcol = Z_GM // LANES
    hf, hb, c_new, n_new, m_new = pl.pallas_call(
        _mlstm_kernel,
        grid=(nb // ps, nc),
        in_specs=[wide(fwd(0)), wide(fwd(1)), wide(fwd(2)), gate(fwd(gcol)),
                  wide(bwd(0)), wide(bwd(1)), wide(bwd(2)), gate(bwd(gcol)),
                  pl.BlockSpec((1, LANES), lambda b, c: (0, 0)),
                  state((nh, M_DH, M_DH)), state((nh, M_DH)), state((nh, LANES))],
        out_specs=[wide(fwd(0)), wide(bwd(0)),
                   state((nh, M_DH, M_DH)), state((nh, M_DH)), state((nh, LANES))],
        out_shape=[jax.ShapeDtypeStruct((nb, s, MIX_W), f32), jax.ShapeDtypeStruct((nb, s, MIX_W), f32),
                   jax.ShapeDtypeStruct((nb, nh, M_DH, M_DH), f32), jax.ShapeDtypeStruct((nb, nh, M_DH), f32),
                   jax.ShapeDtypeStruct((nb, nh, LANES), f32)],
        scratch_shapes=[pltpu.VMEM((ps * nh, M_DH, M_DH), f32), pltpu.VMEM((ps * nh, M_DH), f32),
                        pltpu.VMEM((ps * nh, LANES), f32)],
        compiler_params=_params(("parallel", "arbitrary")),
        name="mlstm",
    )(z3, z3, z3, z3, z3, z3, z3, z3, bias_row, c0, n0, m0)
    return hf.reshape(nb * s, MIX_W), hb.reshape(nb * s, MIX_W), c_new, n_new, m_new


def _half_placements(a):
    lane = lax.broadcasted_iota(jnp.int32, a.shape, 1)
    g0 = jnp.where(lane < A_DH, a, 0.0)
    g1 = jnp.where(lane >= A_DH, a, 0.0)
    return ((g0, pltpu.roll(g0, A_DH, 1)), (pltpu.roll(g1, A_DH, 1), g1))


def _group_attend(q, k_all, v_all, sink_ref, l, keep):
    kz = _half_placements(k_all)
    vz = _half_placements(v_all)
    heads = [(head, head // 2, head % 2, head // A_GROUP) for head in range(A_HEADS)]
    scores = [_mm_nt(q[:, slab * LANES:(slab + 1) * LANES], kz[g][pos]) * (A_DH ** -0.5)
              for _, slab, pos, g in heads]
    if keep is not None:
        scores = [jnp.where(keep, s, NEG_INF) for s in scores]
    sinks = [sink_ref[l, head] for head, _, _, _ in heads]
    tops = [jnp.maximum(jnp.max(s, axis=1, keepdims=True), sk) for s, sk in zip(scores, sinks)]
    probs = [jnp.exp(s - m) for s, m in zip(scores, tops)]
    dens = [jnp.sum(p, axis=1, keepdims=True) + jnp.exp(sk - m) for p, sk, m in zip(probs, sinks, tops)]
    outs = [_mm(p, vz[g][pos]) / den for p, den, (_, _, pos, g) in zip(probs, dens, heads)]
    return jnp.concatenate([outs[2 * slab] + outs[2 * slab + 1] for slab in range(A_HEADS // 2)], axis=1)


def _ctx_attn_kernel(l, sink_ref, q_ref, k_ref, v_ref, o_ref):
    o_ref[...] = _group_attend(q_ref[...], k_ref[...], v_ref[...], sink_ref, l, None)


def _ctx_attn(grp, z, sink, l):
    s = grp.s
    return pl.pallas_call(
        functools.partial(_ctx_attn_kernel, l),
        grid=(grp.nb,),
        in_specs=[pl.BlockSpec(memory_space=pltpu.SMEM),
                  pl.BlockSpec((s, MIX_W), lambda b: (b, Z_QA // MIX_W)),
                  pl.BlockSpec((s, A_KVW), lambda b: (b, Z_KA // A_KVW)),
                  pl.BlockSpec((s, A_KVW), lambda b: (b, Z_VA // A_KVW))],
        out_specs=pl.BlockSpec((s, MIX_W), lambda b: (b, 0)),
        out_shape=jax.ShapeDtypeStruct((grp.t, MIX_W), f32),
        compiler_params=_params(("parallel",)),
        name="ctx_attn",
    )(sink, z, z, z)


def _rope(x, cos, sin):
    lane = lax.broadcasted_iota(jnp.int32, cos.shape, 1)
    first = (lane % (2 * ROPE_FREQS)) < ROPE_FREQS
    out = []
    for j in range(x.shape[1] // LANES):
        xs = x[:, j * LANES:(j + 1) * LANES]
        partner = jnp.where(first, pltpu.roll(xs, LANES - ROPE_FREQS, 1), pltpu.roll(xs, ROPE_FREQS, 1))
        out.append(xs * cos + partner * sin)
    return out[0] if len(out) == 1 else jnp.concatenate(out, axis=1)


def _lat_attn_kernel(l, nblk, sink_ref, q_ref, kp_ref, kc_ref, kn_ref, vp_ref, vc_ref, vn_ref,
                     cq_ref, sq_ref, cp_ref, sp_ref, cn_ref, sn_ref, ck_ref, cv_ref, o_ref):
    i = pl.program_id(1)
    q = _rope(q_ref[...], cq_ref[...], sq_ref[...])
    k_all = jnp.concatenate([ck_ref[0, 0],
                             _rope(kp_ref[...], cp_ref[...], sp_ref[...]),
                             _rope(kc_ref[...], cq_ref[...], sq_ref[...]),
                             _rope(kn_ref[...], cn_ref[...], sn_ref[...])], axis=0)
    v_all = jnp.concatenate([cv_ref[0, 0], vp_ref[...], vc_ref[...], vn_ref[...]], axis=0)
    lc = ck_ref.shape[2]
    nk = lc + 3 * CHUNK
    r = lax.broadcasted_iota(jnp.int32, (CHUNK, nk), 0)
    cc = lax.broadcasted_iota(jnp.int32, (CHUNK, nk), 1) - lc
    lo = jnp.maximum(r, jnp.where(i == 0, CHUNK, 0))
    hi = jnp.minimum(r + 2 * CHUNK, jnp.where(i == nblk - 1, 2 * CHUNK - 1, 3 * CHUNK))
    keep = (cc < 0) | ((cc >= lo) & (cc <= hi))
    o_ref[...] = _group_attend(q, k_all, v_all, sink_ref, l, keep)


def _lat_attn(grp, z, sink, l, cache_k, cache_v, cos, sin):
    s, nb = grp.s, grp.nb
    nblk = s // CHUNK
    lc = cache_k.shape[2]

    def rows(off):
        return lambda b, i: b * nblk + jnp.clip(i + off, 0, nblk - 1)

    def zspec(width, col, off):
        rf = rows(off)
        return pl.BlockSpec((CHUNK, width), lambda b, i: (rf(b, i), col))

    def tab(off):
        return pl.BlockSpec((CHUNK, LANES), lambda b, i: (jnp.clip(i + off, 0, nblk - 1), 0))

    cache = pl.BlockSpec((1, 1, lc, A_KVW), lambda b, i: (b, l, 0, 0))
    kcol, vcol = Z_KA // A_KVW, Z_VA // A_KVW
    return pl.pallas_call(
        functools.partial(_lat_attn_kernel, l, nblk),
        grid=(nb, nblk),
        in_specs=[pl.BlockSpec(memory_space=pltpu.SMEM),
                  zspec(MIX_W, Z_QA // MIX_W, 0),
                  zspec(A_KVW, kcol, -1), zspec(A_KVW, kcol, 0), zspec(A_KVW, kcol, 1),
                  zspec(A_KVW, vcol, -1), zspec(A_KVW, vcol, 0), zspec(A_KVW, vcol, 1),
                  tab(0), tab(0), tab(-1), tab(-1), tab(1), tab(1), cache, cache],
        out_specs=pl.BlockSpec((CHUNK, MIX_W), lambda b, i: (b * nblk + i, 0)),
        out_shape=jax.ShapeDtypeStruct((grp.t, MIX_W), f32),
        compiler_params=_params(("parallel", "parallel")),
        name="lat_attn",
    )(sink, z, z, z, z, z, z, z, cos, sin, cos, sin, cos, sin, cache_k, cache_v)


def _rope_tables(s):
    t = jnp.arange(s)
    freqs = ROPE_BASE ** (-jnp.arange(ROPE_FREQS, dtype=f32) / ROPE_FREQS)
    a_row = (t // GRID_W).astype(f32)[:, None] * freqs[None, :]
    a_col = (t % GRID_W).astype(f32)[:, None] * freqs[None, :]
    cos = jnp.concatenate([jnp.cos(a_row)] * 2 + [jnp.cos(a_col)] * 2, axis=1)
    sin = jnp.concatenate([-jnp.sin(a_row), jnp.sin(a_row), -jnp.sin(a_col), jnp.sin(a_col)], axis=1)
    return jnp.concatenate([cos] * A_KV, axis=1), jnp.concatenate([sin] * A_KV, axis=1)


def _merge_kernel(x_ref, mod_ref, g1_ref, g2_ref, hf_ref, hb_ref, om_ref, uc_ref, vc_ref, ya_ref,
                  gm_ref, ws_ref, bs_ref, wmerge_ref, bmerge_ref, wbr_ref, wout_ref, wq_ref, keys_ref,
                  x1_ref, h2t_ref, st_ref):
    x = x_ref[...]
    tm = x.shape[0]
    mod = mod_ref[0]
    h = _mod_norm(x, g1_ref[...], mod[1:2], mod[0:1])
    hm = hf_ref[...] + hb_ref[...]
    ym = jnp.concatenate(
        [_rms(hm[:, hh * M_DH:(hh + 1) * M_DH]) * gm_ref[:, hh * M_DH:(hh + 1) * M_DH] for hh in range(M_HEADS)],
        axis=1) * jax.nn.sigmoid(om_ref[...])
    vr = _rms(vc_ref[...])
    zc = []
    for n in range(tm // CHUNK):
        rs = slice(n * CHUNK, (n + 1) * CHUNK)
        zc.append(jnp.concatenate(
            [_mm(ws_ref[g], vr[rs, g * LANES:(g + 1) * LANES]) + _col(bs_ref[...], g) for g in range(C_GROUPS)],
            axis=1))
    yc = uc_ref[...] * jnp.concatenate(zc, axis=0)
    mixed = jnp.zeros((tm, D_MODEL), f32)
    for n, y in enumerate((ym, yc, ya_ref[...])):
        gate = jax.nn.sigmoid(_mm(h, wmerge_ref[:, n * D_MODEL:(n + 1) * D_MODEL])
                              + bmerge_ref[:, n * D_MODEL:(n + 1) * D_MODEL])
        mixed = mixed + gate * _mm(y, wbr_ref[n])
    x1 = x + mod[2:3] * _mm(mixed, wout_ref[...])
    x1_ref[...] = x1
    h2 = _mod_norm(x1, g2_ref[...], mod[4:5], mod[3:4])
    h2t_ref[...] = h2.T.astype(h2t_ref.dtype)
    qp = _mm(h2, wq_ref[...])
    for hp in range(2 * PEER_HEADS):
        st_ref[hp] = _mm_nt(keys_ref[hp], qp[:, hp * N_KEYS:(hp + 1) * N_KEYS])


def _merge(grp, x, mod_l, g1, g2, hf, hb, z, ya, gm, ws, bs_t, wmerge, bmerge, wbr, wout, wq, keys):
    tm = 256
    row = grp.mod_row(tm)
    tok = lambda w, col=0: pl.BlockSpec((tm, w), lambda i: (i, col))
    full = lambda a: pl.BlockSpec(a.shape, lambda i: (0,) * a.ndim, pipeline_mode=pl.Buffered(1))
    return pl.pallas_call(
        _merge_kernel,
        grid=(grp.t // tm,),
        in_specs=[tok(D_MODEL), pl.BlockSpec((1, 6, D_MODEL), lambda i: (row(i), 0, 0)), full(g1), full(g2),
                  tok(MIX_W), tok(MIX_W), tok(MIX_W, Z_OM // MIX_W), tok(MIX_W, Z_UC // MIX_W),
                  tok(MIX_W, Z_VC // MIX_W), tok(MIX_W),
                  full(gm), full(ws), full(bs_t), full(wmerge), full(bmerge), full(wbr), full(wout), full(wq),
                  full(keys)],
        out_specs=[tok(D_MODEL), pl.BlockSpec((D_MODEL, tm), lambda i: (0, i)),
                   pl.BlockSpec((2 * PEER_HEADS, N_KEYS, tm), lambda i: (0, 0, i))],
        out_shape=[jax.ShapeDtypeStruct((grp.t, D_MODEL), f32), jax.ShapeDtypeStruct((D_MODEL, grp.t), MXU_DTYPE),
                   jax.ShapeDtypeStruct((2 * PEER_HEADS, N_KEYS, grp.t), f32)],
        compiler_params=_params(("parallel",)),
        name="merge",
    )(x, mod_l, g1, g2, hf, hb, z, z, z, ya, gm, ws, bs_t, wmerge, bmerge, wbr, wout, wq, keys)


_CELLS = [(i, j) for i in range(PEER_TOPK) for j in range(PEER_TOPK) if (i + 1) * (j + 1) <= PEER_TOPK]


def _route_kernel(st_ref, r2_ref, b_ref, jd_ref, a_ref, val_s, rank_s, work_s):
    tb = st_ref.shape[2]
    nhp = 2 * PEER_HEADS
    unranked = jnp.full((N_KEYS, tb), float(PEER_TOPK), f32)

    def extract(hp, r, hit_of, rank_value=None):
        s = work_s[hp]
        m = jnp.max(s, axis=0, keepdims=True)
        hit = hit_of(s, m)
        val_s[hp % 2, r, pl.ds(hp // 2, 1), :] = m
        if rank_value is not None:
            rank_s[hp] = jnp.where(hit, rank_value, rank_s[hp])
        work_s[hp] = jnp.where(hit, -jnp.inf, s)

    def finish(ranked):
        v1 = [val_s[0, i] for i in range(PEER_TOPK)]
        v2 = [val_s[1, i] for i in range(PEER_TOPK)]
        cand = [v1[i] + v2[j] for (i, j) in _CELLS]
        ncell = len(_CELLS)
        wins = [jnp.zeros((PEER_HEADS, tb), f32) for _ in range(ncell)]
        losses = [jnp.zeros((PEER_HEADS, tb), f32) for _ in range(ncell)]
        for x in range(ncell):
            for y in range(x + 1, ncell):
                x_first = jnp.where(cand[x] >= cand[y], 1.0, 0.0)
                wins[x] = wins[x] + x_first
                losses[y] = losses[y] + x_first
        before = [losses[x] + (float(ncell - 1 - x) - wins[x]) for x in range(ncell)]
        ea = [jnp.exp(v1[i] - v1[0]) for i in range(PEER_TOPK)]
        eb = [jnp.exp(v2[j] - v2[0]) for j in range(PEER_TOPK)]
        jcount = [jnp.zeros((PEER_HEADS, tb), f32) for _ in range(PEER_TOPK)]
        zsum = jnp.zeros((PEER_HEADS, tb), f32)
        for x, (i, j) in enumerate(_CELLS):
            sel = (before[x] < PEER_TOPK).astype(f32)
            jcount[i] = jcount[i] + sel
            zsum = zsum + sel * (ea[i] * eb[j])
        inv_z = 1.0 / zsum

        for h in range(PEER_HEADS):
            s1 = st_ref[2 * h]
            s2 = st_ref[2 * h + 1]
            jd = jnp.zeros((N_KEYS, tb), f32)
            if ranked:
                rank1 = rank_s[2 * h]
                for i in range(PEER_TOPK):
                    jd = jnp.where(rank1 == float(i), jcount[i][h:h + 1, :], jd)
                r2 = rank_s[2 * h + 1]
            else:
                r2 = unranked
                for i in range(PEER_TOPK):
                    jd = jnp.where(s1 == v1[i][h:h + 1, :], jcount[i][h:h + 1, :], jd)
                    r2 = jnp.where(s2 == v2[i][h:h + 1, :], float(i), r2)
            jd_ref[h] = _pair_words(jd)
            a_ref[h] = _pair_words(jnp.exp(s1 - v1[0][h:h + 1, :]) * inv_z[h:h + 1, :])
            b_ref[h] = jnp.exp(s2 - v2[0][h:h + 1, :]).astype(b_ref.dtype)
            r2_ref[h] = r2.astype(r2_ref.dtype)

    work_s[...] = st_ref[...]

    def fast_round(r, _):
        for hp in range(nhp):
            extract(hp, r, lambda s, m: s == m)
        return 0

    lax.fori_loop(0, PEER_TOPK, fast_round, 0)
    removed = jnp.zeros((1, tb), f32)
    for hp in range(nhp):
        removed = jnp.maximum(removed, jnp.sum(jnp.where(work_s[hp] == -jnp.inf, 1.0, 0.0), axis=0, keepdims=True))
    tied = jnp.max(removed) > float(PEER_TOPK)

    @pl.when(jnp.logical_not(tied))
    def _():
        finish(ranked=False)

    @pl.when(tied)
    def _():
        key = lax.broadcasted_iota(jnp.int32, (N_KEYS, tb), 0).astype(f32)

        def lowest_index_hit(s, m):
            return key == jnp.min(jnp.where(s == m, key, float(N_KEYS)), axis=0, keepdims=True)

        work_s[...] = st_ref[...]
        for hp in range(nhp):
            rank_s[hp] = unranked

        def exact_round(i, _):
            r = i % PEER_TOPK
            extract(i // PEER_TOPK, r, lowest_index_hit, lax.convert_element_type(r, f32))
            return 0

        lax.fori_loop(0, nhp * PEER_TOPK, exact_round, 0)
        finish(ranked=True)


def _route(st):
    t = st.shape[2]
    tb = LANES
    out = [jax.ShapeDtypeStruct((PEER_HEADS, N_KEYS, t), dt)
           for dt in (GATE_DTYPE, GATE_DTYPE, jnp.uint32, jnp.uint32)]
    spec = pl.BlockSpec((PEER_HEADS, N_KEYS, tb), lambda i: (0, 0, i))
    return pl.pallas_call(
        _route_kernel,
        grid=(t // tb,),
        in_specs=[pl.BlockSpec((2 * PEER_HEADS, N_KEYS, tb), lambda i: (0, 0, i))],
        out_specs=[spec] * 4,
        out_shape=out,
        scratch_shapes=[pltpu.VMEM((2, PEER_TOPK, PEER_HEADS, tb), f32),
                        pltpu.VMEM((2 * PEER_HEADS, N_KEYS, tb), f32),
                        pltpu.VMEM((2 * PEER_HEADS, N_KEYS, tb), f32)],
        compiler_params=_params(("parallel",)),
        name="peer_route",
    )(st)


PEER_TT = 512
PEER_TW = 512
PEER_E1 = 16
PEER_SUBS = (4, 4, 4, 4)
PEER_ROWS = 16


def _pair_words(x):
    u = lax.bitcast_convert_type(x.astype(jnp.bfloat16).astype(f32), jnp.uint32)
    return u | (u >> 16)


def _bcast_pair_row(row):
    assert PEER_ROWS == 16
    return pltpu.bitcast(jnp.broadcast_to(row, (PEER_ROWS // 2, row.shape[1])), jnp.bfloat16)


def _peer_kernel(final, h2t_ref, r2_ref, b_ref, jd_ref, a_ref, u_ref, vt_ref, x1_ref, mod_ref, gf_ref,
                 o_ref, acc_s, a_s, h_s):
    j = pl.program_id(1)

    @pl.when(j == 0)
    def _():
        acc_s[...] = jnp.zeros_like(acc_s)

    assert sum(PEER_SUBS) == PEER_E1
    starts = [sum(PEER_SUBS[:sb]) for sb in range(len(PEER_SUBS))]
    n_sub = len(PEER_SUBS)
    srow = lambda sb: slice(starts[sb] * N_KEYS, (starts[sb] + PEER_SUBS[sb]) * N_KEYS)
    units = [(sb, pl.ds(w * PEER_TW, PEER_TW)) for w in range(PEER_TT // PEER_TW) for sb in range(n_sub)]

    def hdot(i):
        sb, cols = units[i]
        h_s[i % 2, :PEER_SUBS[sb] * N_KEYS, :] = jnp.dot(u_ref[srow(sb), :], h2t_ref[:, cols],
                                                        preferred_element_type=f32).astype(h_s.dtype)

    def accumulate(i):
        sb, cols = units[i]
        acc_s[:, cols] += jnp.dot(vt_ref[:, srow(sb)], a_s[srow(sb), cols], preferred_element_type=f32)

    hdot(0)
    for i, (sb, cols) in enumerate(units):
        h = h_s.at[i % 2]
        if i + 1 < len(units):
            hdot(i + 1)
        if i >= 1:
            accumulate(i - 1)
        for e in range(PEER_SUBS[sb]):
            e1 = starts[sb] + e
            n_chunk = N_KEYS // PEER_ROWS
            gates = [None] * n_chunk
            for hh in range(PEER_HEADS):
                jd = _bcast_pair_row(jd_ref[hh, e1:e1 + 1, cols])
                aa = _bcast_pair_row(a_ref[hh, e1:e1 + 1, cols])
                for r in range(n_chunk):
                    rows = slice(r * PEER_ROWS, (r + 1) * PEER_ROWS)
                    term = jnp.where(r2_ref[hh, rows, cols] < jd, b_ref[hh, rows, cols],
                                     jnp.zeros((), GATE_DTYPE)) * aa
                    gates[r] = term if gates[r] is None else gates[r] + term
            for r in range(n_chunk):
                lo = e * N_KEYS + r * PEER_ROWS
                act = gates[r] * jax.nn.gelu(h[lo:lo + PEER_ROWS, :])
                a_s[pl.ds(e1 * N_KEYS + r * PEER_ROWS, PEER_ROWS), cols] = act.astype(a_s.dtype)
    accumulate(len(units) - 1)

    @pl.when(j == pl.num_programs(1) - 1)
    def _():
        x2 = x1_ref[...] + mod_ref[0, 5:6, :] * acc_s[...].T
        o_ref[...] = _rms(x2) * gf_ref[...] if final else x2


def _tables_kernel(u_ref, v_ref, ub_ref, vtb_ref):
    ub_ref[0] = u_ref[0].astype(ub_ref.dtype)
    vtb_ref[0] = v_ref[0].T.astype(vtb_ref.dtype)


def _prep_tables(peer_u, peer_v):
    rows = 512
    src = pl.BlockSpec((1, rows, D_MODEL), lambda l, i: (l, i, 0))
    return pl.pallas_call(
        _tables_kernel,
        grid=(DEPTH, N_EXPERTS // rows),
        in_specs=[src, src],
        out_specs=[src, pl.BlockSpec((1, D_MODEL, rows), lambda l, i: (l, 0, i))],
        out_shape=[jax.ShapeDtypeStruct((DEPTH, N_EXPERTS, D_MODEL), MXU_DTYPE),
                   jax.ShapeDtypeStruct((DEPTH, D_MODEL, N_EXPERTS), MXU_DTYPE)],
        compiler_params=_params(("parallel", "parallel")),
        name="peer_tables",
    )(peer_u, peer_v)


def _peer(grp, h2t, r2, b, jd, a, u, vt, l, x1, mod_l, g_final, final):
    tt = PEER_TT
    eb = PEER_E1 * N_KEYS
    row = grp.mod_row(tt)
    gate = pl.BlockSpec((PEER_HEADS, N_KEYS, tt), lambda i, j: (0, 0, i))
    gate1 = pl.BlockSpec((PEER_HEADS, PEER_E1, tt), lambda i, j: (0, j, i))
    return pl.pallas_call(
        functools.partial(_peer_kernel, final),
        grid=(grp.t // tt, N_EXPERTS // eb),
        in_specs=[pl.BlockSpec((D_MODEL, tt), lambda i, j: (0, i)), gate, gate, gate1, gate1,
                  pl.BlockSpec((None, eb, D_MODEL), lambda i, j: (l, j, 0)),
                  pl.BlockSpec((None, D_MODEL, eb), lambda i, j: (l, 0, j)),
                  pl.BlockSpec((tt, D_MODEL), lambda i, j: (i, 0)),
                  pl.BlockSpec((1, 6, D_MODEL), lambda i, j: (row(i), 0, 0)),
                  pl.BlockSpec((1, D_MODEL), lambda i, j: (0, 0))],
        out_specs=pl.BlockSpec((tt, D_MODEL), lambda i, j: (i, 0)),
        out_shape=jax.ShapeDtypeStruct((grp.t, D_MODEL), f32),
        scratch_shapes=[pltpu.VMEM((D_MODEL, tt), f32), pltpu.VMEM((eb, tt), MXU_DTYPE),
                        pltpu.VMEM((2, max(PEER_SUBS) * N_KEYS, PEER_TW), GATE_DTYPE)],
        compiler_params=_params(("parallel", "arbitrary")),
        name="peer",
    )(h2t, r2, b, jd, a, u, vt, x1, mod_l, g_final)


def _reorder_w_in(w_in_l):
    ng = 4 * M_HEADS
    a = w_in_l[:, :4 * MIX_W]
    g = w_in_l[:, 4 * MIX_W:4 * MIX_W + ng]
    rest = w_in_l[:, 4 * MIX_W + ng:]
    pad = jnp.zeros((D_MODEL, LANES - ng), w_in_l.dtype)
    return jnp.concatenate([a, rest, g, pad], axis=1).astype(MXU_DTYPE)


def _forward(x_prompt, x_sample, cache_k, cache_v, state_C, state_n, state_m, c, c_ctx,
             w_ada, b_ada, g_norm1, g_norm2, w_in, b_gates_m, g_mlstm, w_spatial, b_spatial, sink,
             w_branch, w_merge, b_merge, w_out, w_peer_q, peer_keys, peer_u, peer_v, g_final):
    nb_c, s_c, _ = x_prompt.shape
    nb_l, s_l, _ = x_sample.shape
    ctx = _Group(nb_c, s_c, 0, False)
    lat = _Group(nb_l, s_l, 1, True)
    lc = cache_k.shape[2]
    nh = 2 * M_HEADS

    cond = jnp.concatenate([c_ctx[None, :], c], axis=0)
    cond = jnp.pad(cond, ((0, (-cond.shape[0]) % 8), (0, 0)))
    mod = _ada(cond, w_ada, b_ada).reshape(DEPTH, cond.shape[0], 6, D_MODEL)
    xc = x_prompt.reshape(ctx.t, D_MODEL)
    xl = x_sample.reshape(lat.t, D_MODEL)
    cos, sin = _rope_tables(s_l)
    ck = cache_k.reshape(nb_l, DEPTH, lc, A_KVW)
    cv = cache_v.reshape(nb_l, DEPTH, lc, A_KVW)
    zero_c = jnp.zeros((nb_c, nh, M_DH, M_DH), f32)
    zero_n = jnp.zeros((nb_c, nh, M_DH), f32)
    zero_m = jnp.zeros((nb_c, nh, LANES), f32)
    gfin = g_final.reshape(1, D_MODEL)
    u, vt = _prep_tables(peer_u, peer_v)

    ks, vs, cs, ns, ms = [], [], [], [], []
    for l in range(DEPTH):
        g1 = g_norm1[l].reshape(1, D_MODEL)
        g2 = g_norm2[l].reshape(1, D_MODEL)
        w_in_l = _reorder_w_in(w_in[l])
        bias_row = jnp.pad(b_gates_m[l].reshape(1, 4 * M_HEADS), ((0, 0), (0, LANES - 4 * M_HEADS)))
        merge_w = (g_mlstm[l].reshape(1, MIX_W), w_spatial[l].astype(MXU_DTYPE),
                   jnp.pad(b_spatial[l].T, ((0, 0), (0, LANES - C_GROUPS))), w_merge[l].astype(MXU_DTYPE),
                   b_merge[l].reshape(1, 3 * D_MODEL), w_branch[l].astype(MXU_DTYPE), w_out[l].astype(MXU_DTYPE),
                   w_peer_q[l].astype(MXU_DTYPE),
                   peer_keys[l].reshape(2 * PEER_HEADS, N_KEYS, N_KEYS).astype(MXU_DTYPE))
        final = l == DEPTH - 1

        zc, k_new, v_new = _inproj(ctx, xc, mod[l], g1, w_in_l, True)
        hf, hb, c_new, n_new, m_new = _mlstm(zc, bias_row, zero_c, zero_n, zero_m, nb_c, s_c)
        ya = _ctx_attn(ctx, zc, sink, l)
        x1, h2t, st = _merge(ctx, xc, mod[l], g1, g2, hf, hb, zc, ya, *merge_w)
        xc = _peer(ctx, h2t, *_route(st), u, vt, l, x1, mod[l], gfin, final)
        ks.append(k_new.reshape(nb_c, s_c, A_KV, A_DH))
        vs.append(v_new.reshape(nb_c, s_c, A_KV, A_DH))
        cs.append(c_new.reshape(nb_c, 2, M_HEADS, M_DH, M_DH))
        ns.append(n_new.reshape(nb_c, 2, M_HEADS, M_DH))
        ms.append(m_new[:, :, 0].reshape(nb_c, 2, M_HEADS))

        zl = _inproj(lat, xl, mod[l], g1, w_in_l, False)[0]
        m0 = jnp.broadcast_to(state_m[:, l].reshape(nb_l, nh, 1), (nb_l, nh, LANES))
        hf, hb, _, _, _ = _mlstm(zl, bias_row, state_C[:, l].reshape(nb_l, nh, M_DH, M_DH),
                                 state_n[:, l].reshape(nb_l, nh, M_DH), m0, nb_l, s_l)
        ya = _lat_attn(lat, zl, sink, l, ck, cv, cos, sin)
        x1, h2t, st = _merge(lat, xl, mod[l], g1, g2, hf, hb, zl, ya, *merge_w)
        xl = _peer(lat, h2t, *_route(st), u, vt, l, x1, mod[l], gfin, final)
    return (xc.reshape(nb_c, s_c, D_MODEL), xl.reshape(nb_l, s_l, D_MODEL),
            jnp.stack(ks, axis=1), jnp.stack(vs, axis=1), jnp.stack(cs, axis=1), jnp.stack(ns, axis=1),
            jnp.stack(ms, axis=1))


def kernel(x_prompt, x_sample, cache_k, cache_v, state_C, state_n, state_m, c, c_ctx, w_ada, b_ada, g_norm1, g_norm2, w_in, b_gates_m, g_mlstm, w_spatial, b_spatial, sink, w_branch, w_merge, b_merge, w_out, w_peer_q, peer_keys, peer_u, peer_v, g_final):
    return _forward(x_prompt, x_sample, cache_k, cache_v, state_C, state_n, state_m, c, c_ctx, w_ada, b_ada,
                    g_norm1, g_norm2, w_in, b_gates_m, g_mlstm, w_spatial, b_spatial, sink, w_branch, w_merge,
                    b_merge, w_out, w_peer_q, peer_keys, peer_u, peer_v, g_final)
```

```python
import functools

import jax
import jax.numpy as jnp
from jax import lax
from jax.experimental import pallas as pl
from jax.experimental.pallas import tpu as pltpu

D_MODEL = 1024
DEPTH = 2
GRID_W = 64
EPS = 1e-6
NEG_INF = -1e30
MIX_W = D_MODEL // 2
M_HEADS = 4
M_DH = MIX_W // M_HEADS
CHUNK = 128
C_GROUPS = 4
A_HEADS = 8
A_KV = 2
A_GROUP = A_HEADS // A_KV
A_DH = MIX_W // A_HEADS
A_KVW = A_KV * A_DH
ROPE_BASE = 10000.0
ROPE_FREQS = A_DH // 4
N_KEYS = 128
N_EXPERTS = N_KEYS * N_KEYS
PEER_HEADS = 8
PEER_TOPK = 16

LANES = 128
MXU_DTYPE = jnp.bfloat16
GATE_DTYPE = jnp.bfloat16
VMEM_LIMIT = 56 * 1024 * 1024

Z_QM, Z_KM, Z_VM, Z_OM, Z_UC, Z_VC, Z_QA = (i * MIX_W for i in range(7))
Z_KA = 7 * MIX_W
Z_VA = Z_KA + A_KVW
Z_GM = Z_VA + A_KVW
Z_W = Z_GM + LANES

f32 = jnp.float32


def _params(sem):
    return pltpu.CompilerParams(dimension_semantics=sem, vmem_limit_bytes=VMEM_LIMIT)


def _mm(a, b):
    return jnp.dot(a.astype(MXU_DTYPE), b.astype(MXU_DTYPE), preferred_element_type=f32)


def _mm_nt(a, b):
    return lax.dot_general(a.astype(MXU_DTYPE), b.astype(MXU_DTYPE), (((1,), (1,)), ((), ())),
                           preferred_element_type=f32)


def _split3(x):
    hi = x.astype(jnp.bfloat16)
    r1 = x - hi.astype(f32)
    mid = r1.astype(jnp.bfloat16)
    lo = (r1 - mid.astype(f32)).astype(jnp.bfloat16)
    return hi, mid, lo


def _mm_exact_lhs(a01, x):
    a = a01.astype(jnp.bfloat16)
    hi, mid, lo = _split3(x)
    return (jnp.dot(a, hi, preferred_element_type=f32) + jnp.dot(a, mid, preferred_element_type=f32)
            + jnp.dot(a, lo, preferred_element_type=f32))


def _mm_exact_rhs(x, b01):
    b = b01.astype(jnp.bfloat16)
    hi, mid, lo = _split3(x)
    return (jnp.dot(hi, b, preferred_element_type=f32) + jnp.dot(mid, b, preferred_element_type=f32)
            + jnp.dot(lo, b, preferred_element_type=f32))


def _mm3(a, b):
    ah, am, al = _split3(a)
    bh, bm, bl = _split3(b)
    d = functools.partial(jnp.dot, preferred_element_type=f32)
    return (d(ah, bh) + (d(ah, bm) + d(am, bh)) + (d(ah, bl) + d(al, bh) + d(am, bm)))


def _rms(x):
    return x * lax.rsqrt(jnp.mean(x * x, axis=-1, keepdims=True) + EPS)


def _mod_norm(x, g, scale, shift):
    return _rms(x) * g * (1.0 + scale) + shift


def _ada_kernel(c_ref, w_ref, b_ref, o_ref):
    c = c_ref[...]
    o_ref[0] = _mm3(c * jax.nn.sigmoid(c), w_ref[0]) + b_ref[0]


def _ada(cond, w_ada, b_ada):
    rows = cond.shape[0]
    tn = 1536
    return pl.pallas_call(
        _ada_kernel,
        grid=(DEPTH, 6 * D_MODEL // tn),
        in_specs=[pl.BlockSpec((rows, D_MODEL), lambda l, j: (0, 0)),
                  pl.BlockSpec((1, D_MODEL, tn), lambda l, j: (l, 0, j)),
                  pl.BlockSpec((1, 1, tn), lambda l, j: (l, 0, j))],
        out_specs=pl.BlockSpec((1, rows, tn), lambda l, j: (l, 0, j)),
        out_shape=jax.ShapeDtypeStruct((DEPTH, rows, 6 * D_MODEL), f32),
        compiler_params=_params(("parallel", "parallel")),
        name="ada",
    )(cond, w_ada, b_ada.reshape(DEPTH, 1, 6 * D_MODEL))


class _Group:
    def __init__(self, nb, s, cond0, per_sequence):
        self.nb, self.s, self.cond0, self.per_sequence = nb, s, cond0, per_sequence
        self.t = nb * s

    def mod_row(self, tm):
        assert self.t % tm == 0
        if not self.per_sequence:
            return lambda i: self.cond0
        assert self.s % tm == 0
        return lambda i: self.cond0 + i // (self.s // tm)


def _inproj_kernel(x_ref, mod_ref, g1_ref, w_ref, z_ref, *kv_refs):
    h = _mod_norm(x_ref[...], g1_ref[...], mod_ref[0, 1:2, :], mod_ref[0, 0:1, :])
    z = _mm(h, w_ref[...])
    z_ref[...] = z
    if kv_refs:
        kv_refs[0][...] = z[:, Z_KA:Z_KA + A_KVW]
        kv_refs[1][...] = z[:, Z_VA:Z_VA + A_KVW]


def _inproj(grp, x, mod_l, g1, w, emit_kv):
    tm = 256
    row = grp.mod_row(tm)
    kv_spec = [pl.BlockSpec((tm, A_KVW), lambda i: (i, 0))] * 2 if emit_kv else []
    kv_shape = [jax.ShapeDtypeStruct((grp.t, A_KVW), f32)] * 2 if emit_kv else []
    return pl.pallas_call(
        _inproj_kernel,
        grid=(grp.t // tm,),
        in_specs=[pl.BlockSpec((tm, D_MODEL), lambda i: (i, 0)),
                  pl.BlockSpec((1, 6, D_MODEL), lambda i: (row(i), 0, 0)),
                  pl.BlockSpec((1, D_MODEL), lambda i: (0, 0)),
                  pl.BlockSpec((D_MODEL, Z_W), lambda i: (0, 0), pipeline_mode=pl.Buffered(1))],
        out_specs=[pl.BlockSpec((tm, Z_W), lambda i: (i, 0))] + kv_spec,
        out_shape=[jax.ShapeDtypeStruct((grp.t, Z_W), f32)] + kv_shape,
        compiler_params=_params(("parallel",)),
        name="inproj",
    )(x, mod_l, g1, w)


def _log_sigmoid(x):
    return jnp.minimum(x, 0.0) - jnp.log1p(jnp.exp(-jnp.abs(x)))


def _col(a, j):
    lane = lax.broadcasted_iota(jnp.int32, a.shape, 1)
    return jnp.sum(jnp.where(lane == j, a, 0.0), axis=1, keepdims=True)


def _mlstm_kernel(qf_ref, kf_ref, vf_ref, gf_ref, qb_ref, kb_ref, vb_ref, gb_ref, bias_ref,
                  c0_ref, n0_ref, m0_ref, hf_ref, hb_ref, c_out, n_out, m_out, ct_s, n_s, m_s):
    c = pl.program_id(1)
    nseq = qf_ref.shape[0]
    nh = 2 * M_HEADS

    @pl.when(c == 0)
    def _():
        for p in range(nseq):
            for i in range(nh):
                ct_s[p * nh + i] = c0_ref[p, i].T
            n_s[p * nh:(p + 1) * nh, :] = n0_ref[p]
            m_s[p * nh:(p + 1) * nh, :] = m0_ref[p]

    ri = lax.broadcasted_iota(jnp.int32, (CHUNK, CHUNK), 0)
    ci = lax.broadcasted_iota(jnp.int32, (CHUNK, CHUNK), 1)
    lane = lax.broadcasted_iota(jnp.int32, (CHUNK, LANES), 1)
    is_forget = ((lane // M_HEADS) % 2) == 1

    refs = ((qf_ref, kf_ref, vf_ref, gf_ref, hf_ref), (qb_ref, kb_ref, vb_ref, gb_ref, hb_ref))
    keeps = (ri >= ci, ri <= ci)
    krow = lax.broadcasted_iota(jnp.int32, (LANES, M_HEADS * LANES), 0)
    head = lax.broadcasted_iota(jnp.int32, (LANES, M_HEADS * LANES), 1) // LANES
    gates = {}
    for p in range(nseq):
        for d in range(2):
            g = refs[d][3][p] + bias_ref[...]
            lg = jnp.where(is_forget, _log_sigmoid(g), g)
            bc = _mm_exact_lhs(keeps[d], lg)
            li_tiles = _mm_exact_rhs(lg, krow == 2 * d * M_HEADS + head)
            lf_tiles = _mm_exact_rhs(bc, krow == (2 * d + 1) * M_HEADS + head)
            gates[p, d] = (lg.T, bc.T, li_tiles, lf_tiles)

    chains = [(p, d, hh) for p in range(nseq) for d in range(2) for hh in range(M_HEADS)]
    st = {}
    for ch in chains:
        p, d, hh = ch
        lgt, bct, li_tiles, lf_tiles = gates[p, d]
        idx = p * nh + d * M_HEADS + hh
        j_li = 2 * d * M_HEADS + hh
        j_lf = j_li + M_HEADS
        sl = slice(hh * M_DH, (hh + 1) * M_DH)
        q = refs[d][0][p, :, sl]
        k = refs[d][1][p, :, sl] * (M_DH ** -0.5)
        v = refs[d][2][p, :, sl]
        b_t = lf_tiles[:, hh * LANES:(hh + 1) * LANES]
        li_t = li_tiles[:, hh * LANES:(hh + 1) * LANES]
        a_t = b_t + m_s[idx:idx + 1, :]
        dm = jnp.where(keeps[d], b_t - bct[j_lf:j_lf + 1, :] + lgt[j_li:j_li + 1, :], -jnp.inf)
        m_t = jnp.maximum(a_t, jnp.max(dm, axis=1, keepdims=True))
        st[ch] = dict(idx=idx, sl=sl, q=q, k=k, v=v, kt=k.T, b_t=b_t, li_t=li_t, m_t=m_t,
                      w=jnp.exp(dm - m_t), w0=jnp.exp(a_t - m_t))
    for ch in chains:
        c_ = st[ch]
        c_["s"] = c_["w"] * _mm(c_["q"], c_["kt"])
        c_["ct"] = ct_s[c_["idx"]]
        c_["qc"] = _mm(c_["q"], c_["ct"])
        c_["n_row"] = n_s[c_["idx"]:c_["idx"] + 1, :]
        c_["qn"] = _mm_nt(c_["q"], jnp.broadcast_to(c_["n_row"], (CHUNK, M_DH)))
    for ch in chains:
        p, d, hh = ch
        c_ = st[ch]
        num = c_["w0"] * c_["qc"] + _mm(c_["s"], c_["v"])
        den = c_["w0"] * c_["qn"] + jnp.sum(c_["s"], axis=1, keepdims=True)
        refs[d][4][p, :, c_["sl"]] = num / jnp.maximum(jnp.abs(den), jnp.exp(-c_["m_t"]))
    for ch in chains:
        p, d, hh = ch
        c_ = st[ch]
        idx = c_["idx"]
        te = 0 if d == 1 else CHUNK - 1
        m_end = c_["m_t"][te:te + 1, :]
        w0_end = c_["w0"][te:te + 1, :]
        w_end = jnp.exp(c_["b_t"][te:te + 1, :] - c_["b_t"] + c_["li_t"] - m_end)
        ct_s[idx] = w0_end * c_["ct"] + _mm(c_["kt"], c_["v"] * w_end)
        n_s[idx:idx + 1, :] = w0_end * c_["n_row"] + jnp.sum(c_["k"] * w_end, axis=0, keepdims=True)
        m_s[idx:idx + 1, :] = m_end

    @pl.when(c == pl.num_programs(1) - 1)
    def _():
        for p in range(nseq):
            for i in range(nh):
                c_out[p, i] = ct_s[p * nh + i].T
            n_out[p] = n_s[p * nh:(p + 1) * nh, :]
            m_out[p] = m_s[p * nh:(p + 1) * nh, :]


MLSTM_SEQS = 2


def _mlstm(z, bias_row, c0, n0, m0, nb, s):
    nc = s // CHUNK
    nh = 2 * M_HEADS
    ps = MLSTM_SEQS
    assert nb % ps == 0
    z3 = z.reshape(nb, s, Z_W)
    fwd = lambda col: (lambda b, c: (b, c, col))
    bwd = lambda col: (lambda b, c: (b, nc - 1 - c, col))
    wide = lambda im: pl.BlockSpec((ps, CHUNK, MIX_W), im)
    gate = lambda im: pl.BlockSpec((ps, CHUNK, LANES), im)
    state = lambda shape: pl.BlockSpec((ps,) + shape, lambda b, c: (b,) + (0,) * len(shape))
    gcol = Z_GM // LANES
    hf, hb, c_new, n_new, m_new = pl.pallas_call(
        _mlstm_kernel,
        grid=(nb // ps, nc),
        in_specs=[wide(fwd(0)), wide(fwd(1)), wide(fwd(2)), gate(fwd(gcol)),
                  wide(bwd(0)), wide(bwd(1)), wide(bwd(2)), gate(bwd(gcol)),
                  pl.BlockSpec((1, LANES), lambda b, c: (0, 0)),
                  state((nh, M_DH, M_DH)), state((nh, M_DH)), state((nh, LANES))],
        out_specs=[wide(fwd(0)), wide(bwd(0)),
                   state((nh, M_DH, M_DH)), state((nh, M_DH)), state((nh, LANES))],
        out_shape=[jax.ShapeDtypeStruct((nb, s, MIX_W), f32), jax.ShapeDtypeStruct((nb, s, MIX_W), f32),
                   jax.ShapeDtypeStruct((nb, nh, M_DH, M_DH), f32), jax.ShapeDtypeStruct((nb, nh, M_DH), f32),
                   jax.ShapeDtypeStruct((nb, nh, LANES), f32)],
        scratch_shapes=[pltpu.VMEM((ps * nh, M_DH, M_DH), f32), pltpu.VMEM((ps * nh, M_DH), f32),
                        pltpu.VMEM((ps * nh, LANES), f32)],
        compiler_params=_params(("parallel", "arbitrary")),
        name="mlstm",
    )(z3, z3, z3, z3, z3, z3, z3, z3, bias_row, c0, n0, m0)
    return hf.reshape(nb * s, MIX_W), hb.reshape(nb * s, MIX_W), c_new, n_new, m_new


def _half_placements(a):
    lane = lax.broadcasted_iota(jnp.int32, a.shape, 1)
    g0 = jnp.where(lane < A_DH, a, 0.0)
    g1 = jnp.where(lane >= A_DH, a, 0.0)
    return ((g0, pltpu.roll(g0, A_DH, 1)), (pltpu.roll(g1, A_DH, 1), g1))


def _group_attend(q, k_all, v_all, sink_ref, l, keep):
    kz = _half_placements(k_all)
    vz = _half_placements(v_all)
    heads = [(head, head // 2, head % 2, head // A_GROUP) for head in range(A_HEADS)]
    scores = [_mm_nt(q[:, slab * LANES:(slab + 1) * LANES], kz[g][pos]) * (A_DH ** -0.5)
              for _, slab, pos, g in heads]
    if keep is not None:
        scores = [jnp.where(keep, s, NEG_INF) for s in scores]
    sinks = [sink_ref[l, head] for head, _, _, _ in heads]
    tops = [jnp.maximum(jnp.max(s, axis=1, keepdims=True), sk) for s, sk in zip(scores, sinks)]
    probs = [jnp.exp(s - m) for s, m in zip(scores, tops)]
    dens = [jnp.sum(p, axis=1, keepdims=True) + jnp.exp(sk - m) for p, sk, m in zip(probs, sinks, tops)]
    outs = [_mm(p, vz[g][pos]) / den for p, den, (_, _, pos, g) in zip(probs, dens, heads)]
    return jnp.concatenate([outs[2 * slab] + outs[2 * slab + 1] for slab in range(A_HEADS // 2)], axis=1)


def _ctx_attn_kernel(l, sink_ref, q_ref, k_ref, v_ref, o_ref):
    o_ref[...] = _group_attend(q_ref[...], k_ref[...], v_ref[...], sink_ref, l, None)


def _ctx_attn(grp, z, sink, l):
    s = grp.s
    return pl.pallas_call(
        functools.partial(_ctx_attn_kernel, l),
        grid=(grp.nb,),
        in_specs=[pl.BlockSpec(memory_space=pltpu.SMEM),
                  pl.BlockSpec((s, MIX_W), lambda b: (b, Z_QA // MIX_W)),
                  pl.BlockSpec((s, A_KVW), lambda b: (b, Z_KA // A_KVW)),
                  pl.BlockSpec((s, A_KVW), lambda b: (b, Z_VA // A_KVW))],
        out_specs=pl.BlockSpec((s, MIX_W), lambda b: (b, 0)),
        out_shape=jax.ShapeDtypeStruct((grp.t, MIX_W), f32),
        compiler_params=_params(("parallel",)),
        name="ctx_attn",
    )(sink, z, z, z)


def _rope(x, cos, sin):
    lane = lax.broadcasted_iota(jnp.int32, cos.shape, 1)
    first = (lane % (2 * ROPE_FREQS)) < ROPE_FREQS
    out = []
    for j in range(x.shape[1] // LANES):
        xs = x[:, j * LANES:(j + 1) * LANES]
        partner = jnp.where(first, pltpu.roll(xs, LANES - ROPE_FREQS, 1), pltpu.roll(xs, ROPE_FREQS, 1))
        out.append(xs * cos + partner * sin)
    return out[0] if len(out) == 1 else jnp.concatenate(out, axis=1)


def _lat_attn_kernel(l, nblk, sink_ref, q_ref, kp_ref, kc_ref, kn_ref, vp_ref, vc_ref, vn_ref,
                     cq_ref, sq_ref, cp_ref, sp_ref, cn_ref, sn_ref, ck_ref, cv_ref, o_ref):
    i = pl.program_id(1)
    q = _rope(q_ref[...], cq_ref[...], sq_ref[...])
    k_all = jnp.concatenate([ck_ref[0, 0],
                             _rope(kp_ref[...], cp_ref[...], sp_ref[...]),
                             _rope(kc_ref[...], cq_ref[...], sq_ref[...]),
                             _rope(kn_ref[...], cn_ref[...], sn_ref[...])], axis=0)
    v_all = jnp.concatenate([cv_ref[0, 0], vp_ref[...], vc_ref[...], vn_ref[...]], axis=0)
    lc = ck_ref.shape[2]
    nk = lc + 3 * CHUNK
    r = lax.broadcasted_iota(jnp.int32, (CHUNK, nk), 0)
    cc = lax.broadcasted_iota(jnp.int32, (CHUNK, nk), 1) - lc
    lo = jnp.maximum(r, jnp.where(i == 0, CHUNK, 0))
    hi = jnp.minimum(r + 2 * CHUNK, jnp.where(i == nblk - 1, 2 * CHUNK - 1, 3 * CHUNK))
    keep = (cc < 0) | ((cc >= lo) & (cc <= hi))
    o_ref[...] = _group_attend(q, k_all, v_all, sink_ref, l, keep)


def _lat_attn(grp, z, sink, l, cache_k, cache_v, cos, sin):
    s, nb = grp.s, grp.nb
    nblk = s // CHUNK
    lc = cache_k.shape[2]

    def rows(off):
        return lambda b, i: b * nblk + jnp.clip(i + off, 0, nblk - 1)

    def zspec(width, col, off):
        rf = rows(off)
        return pl.BlockSpec((CHUNK, width), lambda b, i: (rf(b, i), col))

    def tab(off):
        return pl.BlockSpec((CHUNK, LANES), lambda b, i: (jnp.clip(i + off, 0, nblk - 1), 0))

    cache = pl.BlockSpec((1, 1, lc, A_KVW), lambda b, i: (b, l, 0, 0))
    kcol, vcol = Z_KA // A_KVW, Z_VA // A_KVW
    return pl.pallas_call(
        functools.partial(_lat_attn_kernel, l, nblk),
        grid=(nb, nblk),
        in_specs=[pl.BlockSpec(memory_space=pltpu.SMEM),
                  zspec(MIX_W, Z_QA // MIX_W, 0),
                  zspec(A_KVW, kcol, -1), zspec(A_KVW, kcol, 0), zspec(A_KVW, kcol, 1),
                  zspec(A_KVW, vcol, -1), zspec(A_KVW, vcol, 0), zspec(A_KVW, vcol, 1),
                  tab(0), tab(0), tab(-1), tab(-1), tab(1), tab(1), cache, cache],
        out_specs=pl.BlockSpec((CHUNK, MIX_W), lambda b, i: (b * nblk + i, 0)),
        out_shape=jax.ShapeDtypeStruct((grp.t, MIX_W), f32),
        compiler_params=_params(("parallel", "parallel")),
        name="lat_attn",
    )(sink, z, z, z, z, z, z, z, cos, sin, cos, sin, cos, sin, cache_k, cache_v)


def _rope_tables(s):
    t = jnp.arange(s)
    freqs = ROPE_BASE ** (-jnp.arange(ROPE_FREQS, dtype=f32) / ROPE_FREQS)
    a_row = (t // GRID_W).astype(f32)[:, None] * freqs[None, :]
    a_col = (t % GRID_W).astype(f32)[:, None] * freqs[None, :]
    cos = jnp.concatenate([jnp.cos(a_row)] * 2 + [jnp.cos(a_col)] * 2, axis=1)
    sin = jnp.concatenate([-jnp.sin(a_row), jnp.sin(a_row), -jnp.sin(a_col), jnp.sin(a_col)], axis=1)
    return jnp.concatenate([cos] * A_KV, axis=1), jnp.concatenate([sin] * A_KV, axis=1)


def _merge_kernel(x_ref, mod_ref, g1_ref, g2_ref, hf_ref, hb_ref, om_ref, uc_ref, vc_ref, ya_ref,
                  gm_ref, ws_ref, bs_ref, wmerge_ref, bmerge_ref, wbr_ref, wout_ref, wq_ref, keys_ref,
                  x1_ref, h2t_ref, st_ref):
    x = x_ref[...]
    tm = x.shape[0]
    mod = mod_ref[0]
    h = _mod_norm(x, g1_ref[...], mod[1:2], mod[0:1])
    hm = hf_ref[...] + hb_ref[...]
    ym = jnp.concatenate(
        [_rms(hm[:, hh * M_DH:(hh + 1) * M_DH]) * gm_ref[:, hh * M_DH:(hh + 1) * M_DH] for hh in range(M_HEADS)],
        axis=1) * jax.nn.sigmoid(om_ref[...])
    vr = _rms(vc_ref[...])
    zc = []
    for n in range(tm // CHUNK):
        rs = slice(n * CHUNK, (n + 1) * CHUNK)
        zc.append(jnp.concatenate(
            [_mm(ws_ref[g], vr[rs, g * LANES:(g + 1) * LANES]) + _col(bs_ref[...], g) for g in range(C_GROUPS)],
            axis=1))
    yc = uc_ref[...] * jnp.concatenate(zc, axis=0)
    mixed = jnp.zeros((tm, D_MODEL), f32)
    for n, y in enumerate((ym, yc, ya_ref[...])):
        gate = jax.nn.sigmoid(_mm(h, wmerge_ref[:, n * D_MODEL:(n + 1) * D_MODEL])
                              + bmerge_ref[:, n * D_MODEL:(n + 1) * D_MODEL])
        mixed = mixed + gate * _mm(y, wbr_ref[n])
    x1 = x + mod[2:3] * _mm(mixed, wout_ref[...])
    x1_ref[...] = x1
    h2 = _mod_norm(x1, g2_ref[...], mod[4:5], mod[3:4])
    h2t_ref[...] = h2.T.astype(h2t_ref.dtype)
    qp = _mm(h2, wq_ref[...])
    for hp in range(2 * PEER_HEADS):
        st_ref[hp] = _mm_nt(keys_ref[hp], qp[:, hp * N_KEYS:(hp + 1) * N_KEYS])


def _merge(grp, x, mod_l, g1, g2, hf, hb, z, ya, gm, ws, bs_t, wmerge, bmerge, wbr, wout, wq, keys):
    tm = 256
    row = grp.mod_row(tm)
    tok = lambda w, col=0: pl.BlockSpec((tm, w), lambda i: (i, col))
    full = lambda a: pl.BlockSpec(a.shape, lambda i: (0,) * a.ndim, pipeline_mode=pl.Buffered(1))
    return pl.pallas_call(
        _merge_kernel,
        grid=(grp.t // tm,),
        in_specs=[tok(D_MODEL), pl.BlockSpec((1, 6, D_MODEL), lambda i: (row(i), 0, 0)), full(g1), full(g2),
                  tok(MIX_W), tok(MIX_W), tok(MIX_W, Z_OM // MIX_W), tok(MIX_W, Z_UC // MIX_W),
                  tok(MIX_W, Z_VC // MIX_W), tok(MIX_W),
                  full(gm), full(ws), full(bs_t), full(wmerge), full(bmerge), full(wbr), full(wout), full(wq),
                  full(keys)],
        out_specs=[tok(D_MODEL), pl.BlockSpec((D_MODEL, tm), lambda i: (0, i)),
                   pl.BlockSpec((2 * PEER_HEADS, N_KEYS, tm), lambda i: (0, 0, i))],
        out_shape=[jax.ShapeDtypeStruct((grp.t, D_MODEL), f32), jax.ShapeDtypeStruct((D_MODEL, grp.t), MXU_DTYPE),
                   jax.ShapeDtypeStruct((2 * PEER_HEADS, N_KEYS, grp.t), f32)],
        compiler_params=_params(("parallel",)),
        name="merge",
    )(x, mod_l, g1, g2, hf, hb, z, z, z, ya, gm, ws, bs_t, wmerge, bmerge, wbr, wout, wq, keys)


_CELLS = [(i, j) for i in range(PEER_TOPK) for j in range(PEER_TOPK) if (i + 1) * (j + 1) <= PEER_TOPK]


def _route_kernel(st_ref, r2_ref, b_ref, jd_ref, a_ref, val_s, rank_s, work_s):
    tb = st_ref.shape[2]
    nhp = 2 * PEER_HEADS
    unranked = jnp.full((N_KEYS, tb), float(PEER_TOPK), f32)

    def extract(hp, r, hit_of, rank_value=None):
        s = work_s[hp]
        m = jnp.max(s, axis=0, keepdims=True)
        hit = hit_of(s, m)
        val_s[hp % 2, r, pl.ds(hp // 2, 1), :] = m
        if rank_value is not None:
            rank_s[hp] = jnp.where(hit, rank_value, rank_s[hp])
        work_s[hp] = jnp.where(hit, -jnp.inf, s)

    def finish(ranked):
        v1 = [val_s[0, i] for i in range(PEER_TOPK)]
        v2 = [val_s[1, i] for i in range(PEER_TOPK)]
        cand = [v1[i] + v2[j] for (i, j) in _CELLS]
        ncell = len(_CELLS)
        wins = [jnp.zeros((PEER_HEADS, tb), f32) for _ in range(ncell)]
        losses = [jnp.zeros((PEER_HEADS, tb), f32) for _ in range(ncell)]
        for x in range(ncell):
            for y in range(x + 1, ncell):
                x_first = jnp.where(cand[x] >= cand[y], 1.0, 0.0)
                wins[x] = wins[x] + x_first
                losses[y] = losses[y] + x_first
        before = [losses[x] + (float(ncell - 1 - x) - wins[x]) for x in range(ncell)]
        ea = [jnp.exp(v1[i] - v1[0]) for i in range(PEER_TOPK)]
        eb = [jnp.exp(v2[j] - v2[0]) for j in range(PEER_TOPK)]
        jcount = [jnp.zeros((PEER_HEADS, tb), f32) for _ in range(PEER_TOPK)]
        zsum = jnp.zeros((PEER_HEADS, tb), f32)
        for x, (i, j) in enumerate(_CELLS):
            sel = (before[x] < PEER_TOPK).astype(f32)
            jcount[i] = jcount[i] + sel
            zsum = zsum + sel * (ea[i] * eb[j])
        inv_z = 1.0 / zsum

        for h in range(PEER_HEADS):
            s1 = st_ref[2 * h]
            s2 = st_ref[2 * h + 1]
            jd = jnp.zeros((N_KEYS, tb), f32)
            if ranked:
                rank1 = rank_s[2 * h]
                for i in range(PEER_TOPK):
                    jd = jnp.where(rank1 == float(i), jcount[i][h:h + 1, :], jd)
                r2 = rank_s[2 * h + 1]
            else:
                r2 = unranked
                for i in range(PEER_TOPK):
                    jd = jnp.where(s1 == v1[i][h:h + 1, :], jcount[i][h:h + 1, :], jd)
                    r2 = jnp.where(s2 == v2[i][h:h + 1, :], float(i), r2)
            jd_ref[h] = _pair_words(jd)
            a_ref[h] = _pair_words(jnp.exp(s1 - v1[0][h:h + 1, :]) * inv_z[h:h + 1, :])
            b_ref[h] = jnp.exp(s2 - v2[0][h:h + 1, :]).astype(b_ref.dtype)
            r2_ref[h] = r2.astype(r2_ref.dtype)

    work_s[...] = st_ref[...]

    def fast_round(r, _):
        for hp in range(nhp):
            extract(hp, r, lambda s, m: s == m)
        return 0

    lax.fori_loop(0, PEER_TOPK, fast_round, 0)
    removed = jnp.zeros((1, tb), f32)
    for hp in range(nhp):
        removed = jnp.maximum(removed, jnp.sum(jnp.where(work_s[hp] == -jnp.inf, 1.0, 0.0), axis=0, keepdims=True))
    tied = jnp.max(removed) > float(PEER_TOPK)

    @pl.when(jnp.logical_not(tied))
    def _():
        finish(ranked=False)

    @pl.when(tied)
    def _():
        key = lax.broadcasted_iota(jnp.int32, (N_KEYS, tb), 0).astype(f32)

        def lowest_index_hit(s, m):
            return key == jnp.min(jnp.where(s == m, key, float(N_KEYS)), axis=0, keepdims=True)

        work_s[...] = st_ref[...]
        for hp in range(nhp):
            rank_s[hp] = unranked

        def exact_round(i, _):
            r = i % PEER_TOPK
            extract(i // PEER_TOPK, r, lowest_index_hit, lax.convert_element_type(r, f32))
            return 0

        lax.fori_loop(0, nhp * PEER_TOPK, exact_round, 0)
        finish(ranked=True)


def _route(st):
    t = st.shape[2]
    tb = LANES
    out = [jax.ShapeDtypeStruct((PEER_HEADS, N_KEYS, t), dt)
           for dt in (GATE_DTYPE, GATE_DTYPE, jnp.uint32, jnp.uint32)]
    spec = pl.BlockSpec((PEER_HEADS, N_KEYS, tb), lambda i: (0, 0, i))
    return pl.pallas_call(
        _route_kernel,
        grid=(t // tb,),
        in_specs=[pl.BlockSpec((2 * PEER_HEADS, N_KEYS, tb), lambda i: (0, 0, i))],
        out_specs=[spec] * 4,
        out_shape=out,
        scratch_shapes=[pltpu.VMEM((2, PEER_TOPK, PEER_HEADS, tb), f32),
                        pltpu.VMEM((2 * PEER_HEADS, N_KEYS, tb), f32),
                        pltpu.VMEM((2 * PEER_HEADS, N_KEYS, tb), f32)],
        compiler_params=_params(("parallel",)),
        name="peer_route",
    )(st)


PEER_TT = 512
PEER_TW = 512
PEER_E1 = 16
PEER_SUBS = (4, 4, 4, 4)
PEER_ROWS = 16


def _pair_words(x):
    u = lax.bitcast_convert_type(x.astype(jnp.bfloat16).astype(f32), jnp.uint32)
    return u | (u >> 16)


def _bcast_pair_row(row):
    assert PEER_ROWS == 16
    return pltpu.bitcast(jnp.broadcast_to(row, (PEER_ROWS // 2, row.shape[1])), jnp.bfloat16)


def _peer_kernel(final, h2t_ref, r2_ref, b_ref, jd_ref, a_ref, u_ref, vt_ref, x1_ref, mod_ref, gf_ref,
                 o_ref, acc_s, a_s, h_s):
    j = pl.program_id(1)

    @pl.when(j == 0)
    def _():
        acc_s[...] = jnp.zeros_like(acc_s)

    assert sum(PEER_SUBS) == PEER_E1
    starts = [sum(PEER_SUBS[:sb]) for sb in range(len(PEER_SUBS))]
    n_sub = len(PEER_SUBS)
    srow = lambda sb: slice(starts[sb] * N_KEYS, (starts[sb] + PEER_SUBS[sb]) * N_KEYS)
    units = [(sb, pl.ds(w * PEER_TW, PEER_TW)) for w in range(PEER_TT // PEER_TW) for sb in range(n_sub)]

    def hdot(i):
        sb, cols = units[i]
        h_s[i % 2, :PEER_SUBS[sb] * N_KEYS, :] = jnp.dot(u_ref[srow(sb), :], h2t_ref[:, cols],
                                                        preferred_element_type=f32).astype(h_s.dtype)

    def accumulate(i):
        sb, cols = units[i]
        acc_s[:, cols] += jnp.dot(vt_ref[:, srow(sb)], a_s[srow(sb), cols], preferred_element_type=f32)

    hdot(0)
    for i, (sb, cols) in enumerate(units):
        h = h_s.at[i % 2]
        if i + 1 < len(units):
            hdot(i + 1)
        if i >= 1:
            accumulate(i - 1)
        for e in range(PEER_SUBS[sb]):
            e1 = starts[sb] + e
            n_chunk = N_KEYS // PEER_ROWS
            gates = [None] * n_chunk
            for hh in range(PEER_HEADS):
                jd = _bcast_pair_row(jd_ref[hh, e1:e1 + 1, cols])
                aa = _bcast_pair_row(a_ref[hh, e1:e1 + 1, cols])
                for r in range(n_chunk):
                    rows = slice(r * PEER_ROWS, (r + 1) * PEER_ROWS)
                    term = jnp.where(r2_ref[hh, rows, cols] < jd, b_ref[hh, rows, cols],
                                     jnp.zeros((), GATE_DTYPE)) * aa
                    gates[r] = term if gates[r] is None else gates[r] + term
            for r in range(n_chunk):
                lo = e * N_KEYS + r * PEER_ROWS
                act = gates[r] * jax.nn.gelu(h[lo:lo + PEER_ROWS, :])
                a_s[pl.ds(e1 * N_KEYS + r * PEER_ROWS, PEER_ROWS), cols] = act.astype(a_s.dtype)
    accumulate(len(units) - 1)

    @pl.when(j == pl.num_programs(1) - 1)
    def _():
        x2 = x1_ref[...] + mod_ref[0, 5:6, :] * acc_s[...].T
        o_ref[...] = _rms(x2) * gf_ref[...] if final else x2


def _tables_kernel(u_ref, v_ref, ub_ref, vtb_ref):
    ub_ref[0] = u_ref[0].astype(ub_ref.dtype)
    vtb_ref[0] = v_ref[0].T.astype(vtb_ref.dtype)


def _prep_tables(peer_u, peer_v):
    rows = 512
    src = pl.BlockSpec((1, rows, D_MODEL), lambda l, i: (l, i, 0))
    return pl.pallas_call(
        _tables_kernel,
        grid=(DEPTH, N_EXPERTS // rows),
        in_specs=[src, src],
        out_specs=[src, pl.BlockSpec((1, D_MODEL, rows), lambda l, i: (l, 0, i))],
        out_shape=[jax.ShapeDtypeStruct((DEPTH, N_EXPERTS, D_MODEL), MXU_DTYPE),
                   jax.ShapeDtypeStruct((DEPTH, D_MODEL, N_EXPERTS), MXU_DTYPE)],
        compiler_params=_params(("parallel", "parallel")),
        name="peer_tables",
    )(peer_u, peer_v)


def _peer(grp, h2t, r2, b, jd, a, u, vt, l, x1, mod_l, g_final, final):
    tt = PEER_TT
    eb = PEER_E1 * N_KEYS
    row = grp.mod_row(tt)
    gate = pl.BlockSpec((PEER_HEADS, N_KEYS, tt), lambda i, j: (0, 0, i))
    gate1 = pl.BlockSpec((PEER_HEADS, PEER_E1, tt), lambda i, j: (0, j, i))
    return pl.pallas_call(
        functools.partial(_peer_kernel, final),
        grid=(grp.t // tt, N_EXPERTS // eb),
        in_specs=[pl.BlockSpec((D_MODEL, tt), lambda i, j: (0, i)), gate, gate, gate1, gate1,
                  pl.BlockSpec((None, eb, D_MODEL), lambda i, j: (l, j, 0)),
                  pl.BlockSpec((None, D_MODEL, eb), lambda i, j: (l, 0, j)),
                  pl.BlockSpec((tt, D_MODEL), lambda i, j: (i, 0)),
                  pl.BlockSpec((1, 6, D_MODEL), lambda i, j: (row(i), 0, 0)),
                  pl.BlockSpec((1, D_MODEL), lambda i, j: (0, 0))],
        out_specs=pl.BlockSpec((tt, D_MODEL), lambda i, j: (i, 0)),
        out_shape=jax.ShapeDtypeStruct((grp.t, D_MODEL), f32),
        scratch_shapes=[pltpu.VMEM((D_MODEL, tt), f32), pltpu.VMEM((eb, tt), MXU_DTYPE),
                        pltpu.VMEM((2, max(PEER_SUBS) * N_KEYS, PEER_TW), GATE_DTYPE)],
        compiler_params=_params(("parallel", "arbitrary")),
        name="peer",
    )(h2t, r2, b, jd, a, u, vt, x1, mod_l, g_final)


def _reorder_w_in(w_in_l):
    ng = 4 * M_HEADS
    a = w_in_l[:, :4 * MIX_W]
    g = w_in_l[:, 4 * MIX_W:4 * MIX_W + ng]
    rest = w_in_l[:, 4 * MIX_W + ng:]
    pad = jnp.zeros((D_MODEL, LANES - ng), w_in_l.dtype)
    return jnp.concatenate([a, rest, g, pad], axis=1).astype(MXU_DTYPE)


def _forward(x_prompt, x_sample, cache_k, cache_v, state_C, state_n, state_m, c, c_ctx,
             w_ada, b_ada, g_norm1, g_norm2, w_in, b_gates_m, g_mlstm, w_spatial, b_spatial, sink,
             w_branch, w_merge, b_merge, w_out, w_peer_q, peer_keys, peer_u, peer_v, g_final):
    nb_c, s_c, _ = x_prompt.shape
    nb_l, s_l, _ = x_sample.shape
    ctx = _Group(nb_c, s_c, 0, False)
    lat = _Group(nb_l, s_l, 1, True)
    lc = cache_k.shape[2]
    nh = 2 * M_HEADS

    cond = jnp.concatenate([c_ctx[None, :], c], axis=0)
    cond = jnp.pad(cond, ((0, (-cond.shape[0]) % 8), (0, 0)))
    mod = _ada(cond, w_ada, b_ada).reshape(DEPTH, cond.shape[0], 6, D_MODEL)
    xc = x_prompt.reshape(ctx.t, D_MODEL)
    xl = x_sample.reshape(lat.t, D_MODEL)
    cos, sin = _rope_tables(s_l)
    ck = cache_k.reshape(nb_l, DEPTH, lc, A_KVW)
    cv = cache_v.reshape(nb_l, DEPTH, lc, A_KVW)
    zero_c = jnp.zeros((nb_c, nh, M_DH, M_DH), f32)
    zero_n = jnp.zeros((nb_c, nh, M_DH), f32)
    zero_m = jnp.zeros((nb_c, nh, LANES), f32)
    gfin = g_final.reshape(1, D_MODEL)
    u, vt = _prep_tables(peer_u, peer_v)

    ks, vs, cs, ns, ms = [], [], [], [], []
    for l in range(DEPTH):
        g1 = g_norm1[l].reshape(1, D_MODEL)
        g2 = g_norm2[l].reshape(1, D_MODEL)
        w_in_l = _reorder_w_in(w_in[l])
        bias_row = jnp.pad(b_gates_m[l].reshape(1, 4 * M_HEADS), ((0, 0), (0, LANES - 4 * M_HEADS)))
        merge_w = (g_mlstm[l].reshape(1, MIX_W), w_spatial[l].astype(MXU_DTYPE),
                   jnp.pad(b_spatial[l].T, ((0, 0), (0, LANES - C_GROUPS))), w_merge[l].astype(MXU_DTYPE),
                   b_merge[l].reshape(1, 3 * D_MODEL), w_branch[l].astype(MXU_DTYPE), w_out[l].astype(MXU_DTYPE),
                   w_peer_q[l].astype(MXU_DTYPE),
                   peer_keys[l].reshape(2 * PEER_HEADS, N_KEYS, N_KEYS).astype(MXU_DTYPE))
        final = l == DEPTH - 1

        zc, k_new, v_new = _inproj(ctx, xc, mod[l], g1, w_in_l, True)
        hf, hb, c_new, n_new, m_new = _mlstm(zc, bias_row, zero_c, zero_n, zero_m, nb_c, s_c)
        ya = _ctx_attn(ctx, zc, sink, l)
        x1, h2t, st = _merge(ctx, xc, mod[l], g1, g2, hf, hb, zc, ya, *merge_w)
        xc = _peer(ctx, h2t, *_route(st), u, vt, l, x1, mod[l], gfin, final)
        ks.append(k_new.reshape(nb_c, s_c, A_KV, A_DH))
        vs.append(v_new.reshape(nb_c, s_c, A_KV, A_DH))
        cs.append(c_new.reshape(nb_c, 2, M_HEADS, M_DH, M_DH))
        ns.append(n_new.reshape(nb_c, 2, M_HEADS, M_DH))
        ms.append(m_new[:, :, 0].reshape(nb_c, 2, M_HEADS))

        zl = _inproj(lat, xl, mod[l], g1, w_in_l, False)[0]
        m0 = jnp.broadcast_to(state_m[:, l].reshape(nb_l, nh, 1), (nb_l, nh, LANES))
        hf, hb, _, _, _ = _mlstm(zl, bias_row, state_C[:, l].reshape(nb_l, nh, M_DH, M_DH),
                                 state_n[:, l].reshape(nb_l, nh, M_DH), m0, nb_l, s_l)
        ya = _lat_attn(lat, zl, sink, l, ck, cv, cos, sin)
        x1, h2t, st = _merge(lat, xl, mod[l], g1, g2, hf, hb, zl, ya, *merge_w)
        xl = _peer(lat, h2t, *_route(st), u, vt, l, x1, mod[l], gfin, final)
    return (xc.reshape(nb_c, s_c, D_MODEL), xl.reshape(nb_l, s_l, D_MODEL),
            jnp.stack(ks, axis=1), jnp.stack(vs, axis=1), jnp.stack(cs, axis=1), jnp.stack(ns, axis=1),
            jnp.stack(ms, axis=1))


def kernel(x_prompt, x_sample, cache_k, cache_v, state_C, state_n, state_m, c, c_ctx, w_ada, b_ada, g_norm1, g_norm2, w_in, b_gates_m, g_mlstm, w_spatial, b_spatial, sink, w_branch, w_merge, b_merge, w_out, w_peer_q, peer_keys, peer_u, peer_v, g_final):
    return _forward(x_prompt, x_sample, cache_k, cache_v, state_C, state_n, state_m, c, c_ctx, w_ada, b_ada,
                    g_norm1, g_norm2, w_in, b_gates_m, g_mlstm, w_spatial, b_spatial, sink, w_branch, w_merge,
                    b_merge, w_out, w_peer_q, peer_keys, peer_u, peer_v, g_final)
```
